```python
import math
import jax, jax.numpy as jnp
from jax import lax
import numpy as np

D_MODEL = 2048
BATCH = 4
SEQ = 2048
DEPTH = 4
DEC_BATCH = 128
DEC_SEQ = 1
PAST_LEN = 16384
PAGE_SIZE = 128

N_AB = (DEPTH + 1) // 2
N_C = DEPTH // 2
D_A = D_MODEL // 2
GS_A = 16
G_A = D_A // GS_A
N_A = 64
D_B = D_MODEL // 2
K_B = 128
H_B = D_B // K_B
V_B = D_B // H_B
D_IN_AB = D_A + 4 * D_B
CHUNK_B = 64
D_RNN = (D_MODEL * 4 // 3) // 256 * 256
NB_C = 16
BS_C = D_RNN // NB_C
CONV_W = 4
LRU_C = 8.0
D_FF = (D_MODEL * 8 // 3) // 128 * 128
N_MOD = 9
EPS = 1e-6

kernel_name = "hybrid_s5_hgrn2_rglru_decode_step"


def rmsnorm(x, w):
    xf = x.astype(jnp.float32)
    y = xf * lax.rsqrt(jnp.mean(xf * xf, axis=-1, keepdims=True) + EPS)
    return (y * w.astype(jnp.float32)).astype(x.dtype)


def modulate(h, shift, scale):
    return h * (1.0 + scale) + shift


def swiglu(h, w_gu, w_d):
    gate, up = jnp.split(h @ w_gu, 2, axis=-1)
    return (jax.nn.silu(gate) * up) @ w_d


def cmul(ar, ai, br, bi):
    return ar * br - ai * bi, ar * bi + ai * br


def complex_linear_scan(ar, ai, br, bi):
    def combine(e1, e2):
        a1r, a1i, b1r, b1i = e1
        a2r, a2i, b2r, b2i = e2
        nar, nai = cmul(a2r, a2i, a1r, a1i)
        tr, ti = cmul(a2r, a2i, b1r, b1i)
        return nar, nai, tr + b2r, ti + b2i
    return lax.associative_scan(combine, (ar, ai, br, bi), axis=1)


def real_linear_scan(a, b):
    def combine(e1, e2):
        a1, b1 = e1
        a2, b2 = e2
        return a1 * a2, a2 * b1 + b2
    return lax.associative_scan(combine, (a, b), axis=1)[1]


def s5_mixer(u, h0_re, h0_im, lam_re, lam_im, b_re, b_im, c_re, c_im, d_skip, log_step, w_glu, b_glu):
    f32 = jnp.float32
    Bsz, L, _ = u.shape
    uf = u.astype(f32).reshape(Bsz, L, G_A, GS_A)
    step = jnp.exp(log_step.astype(f32))[:, None]
    lr, li = lam_re.astype(f32), lam_im.astype(f32)
    mag = jnp.exp(lr * step)
    abar_r, abar_i = mag * jnp.cos(li * step), mag * jnp.sin(li * step)
    den = lr * lr + li * li
    pr, pim = abar_r - 1.0, abar_i
    zr = (pr * lr + pim * li) / den
    zi = (pim * lr - pr * li) / den
    bbr, bbi = cmul(zr[..., None], zi[..., None], b_re.astype(f32), b_im.astype(f32))
    xr = jnp.einsum("blgk,gnk->blgn", uf, bbr)
    xi = jnp.einsum("blgk,gnk->blgn", uf, bbi)
    ir, ii = cmul(abar_r, abar_i, h0_re.astype(f32), h0_im.astype(f32))
    xr = xr.at[:, 0].add(ir)
    xi = xi.at[:, 0].add(ii)
    ar = jnp.broadcast_to(abar_r, xr.shape)
    ai = jnp.broadcast_to(abar_i, xr.shape)
    _, _, hr, hi = complex_linear_scan(ar, ai, xr, xi)
    y = (jnp.einsum("blgn,gkn->blgk", hr, c_re.astype(f32))
         - jnp.einsum("blgn,gkn->blgk", hi, c_im.astype(f32)))
    y = y.reshape(Bsz, L, D_A) + d_skip.astype(f32) * u.astype(f32)
    y = jax.nn.gelu(y).astype(u.dtype)
    out = y * jax.nn.sigmoid(y @ w_glu + b_glu)
    return out, hr[:, -1].astype(h0_re.dtype), hi[:, -1].astype(h0_im.dtype)


def hgrn2_mixer(q, fz, iv, g, s0, lb, gnorm_w):
    f32 = jnp.float32
    Bsz, L, _ = q.shape
    lbf = lb.astype(f32)
    log_f = jnp.logaddexp(jnp.log(lbf), jnp.log1p(-lbf) + jax.nn.log_sigmoid(fz.astype(f32)))
    k = -jnp.expm1(log_f)
    C = min(CHUNK_B, L)
    n_chunks = -(-L // C)
    pad = n_chunks * C - L

    def to_chunks(t):
        t = jnp.pad(t.astype(f32), ((0, 0), (0, pad), (0, 0)))
        return t.reshape(Bsz, n_chunks, C, H_B, -1).transpose(1, 0, 3, 2, 4)

    qc, kc, vc, gc = to_chunks(q), to_chunks(k), to_chunks(iv), to_chunks(log_f)
    causal = jnp.tril(jnp.ones((C, C), dtype=bool))[:, :, None]

    def chunk_step(S, inp):
        qb, kb, vb, lfb = inp
        G = jnp.cumsum(lfb, axis=2)
        diff = G[:, :, :, None, :] - G[:, :, None, :, :]
        decay = jnp.exp(jnp.where(causal, diff, -jnp.inf))
        scores = jnp.einsum("bhtk,bhsk,bhtsk->bhts", qb, kb, decay)
        o = (jnp.einsum("bhts,bhsv->bhtv", scores, vb)
             + jnp.einsum("bhtk,bhkv->bhtv", qb * jnp.exp(G), S))
        G_last = G[:, :, -1:, :]
        S_new = (jnp.exp(G_last[:, :, 0, :, None]) * S
                 + jnp.einsum("bhsk,bhsv->bhkv", kb * jnp.exp(G_last - G), vb))
        return S_new, o

    S_fin, o = lax.scan(chunk_step, s0.astype(f32), (qc, kc, vc, gc))
    o = o.transpose(1, 0, 3, 2, 4).reshape(Bsz, n_chunks * C, H_B, V_B)[:, :L]
    o = o * lax.rsqrt(jnp.mean(o * o, axis=-1, keepdims=True) + EPS) * gnorm_w.astype(f32)
    o = o * jax.nn.silu(g.astype(f32).reshape(Bsz, L, H_B, V_B))
    return o.reshape(Bsz, L, D_B).astype(q.dtype), S_fin.astype(s0.dtype)


def block_diag(x, w, b):
    xb = x.reshape(x.shape[0], x.shape[1], NB_C, BS_C)
    return jnp.einsum("blnj,njk->blnk", xb, w).reshape(x.shape) + b


def rglru_block(h, conv_buf, h0, w_in_c, conv_w, conv_b, w_ga, b_ga, w_gx, b_gx, lru_lambda, w_out_c):
    f32 = jnp.float32
    gate_br, xr = jnp.split(h @ w_in_c, 2, axis=-1)
    L = xr.shape[1]
    xp = jnp.concatenate([conv_buf.astype(xr.dtype), xr], axis=1)
    xc = conv_b + xp[:, 0:L] * conv_w[0]
    for j in range(1, CONV_W):
        xc = xc + xp[:, j:j + L] * conv_w[j]
    new_buf = xp[:, L:]
    r = jax.nn.sigmoid(block_diag(xc, w_ga, b_ga).astype(f32))
    ig = jax.nn.sigmoid(block_diag(xc, w_gx, b_gx).astype(f32))
    log_a = LRU_C * r * jax.nn.log_sigmoid(lru_lambda.astype(f32))
    a = jnp.exp(log_a)
    bterm = jnp.sqrt(-jnp.expm1(2.0 * log_a)) * ig * xc.astype(f32)
    bterm = bterm.at[:, 0].add(a[:, 0] * h0.astype(f32))
    hs = real_linear_scan(a, bterm)
    y = (jax.nn.gelu(gate_br.astype(f32)) * hs).astype(h.dtype) @ w_out_c
    return y, new_buf.astype(conv_buf.dtype), hs[:, -1].astype(h0.dtype)


def run_group(x, c, s5_re0, s5_im0, hgrn0, lru0, conv0, P):
    Bsz = x.shape[0]
    lb_all = jnp.cumsum(jax.nn.softmax(P["hg_lb_logits"].astype(jnp.float32), axis=0), axis=0)
    lb_all = lb_all - lb_all[0:1]
    sc = jax.nn.silu(c)
    s5r_l, s5i_l, hg_l, lru_l, conv_l = [], [], [], [], []
    for l in range(DEPTH):
        j = l // 2
        m = (sc @ P["w_ada"][l] + P["b_ada"][l]).reshape(Bsz, N_MOD, 1, D_MODEL)
        h = modulate(rmsnorm(x, P["norm_w"][l, 0]), m[:, 0], m[:, 1])
        x = x + 0.5 * m[:, 2] * swiglu(h, P["w_ffn_gu"][l, 0], P["w_ffn_d"][l, 0])
        h = modulate(rmsnorm(x, P["norm_w"][l, 1]), m[:, 3], m[:, 4])
        if l % 2 == 0:
            z = h @ P["w_in_ab"][j]
            u, q, fz, iv, g = jnp.split(z, [D_A, D_A + D_B, D_A + 2 * D_B, D_A + 3 * D_B], axis=-1)
            ya, s5r, s5i = s5_mixer(u, s5_re0[j], s5_im0[j], P["s5_lam_re"][j], P["s5_lam_im"][j],
                                    P["s5_b_re"][j], P["s5_b_im"][j], P["s5_c_re"][j], P["s5_c_im"][j],
                                    P["s5_d"][j], P["s5_log_step"][j], P["s5_w_glu"][j], P["s5_b_glu"][j])
            yb, hgS = hgrn2_mixer(q, fz, iv, g, hgrn0[j], lb_all[j], P["hg_norm_w"][j])
            mix = jnp.concatenate([ya, yb], axis=-1) @ P["w_out_ab"][j]
            s5r_l.append(s5r)
            s5i_l.append(s5i)
            hg_l.append(hgS)
        else:
            mix, cb, hl = rglru_block(h, conv0[j], lru0[j], P["w_in_c"][j], P["conv_w"][j], P["conv_b"][j],
                                      P["w_gate_a"][j], P["b_gate_a"][j], P["w_gate_x"][j], P["b_gate_x"][j],
                                      P["lru_lambda"][j], P["w_out_c"][j])
            conv_l.append(cb)
            lru_l.append(hl)
        x = x + m[:, 5] * mix
        h = modulate(rmsnorm(x, P["norm_w"][l, 2]), m[:, 6], m[:, 7])
        x = x + 0.5 * m[:, 8] * swiglu(h, P["w_ffn_gu"][l, 1], P["w_ffn_d"][l, 1])
    y = rmsnorm(x, P["final_norm_w"])
    return y, jnp.stack(s5r_l), jnp.stack(s5i_l), jnp.stack(hg_l), jnp.stack(lru_l), jnp.stack(conv_l)


def setup_inputs(seed: int = 0) -> dict:
    key = jax.random.key(seed)
    ks = iter(jax.random.split(key, 64))
    f32 = jnp.float32

    def nrm(shape, scale):
        return jax.random.normal(next(ks), shape, f32) * scale

    n_idx = jnp.arange(N_A, dtype=f32)
    a0 = jax.random.uniform(next(ks), (N_C, D_RNN), f32, 0.9, 0.999)
    p = a0 ** (1.0 / LRU_C)
    return {
        "x_prompt": nrm((BATCH, SEQ, D_MODEL), 1.0),
        "x_sample": nrm((DEC_BATCH, DEC_SEQ, D_MODEL), 1.0),
        "state_s5_re": nrm((N_AB, DEC_BATCH, G_A, N_A), 0.1),
        "state_s5_im": nrm((N_AB, DEC_BATCH, G_A, N_A), 0.1),
        "state_hgrn": nrm((N_AB, DEC_BATCH, H_B, K_B, V_B), 0.5),
        "state_lru": nrm((N_C, DEC_BATCH, D_RNN), 0.5),
        "state_conv": nrm((N_C, DEC_BATCH, CONV_W - 1, D_RNN), 1.0),
        "c_prompt": nrm((BATCH, D_MODEL), 1.0),
        "c_sample": nrm((DEC_BATCH, D_MODEL), 1.0),
        "norm_w": 1.0 + nrm((DEPTH, 3, D_MODEL), 0.02),
        "final_norm_w": 1.0 + nrm((D_MODEL,), 0.02),
        "w_ada": nrm((DEPTH, D_MODEL, N_MOD * D_MODEL), 0.5 * D_MODEL ** -0.5),
        "b_ada": nrm((DEPTH, N_MOD * D_MODEL), 0.02),
        "w_ffn_gu": nrm((DEPTH, 2, D_MODEL, 2 * D_FF), D_MODEL ** -0.5),
        "w_ffn_d": nrm((DEPTH, 2, D_FF, D_MODEL), D_FF ** -0.5),
        "w_in_ab": nrm((N_AB, D_MODEL, D_IN_AB), D_MODEL ** -0.5),
        "s5_lam_re": -0.5 * (1.0 + nrm((N_AB, G_A, N_A), 0.01)),
        "s5_lam_im": math.pi * n_idx + nrm((N_AB, G_A, N_A), 0.01),
        "s5_b_re": nrm((N_AB, G_A, N_A, GS_A), (2 * GS_A) ** -0.5),
        "s5_b_im": nrm((N_AB, G_A, N_A, GS_A), (2 * GS_A) ** -0.5),
        "s5_c_re": nrm((N_AB, G_A, GS_A, N_A), N_A ** -0.5),
        "s5_c_im": nrm((N_AB, G_A, GS_A, N_A), N_A ** -0.5),
        "s5_d": nrm((N_AB, D_A), 0.5),
        "s5_log_step": jax.random.uniform(next(ks), (N_AB, G_A), f32, math.log(1e-3), math.log(1e-1)),
        "s5_w_glu": nrm((N_AB, D_A, D_A), D_A ** -0.5),
        "s5_b_glu": nrm((N_AB, D_A), 0.02),
        "hg_lb_logits": nrm((N_AB, D_B), 0.1),
        "hg_norm_w": 1.0 + nrm((N_AB, V_B), 0.02),
        "w_out_ab": nrm((N_AB, D_A + D_B, D_MODEL), (D_A + D_B) ** -0.5),
        "w_in_c": nrm((N_C, D_MODEL, 2 * D_RNN), D_MODEL ** -0.5),
        "conv_w": nrm((N_C, CONV_W, D_RNN), CONV_W ** -0.5),
        "conv_b": nrm((N_C, D_RNN), 0.02),
        "w_gate_a": nrm((N_C, NB_C, BS_C, BS_C), BS_C ** -0.5),
        "b_gate_a": nrm((N_C, D_RNN), 0.02),
        "w_gate_x": nrm((N_C, NB_C, BS_C, BS_C), BS_C ** -0.5),
        "b_gate_x": nrm((N_C, D_RNN), 0.02),
        "lru_lambda": jnp.log(p) - jnp.log1p(-p),
        "w_out_c": nrm((N_C, D_RNN, D_MODEL), D_RNN ** -0.5),
    }


def reference(x_prompt, x_sample, state_s5_re, state_s5_im, state_hgrn, state_lru, state_conv,
              c_prompt, c_sample, norm_w, final_norm_w, w_ada, b_ada, w_ffn_gu, w_ffn_d,
              w_in_ab, s5_lam_re, s5_lam_im, s5_b_re, s5_b_im, s5_c_re, s5_c_im, s5_d, s5_log_step,
              s5_w_glu, s5_b_glu, hg_lb_logits, hg_norm_w, w_out_ab, w_in_c, conv_w, conv_b,
              w_gate_a, b_gate_a, w_gate_x, b_gate_x, lru_lambda, w_out_c):
    P = dict(norm_w=norm_w, final_norm_w=final_norm_w, w_ada=w_ada, b_ada=b_ada,
             w_ffn_gu=w_ffn_gu, w_ffn_d=w_ffn_d, w_in_ab=w_in_ab,
             s5_lam_re=s5_lam_re, s5_lam_im=s5_lam_im, s5_b_re=s5_b_re, s5_b_im=s5_b_im,
             s5_c_re=s5_c_re, s5_c_im=s5_c_im, s5_d=s5_d, s5_log_step=s5_log_step,
             s5_w_glu=s5_w_glu, s5_b_glu=s5_b_glu, hg_lb_logits=hg_lb_logits, hg_norm_w=hg_norm_w,
             w_out_ab=w_out_ab, w_in_c=w_in_c, conv_w=conv_w, conv_b=conv_b,
             w_gate_a=w_gate_a, b_gate_a=b_gate_a, w_gate_x=w_gate_x, b_gate_x=b_gate_x,
             lru_lambda=lru_lambda, w_out_c=w_out_c)
    dt = x_prompt.dtype
    z_s5 = jnp.zeros((N_AB, BATCH, G_A, N_A), dt)
    z_hg = jnp.zeros((N_AB, BATCH, H_B, K_B, V_B), dt)
    z_lru = jnp.zeros((N_C, BATCH, D_RNN), dt)
    z_conv = jnp.zeros((N_C, BATCH, CONV_W - 1, D_RNN), dt)
    y_prompt, p_s5r, p_s5i, p_hg, p_lru, p_conv = run_group(x_prompt, c_prompt, z_s5, z_s5, z_hg, z_lru, z_conv, P)
    y_sample, s_s5r, s_s5i, s_hg, s_lru, s_conv = run_group(x_sample, c_sample, state_s5_re, state_s5_im,
                                                           state_hgrn, state_lru, state_conv, P)
    return (y_prompt, y_sample, p_s5r, p_s5i, p_hg, p_lru, p_conv, s_s5r, s_s5i, s_hg, s_lru, s_conv)
```

```python
import functools

import numpy as np
import jax
import jax.numpy as jnp
from jax import lax
from jax.experimental import pallas as pl
from jax.experimental.pallas import tpu as pltpu

F32 = jnp.float32
BF16 = jnp.bfloat16
EPS = 1e-6
LRU_C = 8.0

VMEM_LIMIT_BYTES = 60 * 1024 * 1024
SUBLANES = 8
LANES = 128

S5_GROUP = 16
S5_STATE = 64
S5_COLS = 8
HG_HEADS = 8
HG_DIM = 128
HG_CHUNK_T = 64
HG_LEVELS = 6
LRU_SUPER = 640
N_MOD = 9


def _cp(*sem):
    return pltpu.CompilerParams(dimension_semantics=sem, vmem_limit_bytes=VMEM_LIMIT_BYTES)


def _dot(a, b):
    return jnp.dot(a, b, preferred_element_type=F32)


def _dot_nt(a, b):
    return lax.dot_general(a, b, (((1,), (1,)), ((), ())), preferred_element_type=F32)


def _dot_tn(a, b):
    return lax.dot_general(a, b, (((0,), (0,)), ((), ())), preferred_element_type=F32)


def _silu(x):
    return x * jax.nn.sigmoid(x)


def _gelu(x):
    return jax.nn.gelu(x, approximate=True)


def _log_sigmoid(x):
    return jnp.minimum(x, 0.0) - jnp.log1p(jnp.exp(-jnp.abs(x)))


def _fma_rows(y, mul, add):
    rm = mul.shape[0]
    r, d = y.shape
    if rm == r:
        return y * mul + add
    y3 = y.reshape(r // rm, rm, d)
    return (y3 * mul[None] + add[None]).reshape(r, d)


def _mul_rows(y, mul):
    rm = mul.shape[0]
    r, d = y.shape
    if rm == r:
        return y * mul
    return (y.reshape(r // rm, rm, d) * mul[None]).reshape(r, d)


def _norm_mod(x, w, shift, scale):
    ms = jnp.mean(x * x, axis=-1, keepdims=True)
    y = x * lax.rsqrt(ms + EPS) * w
    return _fma_rows(y, 1.0 + scale, shift)


def _split3(x):
    hi = x.astype(BF16)
    r1 = x - hi.astype(F32)
    mid = r1.astype(BF16)
    lo = (r1 - mid.astype(F32)).astype(BF16)
    return hi, mid, lo


def _ada_kernel(c_ref, w_ref, b_ref, o_ref):
    sc = _silu(c_ref[...]).astype(BF16)
    o_ref[0] = _dot(sc, w_ref[0].astype(BF16)) + b_ref[0]


def _ada_call(c_all, w_ada, b_ada, tn=1024):
    depth, d, n = w_ada.shape
    r = c_all.shape[0]
    return pl.pallas_call(
        _ada_kernel,
        out_shape=jax.ShapeDtypeStruct((depth, r, n), F32),
        grid=(depth, n // tn),
        in_specs=[pl.BlockSpec((r, d), lambda l, j: (0, 0)),
                  pl.BlockSpec((1, d, tn), lambda l, j: (l, 0, j)),
                  pl.BlockSpec((1, 1, tn), lambda l, j: (l, 0, j))],
        out_specs=pl.BlockSpec((1, r, tn), lambda l, j: (l, 0, j)),
        compiler_params=_cp("parallel", "parallel"),
        name="ada",
    )(c_all, w_ada, b_ada.reshape(depth, 1, n))


def _ffn_kernel(x_ref, nw_ref, sh_ref, sc_ref, gt_ref, wg_ref, wu_ref, wd_ref, o_ref, h_ref):
    f = pl.program_id(1)

    @pl.when(f == 0)
    def _():
        h = _norm_mod(x_ref[...], nw_ref[...], sh_ref[...], sc_ref[...])
        h_ref[...] = h.astype(BF16)
        o_ref[...] = jnp.zeros_like(o_ref)

    h = h_ref[...]
    g = _dot(h, wg_ref[0, 0].astype(BF16))
    u = _dot(h, wu_ref[0, 0].astype(BF16))
    a = (_silu(g) * u).astype(BF16)
    o_ref[...] += _dot(a, wd_ref[0, 0].astype(BF16))

    @pl.when(f == pl.num_programs(1) - 1)
    def _():
        o_ref[...] = x_ref[...] + _mul_rows(o_ref[...], 0.5 * gt_ref[...])


def _ffn_call(x, nw, shift, scale, gate, w_gu, w_d, l, s, tm, tf):
    m, d = x.shape
    ff = w_d.shape[2]
    rm = shift.shape[0]
    nf = ff // tf
    row = lambda i, f: (i, 0)
    fix = lambda i, f: (0, 0)
    return pl.pallas_call(
        _ffn_kernel,
        out_shape=jax.ShapeDtypeStruct((m, d), F32),
        grid=(m // tm, nf),
        in_specs=[pl.BlockSpec((tm, d), row),
                  pl.BlockSpec((1, d), fix),
                  pl.BlockSpec((rm, d), fix),
                  pl.BlockSpec((rm, d), fix),
                  pl.BlockSpec((rm, d), fix),
                  pl.BlockSpec((1, 1, d, tf), lambda i, f: (l, s, 0, f)),
                  pl.BlockSpec((1, 1, d, tf), lambda i, f: (l, s, 0, f + nf)),
                  pl.BlockSpec((1, 1, tf, d), lambda i, f: (l, s, f, 0))],
        out_specs=pl.BlockSpec((tm, d), row),
        scratch_shapes=[pltpu.VMEM((tm, d), BF16)],
        compiler_params=_cp("parallel", "arbitrary"),
        name="ffn",
    )(x, nw, shift, scale, gate, w_gu, w_gu, w_d)


def _inproj_kernel(x_ref, nw_ref, sh_ref, sc_ref, w_ref, o_ref, h_ref):
    @pl.when(pl.program_id(1) == 0)
    def _():
        h_ref[...] = _norm_mod(x_ref[...], nw_ref[...], sh_ref[...], sc_ref[...]).astype(BF16)

    o_ref[...] = _dot(h_ref[...], w_ref[0].astype(BF16))


def _inproj_call(x, nw, shift, scale, w, j, tm, tn):
    m, d = x.shape
    n = w.shape[2]
    rm = shift.shape[0]
    fix = lambda i, k: (0, 0)
    return pl.pallas_call(
        _inproj_kernel,
        out_shape=jax.ShapeDtypeStruct((m, n), F32),
        grid=(m // tm, n // tn),
        in_specs=[pl.BlockSpec((tm, d), lambda i, k: (i, 0)),
                  pl.BlockSpec((1, d), fix),
                  pl.BlockSpec((rm, d), fix),
                  pl.BlockSpec((rm, d), fix),
                  pl.BlockSpec((1, d, tn), lambda i, k: (j, 0, k))],
        out_specs=pl.BlockSpec((tm, tn), lambda i, k: (i, k)),
        scratch_shapes=[pltpu.VMEM((tm, d), BF16)],
        compiler_params=_cp("parallel", "arbitrary"),
        name="inproj",
    )(x, nw, shift, scale, w)


def _outproj_kernel(*refs, n_in):
    x_ref, gt_ref = refs[0], refs[1]
    a_refs = refs[2:2 + n_in]
    w_refs = refs[2 + n_in:2 + 2 * n_in]
    o_ref = refs[2 + 2 * n_in]
    acc = _dot(a_refs[0][...], w_refs[0][0].astype(BF16))
    for a_ref, w_ref in zip(a_refs[1:], w_refs[1:]):
        acc = acc + _dot(a_ref[...], w_ref[0].astype(BF16))
    o_ref[...] = x_ref[...] + _mul_rows(acc, gt_ref[...])


def _outproj_call(x, gate, acts, w, j, tm, tn):
    m, d = x.shape
    rm = gate.shape[0]
    n_in = len(acts)
    ka = acts[0].shape[1]
    in_specs = [pl.BlockSpec((tm, tn), lambda i, k: (i, k)),
                pl.BlockSpec((rm, tn), lambda i, k: (0, k))]
    in_specs += [pl.BlockSpec((tm, ka), lambda i, k: (i, 0)) for _ in acts]
    in_specs += [pl.BlockSpec((1, ka, tn), functools.partial(lambda i, k, p: (j, p, k), p=p))
                 for p in range(n_in)]
    return pl.pallas_call(
        functools.partial(_outproj_kernel, n_in=n_in),
        out_shape=jax.ShapeDtypeStruct((m, d), F32),
        grid=(m // tm, d // tn),
        in_specs=in_specs,
        out_specs=pl.BlockSpec((tm, tn), lambda i, k: (i, k)),
        compiler_params=_cp("parallel", "parallel"),
        name="outproj",
    )(x, gate, *acts, *([w] * n_in))


def _final_norm_kernel(x_ref, w_ref, o_ref):
    x = x_ref[...]
    ms = jnp.mean(x * x, axis=-1, keepdims=True)
    o_ref[...] = x * lax.rsqrt(ms + EPS) * w_ref[...]


def _final_norm_call(x, w, tm):
    m, d = x.shape
    return pl.pallas_call(
        _final_norm_kernel,
        out_shape=jax.ShapeDtypeStruct((m, d), F32),
        grid=(m // tm,),
        in_specs=[pl.BlockSpec((tm, d), lambda i: (i, 0)),
                  pl.BlockSpec((1, d), lambda i: (0, 0))],
        out_specs=pl.BlockSpec((tm, d), lambda i: (i, 0)),
        compiler_params=_cp("parallel"),
        name="final_norm",
    )(x, w)


def _s5_kernel(*refs, rb, tc, use_p):
    if use_p:
        (u_ref, pdup_ref, psel_ref, wb_ref, ar_ref, ai_ref, wcr_ref, wci_ref, d_ref, wglu_ref,
         bglu_ref, h0r_ref, h0i_ref, o_ref, hr_ref, hi_ref, xr_ref, xi_ref) = refs
    else:
        (u_ref, wb_ref, ar_ref, ai_ref, wcr_ref, wci_ref, d_ref, wglu_ref,
         bglu_ref, h0r_ref, h0i_ref, o_ref, hr_ref, hi_ref, xr_ref, xi_ref) = refs

    @pl.when(pl.program_id(0) == 0)
    def _():
        hr_ref[...] = h0r_ref[...]
        hi_ref[...] = h0i_ref[...]

    u = u_ref[...]
    if use_p:
        pdup = pdup_ref[...]
        u_hi, u_mid, u_lo = _split3(u)
        u8 = _dot(pdup, u_hi) + _dot(pdup, u_mid) + _dot(pdup, u_lo)
    else:
        u8 = u
    u8b = u8.astype(BF16)

    n_state = xr_ref.shape[1]
    cw = n_state // S5_COLS
    for c in range(S5_COLS):
        xc = _dot(u8b[:, c * LANES:(c + 1) * LANES], wb_ref[c])
        xr_ref[:, c * cw:(c + 1) * cw] = xc[:, :cw]
        xi_ref[:, c * cw:(c + 1) * cw] = xc[:, cw:]

    lane_group = 1024
    for lo_ in range(0, n_state, lane_group):
        ls = pl.ds(lo_, lane_group)
        ar = ar_ref[:, ls]
        ai = ai_ref[:, ls]
        if tc == 1:
            hr, hi_ = hr_ref[:, ls], hi_ref[:, ls]
            nr = ar * hr - ai * hi_ + xr_ref[:, ls]
            ni = ar * hi_ + ai * hr + xi_ref[:, ls]
            xr_ref[:, ls] = nr
            xi_ref[:, ls] = ni
        else:
            def step(t, carry, ls=ls, ar=ar, ai=ai):
                hr, hi_ = carry
                rows = pl.ds(pl.multiple_of(t * rb, rb), rb)
                nr = ar * hr - ai * hi_ + xr_ref[rows, ls]
                ni = ar * hi_ + ai * hr + xi_ref[rows, ls]
                xr_ref[rows, ls] = nr
                xi_ref[rows, ls] = ni
                return nr, ni

            nr, ni = lax.fori_loop(0, tc, step, (hr_ref[:, ls], hi_ref[:, ls]), unroll=4)
        hr_ref[:, ls] = nr
        hi_ref[:, ls] = ni

    ys = []
    for c in range(S5_COLS):
        cs = pl.ds(c * cw, cw)
        ys.append(_dot(xr_ref[:, cs].astype(BF16), wcr_ref[c])
                  + _dot(xi_ref[:, cs].astype(BF16), wci_ref[c]))
    y = jnp.concatenate(ys, axis=1) + d_ref[...] * u8
    y = _gelu(y)
    z = _dot(y.astype(BF16), wglu_ref[...].astype(BF16)) + bglu_ref[...]
    out = (y * jax.nn.sigmoid(z)).astype(BF16)
    if use_p:
        out = _dot(psel_ref[...], out).astype(BF16)
    o_ref[...] = out


def _s5_perm_mats(tc, nb):
    rb = 2 * nb
    pdup = np.zeros((tc * rb, tc * nb), np.float32)
    psel = np.zeros((tc * nb, tc * rb), np.float32)
    for t in range(tc):
        for b in range(nb):
            pdup[t * rb + b, t * nb + b] = 1.0
            pdup[t * rb + nb + b, t * nb + b] = 1.0
            psel[t * nb + b, t * rb + b] = 1.0
    return pdup, psel


def _s5_call(z, u_col, prm, h0r, h0i, rb, tc, use_p):
    m = z.shape[0]
    d_a = prm["d"].shape[1]
    n_state = prm["ar"].shape[1]
    nb = rb // 2 if use_p else rb
    ru = nb * tc
    rx = rb * tc
    fix2 = lambda i: (0, 0)
    fix3 = lambda i: (0, 0, 0)
    ins = [z]
    specs = [pl.BlockSpec((ru, d_a), lambda i: (i, u_col))]
    if use_p:
        pdup, psel = _s5_perm_mats(tc, nb)
        ins += [jnp.asarray(pdup, BF16), jnp.asarray(psel, BF16)]
        specs += [pl.BlockSpec((rx, ru), fix2), pl.BlockSpec((ru, rx), fix2)]
    ins += [prm["wb"], prm["ar"], prm["ai"], prm["wcr"], prm["wci"], prm["d"], prm["wglu"],
            prm["bglu"], h0r, h0i]
    specs += [pl.BlockSpec(prm["wb"].shape, fix3),
              pl.BlockSpec((1, n_state), fix2), pl.BlockSpec((1, n_state), fix2),
              pl.BlockSpec(prm["wcr"].shape, fix3), pl.BlockSpec(prm["wci"].shape, fix3),
              pl.BlockSpec((1, d_a), fix2), pl.BlockSpec((d_a, d_a), fix2),
              pl.BlockSpec((1, d_a), fix2),
              pl.BlockSpec((rb, n_state), fix2), pl.BlockSpec((rb, n_state), fix2)]
    return pl.pallas_call(
        functools.partial(_s5_kernel, rb=rb, tc=tc, use_p=use_p),
        out_shape=(jax.ShapeDtypeStruct((m, d_a), BF16),
                   jax.ShapeDtypeStruct((rb, n_state), F32),
                   jax.ShapeDtypeStruct((rb, n_state), F32)),
        grid=(m // ru,),
        in_specs=specs,
        out_specs=(pl.BlockSpec((ru, d_a), lambda i: (i, 0)),
                   pl.BlockSpec((rb, n_state), fix2), pl.BlockSpec((rb, n_state), fix2)),
        scratch_shapes=[pltpu.VMEM((rx, n_state), F32), pltpu.VMEM((rx, n_state), F32)],
        compiler_params=_cp("arbitrary"),
        name="s5",
    )(*ins)


def _s5_params(lam_re, lam_im, b_re, b_im, c_re, c_im, d_skip, log_step, w_glu, b_glu):
    g, n = lam_re.shape
    gs = b_re.shape[2]
    step = jnp.exp(log_step)[:, None]
    mag = jnp.exp(lam_re * step)
    abar_r, abar_i = mag * jnp.cos(lam_im * step), mag * jnp.sin(lam_im * step)
    den = lam_re * lam_re + lam_im * lam_im
    pr, pim = abar_r - 1.0, abar_i
    zr = (pr * lam_re + pim * lam_im) / den
    zi = (pim * lam_re - pr * lam_im) / den
    bbr = zr[..., None] * b_re - zi[..., None] * b_im
    bbi = zr[..., None] * b_im + zi[..., None] * b_re
    gl = LANES // gs
    nc = g // gl
    eye = jnp.eye(gl, dtype=F32)

    def in_mat(bb):
        return jnp.einsum("ab,cank->cakbn", eye, bb.reshape(nc, gl, n, gs)).reshape(nc, gl * gs, gl * n)

    def out_mat(cc):
        return jnp.einsum("ab,cakn->canbk", eye, cc.reshape(nc, gl, gs, n)).reshape(nc, gl * n, gl * gs)

    return dict(
        wb=jnp.concatenate([in_mat(bbr), in_mat(bbi)], axis=-1).astype(BF16),
        wcr=out_mat(c_re).astype(BF16),
        wci=out_mat(-c_im).astype(BF16),
        ar=abar_r.reshape(1, g * n), ai=abar_i.reshape(1, g * n),
        d=d_skip.reshape(1, -1), wglu=w_glu, bglu=b_glu.reshape(1, -1))


def _log_f(fz, log_lb, log1m_lb):
    b = log1m_lb + _log_sigmoid(fz)
    a = jnp.broadcast_to(log_lb, b.shape)
    return jnp.maximum(a, b) + jnp.log1p(jnp.exp(-jnp.abs(a - b)))


def _hgrn_consts(tc, nb):
    r = tc * nb
    levels = int(np.log2(tc))
    t = np.arange(r) // nb
    b = np.arange(r) % nb
    same_b = b[:, None] == b[None, :]
    tt, ts = t[:, None], t[None, :]
    expo = np.zeros((levels + 2, r, r), np.float32)
    mask = np.zeros((levels + 1, r, r), np.float32)
    for l in range(levels):
        blk, half = 2 << l, 1 << l
        split = (t // blk) * blk + half - 1
        upper = (t % blk) >= half
        sp = split[:, None]
        up = upper[:, None]
        expo[l] = same_b & np.where(up, (ts > sp) & (ts <= tt), (ts > tt) & (ts <= sp))
        mask[l] = same_b & ((tt // blk) == (ts // blk)) & up & ~upper[None, :]
    expo[levels] = same_b & (ts <= tt)
    expo[levels + 1] = same_b & (ts > tt)
    mask[levels] = np.eye(r, dtype=np.float32)
    bmask = np.zeros((nb, r, LANES), np.float32)
    for k in range(nb):
        bmask[k, b == k, :] = 1.0
    return expo, mask, bmask


def _hgrn_kernel(q_ref, fz_ref, v_ref, g_ref, llb_ref, l1m_ref, om_ref, gw_ref, expo_ref, mask_ref,
                 bm_ref, s0_ref, o_ref, st_ref, sc_ref, gs_ref, *, nb):
    levels = mask_ref.shape[0] - 1
    hd = HG_DIM

    @pl.when(pl.program_id(0) == 0)
    def _():
        st_ref[...] = s0_ref[...]

    fz = fz_ref[...]
    lf = _log_f(fz, llb_ref[...], l1m_ref[...])
    kk = om_ref[...] * jax.nn.sigmoid(-fz)
    q = q_ref[...]
    lf_hi = lf.astype(BF16)
    lf_lo = (lf - lf_hi.astype(F32)).astype(BF16)

    def expo(l):
        e = expo_ref[l]
        return _dot(e, lf_hi) + _dot(e, lf_lo)

    qb, kb = q.astype(BF16), kk.astype(BF16)
    for h in range(HG_HEADS):
        hs = slice(h * hd, (h + 1) * hd)
        sc_ref[h] = mask_ref[levels] * _dot_nt(qb[:, hs], kb[:, hs])
    for l in range(levels):
        e = jnp.exp(expo(l))
        qt = (q * e).astype(BF16)
        kt = (kk * e).astype(BF16)
        for h in range(HG_HEADS):
            hs = slice(h * hd, (h + 1) * hd)
            sc_ref[h] += mask_ref[l] * _dot_nt(qt[:, hs], kt[:, hs])

    gcum = expo(levels)
    gs_ref[...] = gcum
    qg = (q * jnp.exp(gcum)).astype(BF16)
    kend = (kk * jnp.exp(expo(levels + 1))).astype(BF16)
    v = v_ref[...]
    vb = v.astype(BF16)
    r = q.shape[0]
    gw = gw_ref[...]
    for h in range(HG_HEADS):
        hs = slice(h * hd, (h + 1) * hd)
        st = st_ref[h]
        o = _dot(sc_ref[h].astype(BF16), vb[:, hs])
        oi = _dot_nt(qg[:, hs], st.astype(BF16))
        for k in range(nb):
            o = o + bm_ref[k] * oi[:, k * hd:(k + 1) * hd]
        vcat = jnp.concatenate([(v[:, hs] * bm_ref[k]).astype(BF16) for k in range(nb)], axis=1)
        upd = _dot_tn(vcat, kend[:, hs])
        dec = jnp.concatenate(
            [jnp.broadcast_to(jnp.exp(gs_ref[r - nb + k:r - nb + k + 1, hs]), (hd, hd)) for k in range(nb)],
            axis=0)
        st_ref[h] = dec * st + upd
        o = o * lax.rsqrt(jnp.mean(o * o, axis=-1, keepdims=True) + EPS) * gw
        o_ref[:, hs] = (o * _silu(g_ref[:, hs])).astype(BF16)


def _hgrn_call(z, lbp, gnorm_w, s0t, nb):
    m = z.shape[0]
    d_b = HG_HEADS * HG_DIM
    r = HG_CHUNK_T * nb
    expo, mask, bmask = _hgrn_consts(HG_CHUNK_T, nb)
    fix2 = lambda i: (0, 0)
    fix3 = lambda i: (0, 0, 0)
    col = lambda c: pl.BlockSpec((r, d_b), functools.partial(lambda i, c: (i, c), c=c))
    vec = pl.BlockSpec((1, d_b), fix2)
    return pl.pallas_call(
        functools.partial(_hgrn_kernel, nb=nb),
        out_shape=(jax.ShapeDtypeStruct((m, d_b), BF16),
                   jax.ShapeDtypeStruct(s0t.shape, F32)),
        grid=(m // r,),
        in_specs=[col(1), col(2), col(3), col(4), vec, vec, vec,
                  pl.BlockSpec((1, HG_DIM), fix2),
                  pl.BlockSpec(expo.shape, fix3), pl.BlockSpec(mask.shape, fix3),
                  pl.BlockSpec(bmask.shape, fix3), pl.BlockSpec(s0t.shape, fix3)],
        out_specs=(pl.BlockSpec((r, d_b), lambda i: (i, 0)), pl.BlockSpec(s0t.shape, fix3)),
        scratch_shapes=[pltpu.VMEM((HG_HEADS, r, r), F32), pltpu.VMEM((r, d_b), F32)],
        compiler_params=_cp("arbitrary"),
        name="hgrn",
    )(z, z, z, z, lbp["log_lb"], lbp["log1m_lb"], lbp["one_m_lb"], gnorm_w.reshape(1, -1),
      jnp.asarray(expo, BF16), jnp.asarray(mask, F32), jnp.asarray(bmask, F32), s0t)


def _hgrn_dec_kernel(q_ref, fz_ref, v_ref, g_ref, llb_ref, l1m_ref, om_ref, gw_ref, s_ref,
                     o_ref, so_ref):
    hd = HG_DIM
    fz = fz_ref[...]
    f = jnp.exp(_log_f(fz, llb_ref[...], l1m_ref[...]))
    kk = om_ref[...] * jax.nn.sigmoid(-fz)
    q = q_ref[...]
    v = v_ref[...]
    gw = gw_ref[...]
    eye = (lax.broadcasted_iota(jnp.int32, (hd, hd), 0)
           == lax.broadcasted_iota(jnp.int32, (hd, hd), 1)).astype(F32)

    def col(x_row):
        return jnp.sum(eye * x_row, axis=1, keepdims=True)

    rows = []
    for j in range(q.shape[0]):
        outs = []
        for h in range(HG_HEADS):
            hs = slice(h * hd, (h + 1) * hd)
            s_new = col(f[j:j + 1, hs]) * s_ref[j, h] + col(kk[j:j + 1, hs]) * v[j:j + 1, hs]
            so_ref[j, h] = s_new
            o = jnp.sum(col(q[j:j + 1, hs]) * s_new, axis=0, keepdims=True)
            outs.append(o * lax.rsqrt(jnp.mean(o * o, axis=-1, keepdims=True) + EPS) * gw)
        rows.append(jnp.concatenate(outs, axis=1))
    o_all = jnp.concatenate(rows, axis=0)
    o_ref[...] = (o_all * _silu(g_ref[...])).astype(BF16)


def _hgrn_dec_call(z, lbp, gnorm_w, s0, sb=SUBLANES):
    m = z.shape[0]
    d_b = HG_HEADS * HG_DIM
    fix2 = lambda i: (0, 0)
    col = lambda c: pl.BlockSpec((sb, d_b), functools.partial(lambda i, c: (i, c), c=c))
    vec = pl.BlockSpec((1, d_b), fix2)
    sblk = pl.BlockSpec((sb, HG_HEADS, HG_DIM, HG_DIM), lambda i: (i, 0, 0, 0))
    return pl.pallas_call(
        _hgrn_dec_kernel,
        out_shape=(jax.ShapeDtypeStruct((m, d_b), BF16), jax.ShapeDtypeStruct(s0.shape, F32)),
        grid=(m // sb,),
        in_specs=[col(1), col(2), col(3), col(4), vec, vec, vec,
                  pl.BlockSpec((1, HG_DIM), fix2), sblk],
        out_specs=(pl.BlockSpec((sb, d_b), lambda i: (i, 0)), sblk),
        compiler_params=_cp("parallel"),
        name="hgrn_dec",
    )(z, z, z, z, lbp["log_lb"], lbp["log1m_lb"], lbp["one_m_lb"], gnorm_w.reshape(1, -1), s0)


def _lru_coeffs(xc, wg_ref, bga_ref, bgx_ref, lam_ref, a_out, b_out):
    sw = LRU_SUPER
    for j in range(xc.shape[1] // sw):
        cs = slice(j * sw, (j + 1) * sw)
        xj = xc[:, cs]
        gj = _dot(xj.astype(BF16), wg_ref[j])
        rg = jax.nn.sigmoid(gj[:, :sw] + bga_ref[:, cs])
        ig = jax.nn.sigmoid(gj[:, sw:] + bgx_ref[:, cs])
        log_a = LRU_C * rg * _log_sigmoid(lam_ref[:, cs])
        a = jnp.exp(log_a)
        bt = jnp.sqrt(-jnp.tanh(log_a) * (a * a + 1.0)) * ig * xj
        a_out(cs, a)
        b_out(cs, bt)


def _lru_kernel(gb_ref, xr_ref, cw_ref, cb_ref, wg_ref, bga_ref, bgx_ref, lam_ref, c0_ref, h0_ref,
                o_ref, hl_ref, xs_ref, a_ref, b_ref, *, nb):
    r = xr_ref.shape[0]
    hdr = xs_ref.shape[0] - r
    kw = cw_ref.shape[0]

    @pl.when(pl.program_id(0) == 0)
    def _():
        xs_ref[0:hdr, :] = c0_ref[...]
        hl_ref[...] = h0_ref[...]

    xs_ref[hdr:hdr + r, :] = xr_ref[...]
    xc = cb_ref[...] + xs_ref[hdr - (kw - 1) * nb:hdr - (kw - 1) * nb + r, :] * cw_ref[0:1, :]
    for j in range(1, kw):
        xc = xc + xs_ref[hdr - (kw - 1 - j) * nb:hdr - (kw - 1 - j) * nb + r, :] * cw_ref[j:j + 1, :]
    xs_ref[0:hdr, :] = xs_ref[r:r + hdr, :]

    def a_out(cs, val):
        a_ref[:, cs] = val

    def b_out(cs, val):
        b_ref[:, cs] = val

    _lru_coeffs(xc, wg_ref, bga_ref, bgx_ref, lam_ref, a_out, b_out)

    lower = lax.broadcasted_iota(jnp.int32, (SUBLANES, xc.shape[1]), 0) < nb

    def pair(k, h):
        rows = pl.ds(pl.multiple_of(k * SUBLANES, SUBLANES), SUBLANES)
        a8 = a_ref[rows, :]
        b8 = b_ref[rows, :]
        h_a = a8 * h + b8
        h_b = a8 * pltpu.roll(h_a, nb, 0) + b8
        b_ref[rows, :] = jnp.where(lower, h_a, h_b)
        return pltpu.roll(h_b, nb, 0)

    hl_ref[...] = lax.fori_loop(0, r // SUBLANES, pair, hl_ref[...], unroll=2)
    o_ref[...] = (_gelu(gb_ref[...]) * b_ref[...]).astype(BF16)


def _lru_call(z, prm, conv0, h0, nb, tc):
    m = z.shape[0]
    d_rnn = prm["lam"].shape[1]
    r = nb * tc
    hdr = conv0.shape[0]
    fix2 = lambda i: (0, 0)
    fix3 = lambda i: (0, 0, 0)
    vec = pl.BlockSpec((1, d_rnn), fix2)
    return pl.pallas_call(
        functools.partial(_lru_kernel, nb=nb),
        out_shape=(jax.ShapeDtypeStruct((m, d_rnn), BF16),
                   jax.ShapeDtypeStruct((SUBLANES, d_rnn), F32)),
        grid=(m // r,),
        in_specs=[pl.BlockSpec((r, d_rnn), lambda i: (i, 0)),
                  pl.BlockSpec((r, d_rnn), lambda i: (i, 1)),
                  pl.BlockSpec(prm["cw"].shape, fix2), vec,
                  pl.BlockSpec(prm["wg"].shape, fix3), vec, vec, vec,
                  pl.BlockSpec((hdr, d_rnn), fix2), pl.BlockSpec((SUBLANES, d_rnn), fix2)],
        out_specs=(pl.BlockSpec((r, d_rnn), lambda i: (i, 0)),
                   pl.BlockSpec((SUBLANES, d_rnn), fix2)),
        scratch_shapes=[pltpu.VMEM((hdr + r, d_rnn), F32), pltpu.VMEM((r, d_rnn), F32),
                        pltpu.VMEM((r, d_rnn), F32)],
        compiler_params=_cp("arbitrary"),
        name="lru",
    )(z, z, prm["cw"], prm["cb"], prm["wg"], prm["bga"], prm["bgx"], prm["lam"], conv0, h0)


def _lru_dec_kernel(gb_ref, xr_ref, buf_ref, cw_ref, cb_ref, wg_ref, bga_ref, bgx_ref, lam_ref,
                    h0_ref, o_ref, h_ref, a_ref, b_ref):
    kw = cw_ref.shape[0]
    xc = cb_ref[...] + xr_ref[...] * cw_ref[kw - 1:kw, :]
    for j in range(kw - 1):
        xc = xc + buf_ref[j] * cw_ref[j:j + 1, :]

    def a_out(cs, val):
        a_ref[:, cs] = val

    def b_out(cs, val):
        b_ref[:, cs] = val

    _lru_coeffs(xc, wg_ref, bga_ref, bgx_ref, lam_ref, a_out, b_out)
    h = a_ref[...] * h0_ref[...] + b_ref[...]
    h_ref[...] = h
    o_ref[...] = (_gelu(gb_ref[...]) * h).astype(BF16)


def _lru_dec_call(z, prm, buf, h0):
    m = z.shape[0]
    d_rnn = prm["lam"].shape[1]
    fix2 = lambda i: (0, 0)
    fix3 = lambda i: (0, 0, 0)
    vec = pl.BlockSpec((1, d_rnn), fix2)
    full = pl.BlockSpec((m, d_rnn), fix2)
    return pl.pallas_call(
        _lru_dec_kernel,
        out_shape=(jax.ShapeDtypeStruct((m, d_rnn), BF16), jax.ShapeDtypeStruct((m, d_rnn), F32)),
        grid=(1,),
        in_specs=[pl.BlockSpec((m, d_rnn), lambda i: (0, 0)),
                  pl.BlockSpec((m, d_rnn), lambda i: (0, 1)),
                  pl.BlockSpec(buf.shape, fix3),
                  pl.BlockSpec(prm["cw"].shape, fix2), vec,
                  pl.BlockSpec(prm["wg"].shape, fix3), vec, vec, vec, full],
        out_specs=(full, full),
        scratch_shapes=[pltpu.VMEM((m, d_rnn), F32), pltpu.VMEM((m, d_rnn), F32)],
        compiler_params=_cp("arbitrary"),
        name="lru_dec",
    )(z, z, buf, prm["cw"], prm["cb"], prm["wg"], prm["bga"], prm["bgx"], prm["lam"], h0)


def _lru_params(conv_w, conv_b, w_ga, b_ga, w_gx, b_gx, lam):
    nblk, bs = w_ga.shape[0], w_ga.shape[1]
    per = LRU_SUPER // bs
    ns = nblk // per
    eye = jnp.eye(per, dtype=F32)

    def sup(w):
        return jnp.einsum("ab,sajk->sajbk", eye, w.reshape(ns, per, bs, bs)).reshape(ns, per * bs, per * bs)

    return dict(cw=conv_w, cb=conv_b.reshape(1, -1),
                wg=jnp.concatenate([sup(w_ga), sup(w_gx)], axis=-1).astype(BF16),
                bga=b_ga.reshape(1, -1), bgx=b_gx.reshape(1, -1), lam=lam.reshape(1, -1))


def _run_group(x, mods, P, s5p, lbp, lrup, st, prefill, nb):
    m, d = x.shape
    depth = P["w_ada"].shape[0]
    if prefill:
        tm, tf, tn_in, tn_out = 1024, 256, 1024, 1024
    else:
        tm, tf, tn_in, tn_out = m, 768, 1024, 1024
    outs = dict(s5r=[], s5i=[], hg=[], lru=[], conv=[])
    for l in range(depth):
        j = l // 2
        mv = [mods[l, :, k * d:(k + 1) * d] for k in range(N_MOD)]
        nw = P["norm_w"][l]
        x = _ffn_call(x, nw[0:1], mv[0], mv[1], mv[2], P["w_ffn_gu"], P["w_ffn_d"], l, 0, tm, tf)
        if l % 2 == 0:
            z = _inproj_call(x, nw[1:2], mv[3], mv[4], P["w_in_ab"], j, tm, tn_in)
            if prefill:
                ya, hr, hi = _s5_call(z, 0, s5p[j], st["s5r"][j], st["s5i"][j],
                                      rb=2 * nb, tc=64, use_p=True)
                yb, hg = _hgrn_call(z, lbp[j], P["hg_norm_w"][j], st["hg"][j], nb)
            else:
                ya, hr, hi = _s5_call(z, 0, s5p[j], st["s5r"][j], st["s5i"][j],
                                      rb=m, tc=1, use_p=False)
                yb, hg = _hgrn_dec_call(z, lbp[j], P["hg_norm_w"][j], st["hg"][j])
            outs["s5r"].append(hr)
            outs["s5i"].append(hi)
            outs["hg"].append(hg)
            x = _outproj_call(x, mv[5], [ya, yb], P["w_out_ab"], j, tm, tn_out)
        else:
            z = _inproj_call(x, nw[1:2], mv[3], mv[4], P["w_in_c"], j, tm, tn_in)
            d_rnn = z.shape[1] // 2
            if prefill:
                y, hl = _lru_call(z, lrup[j], st["conv"][j], st["lru"][j], nb, tc=64)
            else:
                y, hl = _lru_dec_call(z, lrup[j], st["conv"][j], st["lru"][j])
            outs["lru"].append(hl)
            outs["conv"].append(z[:, d_rnn:])
            x = _outproj_call(x, mv[5], [y], P["w_out_c"], j, tm, tn_out)
        x = _ffn_call(x, nw[2:3], mv[6], mv[7], mv[8], P["w_ffn_gu"], P["w_ffn_d"], l, 1, tm, tf)
    y = _final_norm_call(x, P["final_norm_w"].reshape(1, -1), tm)
    return y, outs


def kernel(x_prompt, x_sample, state_s5_re, state_s5_im, state_hgrn, state_lru, state_conv,
           c_prompt, c_sample, norm_w, final_norm_w, w_ada, b_ada, w_ffn_gu, w_ffn_d,
           w_in_ab, s5_lam_re, s5_lam_im, s5_b_re, s5_b_im, s5_c_re, s5_c_im, s5_d, s5_log_step,
           s5_w_glu, s5_b_glu, hg_lb_logits, hg_norm_w, w_out_ab, w_in_c, conv_w, conv_b,
           w_gate_a, b_gate_a, w_gate_x, b_gate_x, lru_lambda, w_out_c):
    P = dict(norm_w=norm_w, final_norm_w=final_norm_w, w_ada=w_ada, w_ffn_gu=w_ffn_gu,
             w_ffn_d=w_ffn_d, w_in_ab=w_in_ab, hg_norm_w=hg_norm_w, w_out_ab=w_out_ab,
             w_in_c=w_in_c, w_out_c=w_out_c)
    bsz, seq, d = x_prompt.shape
    nsm = x_sample.shape[0]
    n_ab, n_c = w_in_ab.shape[0], w_in_c.shape[0]
    g_a, n_a = s5_lam_re.shape[1], s5_lam_re.shape[2]
    n_state = g_a * n_a
    d_rnn = lru_lambda.shape[1]
    kw = conv_w.shape[1]

    s5p = [_s5_params(s5_lam_re[j], s5_lam_im[j], s5_b_re[j], s5_b_im[j], s5_c_re[j], s5_c_im[j],
                      s5_d[j], s5_log_step[j], s5_w_glu[j], s5_b_glu[j]) for j in range(n_ab)]
    lb_all = jnp.cumsum(jax.nn.softmax(hg_lb_logits.astype(F32), axis=0), axis=0)
    lb_all = lb_all - lb_all[0:1]
    lbp = [dict(log_lb=jnp.log(lb_all[j]).reshape(1, -1),
                log1m_lb=jnp.log1p(-lb_all[j]).reshape(1, -1),
                one_m_lb=(1.0 - lb_all[j]).reshape(1, -1)) for j in range(n_ab)]
    lrup = [_lru_params(conv_w[j], conv_b[j], w_gate_a[j], b_gate_a[j], w_gate_x[j], b_gate_x[j],
                        lru_lambda[j]) for j in range(n_c)]

    pad = (-bsz) % SUBLANES
    c_all = jnp.concatenate([c_prompt, jnp.zeros((pad, d), F32), c_sample], axis=0)
    mods = _ada_call(c_all, w_ada, b_ada)
    mods_p = jnp.concatenate([mods[:, :bsz]] * (SUBLANES // bsz), axis=1)
    mods_s = mods[:, bsz + pad:]

    xp = jnp.transpose(x_prompt, (1, 0, 2)).reshape(seq * bsz, d)
    rb = 2 * bsz
    st_p = dict(s5r=[jnp.zeros((rb, n_state), F32)] * n_ab, s5i=[jnp.zeros((rb, n_state), F32)] * n_ab,
                hg=[jnp.zeros((HG_HEADS, bsz * HG_DIM, HG_DIM), F32)] * n_ab,
                lru=[jnp.zeros((SUBLANES, d_rnn), F32)] * n_c,
                conv=[jnp.zeros((4 * bsz, d_rnn), F32)] * n_c)
    yp, op = _run_group(xp, mods_p, P, s5p, lbp, lrup, st_p, True, bsz)
    y_prompt = jnp.transpose(yp.reshape(seq, bsz, d), (1, 0, 2))
    p_s5r = jnp.stack([h[:bsz].reshape(bsz, g_a, n_a) for h in op["s5r"]])
    p_s5i = jnp.stack([h[:bsz].reshape(bsz, g_a, n_a) for h in op["s5i"]])
    p_hg = jnp.stack([jnp.transpose(s.reshape(HG_HEADS, bsz, HG_DIM, HG_DIM), (1, 0, 3, 2))
                      for s in op["hg"]])
    p_lru = jnp.stack([h[:bsz] for h in op["lru"]])
    p_conv = jnp.stack([jnp.transpose(xr[-(kw - 1) * bsz:].reshape(kw - 1, bsz, d_rnn), (1, 0, 2))
                        for xr in op["conv"]])

    xs = x_sample.reshape(nsm, d)
    st_s = dict(s5r=[state_s5_re[j].reshape(nsm, n_state) for j in range(n_ab)],
                s5i=[state_s5_im[j].reshape(nsm, n_state) for j in range(n_ab)],
                hg=[state_hgrn[j] for j in range(n_ab)],
                lru=[state_lru[j] for j in range(n_c)],
                conv=[jnp.transpose(state_conv[j], (1, 0, 2)) for j in range(n_c)])
    ys, os_ = _run_group(xs, mods_s, P, s5p, lbp, lrup, st_s, False, nsm)
    y_sample = ys.reshape(nsm, 1, d)
    s_s5r = jnp.stack([h.reshape(nsm, g_a, n_a) for h in os_["s5r"]])
    s_s5i = jnp.stack([h.reshape(nsm, g_a, n_a) for h in os_["s5i"]])
    s_hg = jnp.stack(os_["hg"])
    s_lru = jnp.stack(os_["lru"])
    s_conv = jnp.stack([jnp.concatenate([state_conv[j][:, 1:], os_["conv"][j][:, None, :]], axis=1)
                        for j in range(n_c)])
    return (y_prompt, y_sample, p_s5r, p_s5i, p_hg, p_lru, p_conv,
            s_s5r, s_s5i, s_hg, s_lru, s_conv)
```

```python
import functools

import numpy as np
import jax
import jax.numpy as jnp
from jax import lax
from jax.experimental import pallas as pl
from jax.experimental.pallas import tpu as pltpu

F32 = jnp.float32
BF16 = jnp.bfloat16
EPS = 1e-6
LRU_C = 8.0

VMEM_LIMIT_BYTES = 60 * 1024 * 1024
SUBLANES = 8
LANES = 128
BF16_ROWS = 16

S5_COLS = 8
S5_CHUNK_T = 128
HG_HEADS = 8
HG_DIM = 128
HG_CHUNK_T = 64
LRU_CHUNK_T = 64
LRU_SUPER = 640
N_MOD = 9
ROW_TILE = 1040
OUT_ROW_TILE = 640


def _cp(*sem):
    return pltpu.CompilerParams(dimension_semantics=sem, vmem_limit_bytes=VMEM_LIMIT_BYTES)


def _dot(a, b):
    return jnp.dot(a, b, preferred_element_type=F32)


def _dot_nt(a, b):
    return lax.dot_general(a, b, (((1,), (1,)), ((), ())), preferred_element_type=F32)


def _dot_tn(a, b):
    return lax.dot_general(a, b, (((0,), (0,)), ((), ())), preferred_element_type=F32)


def _silu(x):
    return x * jax.nn.sigmoid(x)


def _gelu(x):
    return jax.nn.gelu(x, approximate=True)


def _log_sigmoid(x):
    return jnp.minimum(x, 0.0) - jnp.log1p(jnp.exp(-jnp.abs(x)))


def _fma_rows(y, mul, add):
    rm = mul.shape[0]
    r, d = y.shape
    if rm == r:
        return y * mul + add
    y3 = y.reshape(r // rm, rm, d)
    return (y3 * mul[None] + add[None]).reshape(r, d)


def _mul_rows(y, mul):
    rm = mul.shape[0]
    r, d = y.shape
    if rm == r:
        return y * mul
    return (y.reshape(r // rm, rm, d) * mul[None]).reshape(r, d)


def _norm_mod_rows(x_ref, nw_ref, sh_ref, sc_ref, h_ref, row0, nrows):
    slab = BF16_ROWS
    per_row = sh_ref.shape[0] > SUBLANES
    nw = nw_ref[...]

    def body(s, carry):
        r0 = pl.multiple_of(row0 + s * slab, slab)
        x = x_ref[pl.ds(r0, slab), :]
        ms = jnp.mean(x * x, axis=-1, keepdims=True)
        y = x * lax.rsqrt(ms + EPS) * nw
        if per_row:
            m0 = pl.multiple_of(s * slab, slab)
            h = y * (1.0 + sc_ref[pl.ds(m0, slab), :]) + sh_ref[pl.ds(m0, slab), :]
        else:
            h = _fma_rows(y, 1.0 + sc_ref[...], sh_ref[...])
        h_ref[pl.ds(r0, slab), :] = h.astype(BF16)
        return carry

    lax.fori_loop(0, nrows // slab, body, 0, unroll=2)


def _residual_rows(o_ref, x_ref, acc, gt_ref, gts_ref, scale):
    tm = o_ref.shape[0]
    ns = gts_ref.shape[0]
    np_ = tm - ns
    is_last = pl.program_id(0) == pl.num_programs(0) - 1
    gp = scale * gt_ref[...]
    o_ref[0:np_, :] = x_ref[0:np_, :] + _mul_rows(acc(0, np_), gp)

    @pl.when(jnp.logical_not(is_last))
    def _():
        o_ref[np_:tm, :] = x_ref[np_:tm, :] + _mul_rows(acc(np_, tm), gp)

    @pl.when(is_last)
    def _():
        o_ref[np_:tm, :] = x_ref[np_:tm, :] + acc(np_, tm) * (scale * gts_ref[...])


def _ada_kernel(c_ref, w_ref, b_ref, o_ref):
    sc = _silu(c_ref[...]).astype(BF16)
    o_ref[0] = _dot(sc, w_ref[0].astype(BF16)) + b_ref[0]


def _ada_call(c_all, w_ada, b_ada, tn=1024):
    depth, d, n = w_ada.shape
    r = c_all.shape[0]
    return pl.pallas_call(
        _ada_kernel,
        out_shape=jax.ShapeDtypeStruct((depth, r, n), F32),
        grid=(depth, n // tn),
        in_specs=[pl.BlockSpec((r, d), lambda l, j: (0, 0)),
                  pl.BlockSpec((1, d, tn), lambda l, j: (l, 0, j)),
                  pl.BlockSpec((1, 1, tn), lambda l, j: (l, 0, j))],
        out_specs=pl.BlockSpec((1, r, tn), lambda l, j: (l, 0, j)),
        compiler_params=_cp("parallel", "parallel"),
        name="ada",
    )(c_all, w_ada, b_ada.reshape(depth, 1, n))


def _ffn_kernel(x_ref, nw_ref, sh_ref, sc_ref, gt_ref, shs_ref, scs_ref, gts_ref,
                wg_ref, wu_ref, wd_ref, o_ref, h_ref):
    i, f = pl.program_id(0), pl.program_id(1)
    tm = x_ref.shape[0]
    ns = shs_ref.shape[0]
    is_last = i == pl.num_programs(0) - 1

    @pl.when(f == 0)
    def _():
        _norm_mod_rows(x_ref, nw_ref, sh_ref, sc_ref, h_ref, 0, tm)
        o_ref[...] = jnp.zeros_like(o_ref)

    @pl.when(jnp.logical_and(f == 0, is_last))
    def _():
        _norm_mod_rows(x_ref, nw_ref, shs_ref, scs_ref, h_ref, tm - ns, ns)

    h = h_ref[...]
    g = _dot(h, wg_ref[0, 0].astype(BF16))
    u = _dot(h, wu_ref[0, 0].astype(BF16))
    a = (_silu(g) * u).astype(BF16)
    o_ref[...] += _dot(a, wd_ref[0, 0].astype(BF16))

    @pl.when(f == pl.num_programs(1) - 1)
    def _():
        _residual_rows(o_ref, x_ref, lambda a0, a1: o_ref[a0:a1, :], gt_ref, gts_ref, 0.5)


def _ffn_call(x, nw, mp, ms, w_gu, w_d, l, s, tm, tf):
    m, d = x.shape
    ff = w_d.shape[2]
    rm, ns = mp[0].shape[0], ms[0].shape[0]
    nf = ff // tf
    row = lambda i, f: (i, 0)
    fix = lambda i, f: (0, 0)
    pat = pl.BlockSpec((rm, d), fix)
    smp = pl.BlockSpec((ns, d), fix)
    return pl.pallas_call(
        _ffn_kernel,
        out_shape=jax.ShapeDtypeStruct((m, d), F32),
        grid=(m // tm, nf),
        in_specs=[pl.BlockSpec((tm, d), row), pl.BlockSpec((1, d), fix),
                  pat, pat, pat, smp, smp, smp,
                  pl.BlockSpec((1, 1, d, tf), lambda i, f: (l, s, 0, f)),
                  pl.BlockSpec((1, 1, d, tf), lambda i, f: (l, s, 0, f + nf)),
                  pl.BlockSpec((1, 1, tf, d), lambda i, f: (l, s, f, 0))],
        out_specs=pl.BlockSpec((tm, d), row),
        scratch_shapes=[pltpu.VMEM((tm, d), BF16)],
        compiler_params=_cp("parallel", "arbitrary"),
        name="ffn",
    )(x, nw, *mp, *ms, w_gu, w_gu, w_d)


def _inproj_kernel(x_ref, nw_ref, sh_ref, sc_ref, shs_ref, scs_ref, w_ref, o_ref, h_ref):
    i, k = pl.program_id(0), pl.program_id(1)
    tm = x_ref.shape[0]
    ns = shs_ref.shape[0]

    @pl.when(k == 0)
    def _():
        _norm_mod_rows(x_ref, nw_ref, sh_ref, sc_ref, h_ref, 0, tm)

    @pl.when(jnp.logical_and(k == 0, i == pl.num_programs(0) - 1))
    def _():
        _norm_mod_rows(x_ref, nw_ref, shs_ref, scs_ref, h_ref, tm - ns, ns)

    o_ref[...] = _dot(h_ref[...], w_ref[0].astype(BF16))


def _inproj_call(x, nw, mp, ms, w, j, tm, tn):
    m, d = x.shape
    n = w.shape[2]
    rm, ns = mp[0].shape[0], ms[0].shape[0]
    fix = lambda i, k: (0, 0)
    pat = pl.BlockSpec((rm, d), fix)
    smp = pl.BlockSpec((ns, d), fix)
    return pl.pallas_call(
        _inproj_kernel,
        out_shape=jax.ShapeDtypeStruct((m, n), F32),
        grid=(m // tm, n // tn),
        in_specs=[pl.BlockSpec((tm, d), lambda i, k: (i, 0)), pl.BlockSpec((1, d), fix),
                  pat, pat, smp, smp,
                  pl.BlockSpec((1, d, tn), lambda i, k: (j, 0, k))],
        out_specs=pl.BlockSpec((tm, tn), lambda i, k: (i, k)),
        scratch_shapes=[pltpu.VMEM((tm, d), BF16)],
        compiler_params=_cp("parallel", "arbitrary"),
        name="inproj",
    )(x, nw, mp[0], mp[1], ms[0], ms[1], w)


def _outproj_kernel(*refs, n_in):
    x_ref, gt_ref, gts_ref = refs[0], refs[1], refs[2]
    a_refs = refs[3:3 + n_in]
    w_refs = refs[3 + n_in:3 + 2 * n_in]
    o_ref = refs[3 + 2 * n_in]
    acc = _dot(a_refs[0][...], w_refs[0][...])
    for a_ref, w_ref in zip(a_refs[1:], w_refs[1:]):
        acc = acc + _dot(a_ref[...], w_ref[...])
    _residual_rows(o_ref, x_ref, lambda a0, a1: acc[a0:a1, :], gt_ref, gts_ref, 1.0)


def _outproj_call(x, gate_p, gate_s, acts, w_bf, tm):
    m, d = x.shape
    rm, ns = gate_p.shape[0], gate_s.shape[0]
    n_in = len(acts)
    ka = acts[0].shape[1]
    row = lambda i: (i, 0)
    fix = lambda i: (0, 0)
    in_specs = [pl.BlockSpec((tm, d), row), pl.BlockSpec((rm, d), fix), pl.BlockSpec((ns, d), fix)]
    in_specs += [pl.BlockSpec((tm, ka), row) for _ in acts]
    in_specs += [pl.BlockSpec((ka, d), functools.partial(lambda i, p: (p, 0), p=p)) for p in range(n_in)]
    return pl.pallas_call(
        functools.partial(_outproj_kernel, n_in=n_in),
        out_shape=jax.ShapeDtypeStruct((m, d), F32),
        grid=(m // tm,),
        in_specs=in_specs,
        out_specs=pl.BlockSpec((tm, d), row),
        compiler_params=_cp("parallel"),
        name="outproj",
    )(x, gate_p, gate_s, *acts, *([w_bf] * n_in))


def _final_norm_kernel(x_ref, w_ref, o_ref):
    x = x_ref[...]
    ms = jnp.mean(x * x, axis=-1, keepdims=True)
    o_ref[...] = x * lax.rsqrt(ms + EPS) * w_ref[...]


def _final_norm_call(x, w, tm):
    m, d = x.shape
    return pl.pallas_call(
        _final_norm_kernel,
        out_shape=jax.ShapeDtypeStruct((m, d), F32),
        grid=(m // tm,),
        in_specs=[pl.BlockSpec((tm, d), lambda i: (i, 0)),
                  pl.BlockSpec((1, d), lambda i: (0, 0))],
        out_specs=pl.BlockSpec((tm, d), lambda i: (i, 0)),
        compiler_params=_cp("parallel"),
        name="final_norm",
    )(x, w)


def _s5_kernel(*refs, nb, tc, aliased):
    (u_ref, wb_ref, ar_ref, ai_ref, wcr_ref, wci_ref, d_ref, wglu_ref, bglu_ref,
     h0r_ref, h0i_ref) = refs[:11]
    o_ref, hr_ref, hi_ref, xr_ref, xi_ref = refs[12:] if aliased else refs[11:]

    @pl.when(pl.program_id(0) == 0)
    def _():
        hr_ref[...] = h0r_ref[...]
        hi_ref[...] = h0i_ref[...]

    u = u_ref[...]
    ub = u.astype(BF16)
    n_state = xr_ref.shape[1]
    cw = n_state // S5_COLS
    for c in range(S5_COLS):
        xc = _dot(ub[:, c * LANES:(c + 1) * LANES], wb_ref[c])
        xr_ref[:, c * cw:(c + 1) * cw] = xc[:, :cw]
        xi_ref[:, c * cw:(c + 1) * cw] = xc[:, cw:]

    lane_group = 1024
    for lo_ in range(0, n_state, lane_group):
        ls = pl.ds(lo_, lane_group)
        ar = ar_ref[:, ls]
        ai = ai_ref[:, ls]
        if tc == 1:
            hr, hi = hr_ref[:, ls], hi_ref[:, ls]
            nr = ar * hr - ai * hi + xr_ref[:, ls]
            ni = ar * hi + ai * hr + xi_ref[:, ls]
            xr_ref[:, ls] = nr
            xi_ref[:, ls] = ni
        else:
            lower = lax.broadcasted_iota(jnp.int32, (SUBLANES, lane_group), 0) < nb

            def pair(k, carry, ls=ls, ar=ar, ai=ai, lower=lower):
                hr, hi = carry
                rows = pl.ds(pl.multiple_of(k * SUBLANES, SUBLANES), SUBLANES)
                x_r, x_i = xr_ref[rows, ls], xi_ref[rows, ls]
                ar_ = ar * hr - ai * hi + x_r
                ai_ = ar * hi + ai * hr + x_i
                sr, si = pltpu.roll(ar_, nb, 0), pltpu.roll(ai_, nb, 0)
                br_ = ar * sr - ai * si + x_r
                bi_ = ar * si + ai * sr + x_i
                xr_ref[rows, ls] = jnp.where(lower, ar_, br_)
                xi_ref[rows, ls] = jnp.where(lower, ai_, bi_)
                return pltpu.roll(br_, nb, 0), pltpu.roll(bi_, nb, 0)

            nr, ni = lax.fori_loop(0, tc // 2, pair, (hr_ref[:, ls], hi_ref[:, ls]), unroll=2)
        hr_ref[:, ls] = nr
        hi_ref[:, ls] = ni

    ys = []
    for c in range(S5_COLS):
        cs = pl.ds(c * cw, cw)
        ys.append(_dot(xr_ref[:, cs].astype(BF16), wcr_ref[c])
                  + _dot(xi_ref[:, cs].astype(BF16), wci_ref[c]))
    y = jnp.concatenate(ys, axis=1) + d_ref[...] * u
    y = _gelu(y)
    z = _dot(y.astype(BF16), wglu_ref[...]) + bglu_ref[...]
    o_ref[...] = (y * jax.nn.sigmoid(z)).astype(BF16)


def _s5_call(z, prm, h0r, h0i, nb, tc, row_block0, n_chunks, y_prev=None):
    m = z.shape[0]
    d_a = prm["d"].shape[1]
    n_state = prm["ar"].shape[1]
    r = nb * tc
    sr = h0r.shape[0]
    fix2 = lambda i: (0, 0)
    fix3 = lambda i: (0, 0, 0)
    blk = lambda i: (row_block0 + i, 0)
    ins = [z, prm["wb"], prm["ar"], prm["ai"], prm["wcr"], prm["wci"], prm["d"], prm["wglu"],
           prm["bglu"], h0r, h0i]
    specs = [pl.BlockSpec((r, d_a), blk),
             pl.BlockSpec(prm["wb"].shape, fix3),
             pl.BlockSpec((1, n_state), fix2), pl.BlockSpec((1, n_state), fix2),
             pl.BlockSpec(prm["wcr"].shape, fix3), pl.BlockSpec(prm["wci"].shape, fix3),
             pl.BlockSpec((1, d_a), fix2), pl.BlockSpec((d_a, d_a), fix2),
             pl.BlockSpec((1, d_a), fix2),
             pl.BlockSpec((sr, n_state), fix2), pl.BlockSpec((sr, n_state), fix2)]
    aliases = {}
    if y_prev is not None:
        ins.append(y_prev)
        specs.append(pl.BlockSpec(memory_space=pl.ANY))
        aliases = {len(ins) - 1: 0}
    return pl.pallas_call(
        functools.partial(_s5_kernel, nb=nb, tc=tc, aliased=y_prev is not None),
        out_shape=(jax.ShapeDtypeStruct((m, d_a), BF16),
                   jax.ShapeDtypeStruct((sr, n_state), F32),
                   jax.ShapeDtypeStruct((sr, n_state), F32)),
        grid=(n_chunks,),
        in_specs=specs,
        out_specs=(pl.BlockSpec((r, d_a), blk),
                   pl.BlockSpec((sr, n_state), fix2), pl.BlockSpec((sr, n_state), fix2)),
        scratch_shapes=[pltpu.VMEM((r, n_state), F32), pltpu.VMEM((r, n_state), F32)],
        input_output_aliases=aliases,
        compiler_params=_cp("arbitrary"),
        name="s5",
    )(*ins)


def _s5_params(lam_re, lam_im, b_re, b_im, c_re, c_im, d_skip, log_step, w_glu, b_glu):
    g, n = lam_re.shape
    gs = b_re.shape[2]
    step = jnp.exp(log_step)[:, None]
    mag = jnp.exp(lam_re * step)
    abar_r, abar_i = mag * jnp.cos(lam_im * step), mag * jnp.sin(lam_im * step)
    den = lam_re * lam_re + lam_im * lam_im
    pr, pim = abar_r - 1.0, abar_i
    zr = (pr * lam_re + pim * lam_im) / den
    zi = (pim * lam_re - pr * lam_im) / den
    bbr = zr[..., None] * b_re - zi[..., None] * b_im
    bbi = zr[..., None] * b_im + zi[..., None] * b_re
    gl = LANES // gs
    nc = g // gl
    eye = jnp.eye(gl, dtype=F32)

    def in_mat(bb):
        return jnp.einsum("ab,cank->cakbn", eye, bb.reshape(nc, gl, n, gs)).reshape(nc, gl * gs, gl * n)

    def out_mat(cc):
        return jnp.einsum("ab,cakn->canbk", eye, cc.reshape(nc, gl, gs, n)).reshape(nc, gl * n, gl * gs)

    return dict(
        wb=jnp.concatenate([in_mat(bbr), in_mat(bbi)], axis=-1).astype(BF16),
        wcr=out_mat(c_re).astype(BF16),
        wci=out_mat(-c_im).astype(BF16),
        ar=abar_r.reshape(1, g * n), ai=abar_i.reshape(1, g * n),
        d=d_skip.reshape(1, -1), wglu=w_glu.astype(BF16), bglu=b_glu.reshape(1, -1))


def _log_f(fz, log_lb, log1m_lb):
    b = log1m_lb + _log_sigmoid(fz)
    a = jnp.broadcast_to(log_lb, b.shape)
    return jnp.maximum(a, b) + jnp.log1p(jnp.exp(-jnp.abs(a - b)))


def _hgrn_consts(tc, nb):
    r = tc * nb
    levels = int(np.log2(tc))
    t = np.arange(r) // nb
    b = np.arange(r) % nb
    same_b = b[:, None] == b[None, :]
    tt, ts = t[:, None], t[None, :]
    expo = np.zeros((levels + 2, r, r), np.float32)
    mask = np.zeros((levels + 1, r, r), np.float32)
    for l in range(levels):
        blk, half = 2 << l, 1 << l
        split = (t // blk) * blk + half - 1
        upper = (t % blk) >= half
        sp = split[:, None]
        up = upper[:, None]
        expo[l] = same_b & np.where(up, (ts > sp) & (ts <= tt), (ts > tt) & (ts <= sp))
        mask[l] = same_b & ((tt // blk) == (ts // blk)) & up & ~upper[None, :]
    expo[levels] = same_b & (ts <= tt)
    expo[levels + 1] = same_b & (ts > tt)
    mask[levels] = np.eye(r, dtype=np.float32)
    bmask = np.zeros((nb, r, LANES), np.float32)
    for k in range(nb):
        bmask[k, b == k, :] = 1.0
    return expo, mask, bmask


def _hgrn_kernel(q_ref, fz_ref, v_ref, g_ref, llb_ref, l1m_ref, om_ref, gw_ref, expo_ref, mask_ref,
                 bm_ref, s0_ref, o_ref, st_ref, sc_ref, gs_ref, *, nb):
    levels = mask_ref.shape[0] - 1
    hd = HG_DIM

    @pl.when(pl.program_id(0) == 0)
    def _():
        st_ref[...] = s0_ref[...]

    fz = fz_ref[...]
    lf = _log_f(fz, llb_ref[...], l1m_ref[...])
    kk = om_ref[...] * jax.nn.sigmoid(-fz)
    q = q_ref[...]
    lf_hi = lf.astype(BF16)
    lf_lo = (lf - lf_hi.astype(F32)).astype(BF16)

    def expo(l):
        e = expo_ref[l]
        return _dot(e, lf_hi) + _dot(e, lf_lo)

    qb, kb = q.astype(BF16), kk.astype(BF16)
    for h in range(HG_HEADS):
        hs = slice(h * hd, (h + 1) * hd)
        sc_ref[h] = mask_ref[levels] * _dot_nt(qb[:, hs], kb[:, hs])
    for l in range(levels):
        e = jnp.exp(expo(l))
        qt = (q * e).astype(BF16)
        kt = (kk * e).astype(BF16)
        for h in range(HG_HEADS):
            hs = slice(h * hd, (h + 1) * hd)
            sc_ref[h] += mask_ref[l] * _dot_nt(qt[:, hs], kt[:, hs])

    gcum = expo(levels)
    gs_ref[...] = gcum
    qg = (q * jnp.exp(gcum)).astype(BF16)
    kend = (kk * jnp.exp(expo(levels + 1))).astype(BF16)
    v = v_ref[...]
    vb = v.astype(BF16)
    r = q.shape[0]
    gw = gw_ref[...]
    for h in range(HG_HEADS):
        hs = slice(h * hd, (h + 1) * hd)
        st = st_ref[h]
        o = _dot(sc_ref[h].astype(BF16), vb[:, hs])
        oi = _dot_nt(qg[:, hs], st.astype(BF16))
        for k in range(nb):
            o = o + bm_ref[k] * oi[:, k * hd:(k + 1) * hd]
        vcat = jnp.concatenate([(v[:, hs] * bm_ref[k]).astype(BF16) for k in range(nb)], axis=1)
        upd = _dot_tn(vcat, kend[:, hs])
        dec = jnp.concatenate(
            [jnp.broadcast_to(jnp.exp(gs_ref[r - nb + k:r - nb + k + 1, hs]), (hd, hd)) for k in range(nb)],
            axis=0)
        st_ref[h] = dec * st + upd
        o = o * lax.rsqrt(jnp.mean(o * o, axis=-1, keepdims=True) + EPS) * gw
        o_ref[:, hs] = (o * _silu(g_ref[:, hs])).astype(BF16)


def _hgrn_call(z, lbp, gnorm_w, s0t, nb, n_chunks):
    m = z.shape[0]
    d_b = HG_HEADS * HG_DIM
    r = HG_CHUNK_T * nb
    expo, mask, bmask = _hgrn_consts(HG_CHUNK_T, nb)
    fix2 = lambda i: (0, 0)
    fix3 = lambda i: (0, 0, 0)
    col = lambda c: pl.BlockSpec((r, d_b), functools.partial(lambda i, c: (i, c), c=c))
    vec = pl.BlockSpec((1, d_b), fix2)
    return pl.pallas_call(
        functools.partial(_hgrn_kernel, nb=nb),
        out_shape=(jax.ShapeDtypeStruct((m, d_b), BF16),
                   jax.ShapeDtypeStruct(s0t.shape, F32)),
        grid=(n_chunks,),
        in_specs=[col(1), col(2), col(3), col(4), vec, vec, vec,
                  pl.BlockSpec((1, HG_DIM), fix2),
                  pl.BlockSpec(expo.shape, fix3), pl.BlockSpec(mask.shape, fix3),
                  pl.BlockSpec(bmask.shape, fix3), pl.BlockSpec(s0t.shape, fix3)],
        out_specs=(pl.BlockSpec((r, d_b), lambda i: (i, 0)), pl.BlockSpec(s0t.shape, fix3)),
        scratch_shapes=[pltpu.VMEM((HG_HEADS, r, r), F32), pltpu.VMEM((r, d_b), F32)],
        compiler_params=_cp("arbitrary"),
        name="hgrn",
    )(z, z, z, z, lbp["log_lb"], lbp["log1m_lb"], lbp["one_m_lb"], gnorm_w.reshape(1, -1),
      jnp.asarray(expo, BF16), jnp.asarray(mask, F32), jnp.asarray(bmask, F32), s0t)


def _hgrn_dec_kernel(*refs, aliased):
    q_ref, fz_ref, v_ref, g_ref, llb_ref, l1m_ref, om_ref, gw_ref, s_ref = refs[:9]
    o_ref, so_ref = refs[11:] if aliased else refs[9:]
    hd = HG_DIM
    fz = fz_ref[...]
    f = jnp.exp(_log_f(fz, llb_ref[...], l1m_ref[...]))
    kk = om_ref[...] * jax.nn.sigmoid(-fz)
    q = q_ref[...]
    v = v_ref[...]
    gw = gw_ref[...]
    eye = (lax.broadcasted_iota(jnp.int32, (hd, hd), 0)
           == lax.broadcasted_iota(jnp.int32, (hd, hd), 1)).astype(F32)

    def col(x_row):
        return jnp.sum(eye * x_row, axis=1, keepdims=True)

    rows = []
    for j in range(q.shape[0]):
        outs = []
        for h in range(HG_HEADS):
            hs = slice(h * hd, (h + 1) * hd)
            s_new = col(f[j:j + 1, hs]) * s_ref[0, j, h] + col(kk[j:j + 1, hs]) * v[j:j + 1, hs]
            so_ref[0, j, h] = s_new
            o = jnp.sum(col(q[j:j + 1, hs]) * s_new, axis=0, keepdims=True)
            outs.append(o * lax.rsqrt(jnp.mean(o * o, axis=-1, keepdims=True) + EPS) * gw)
        rows.append(jnp.concatenate(outs, axis=1))
    o_all = jnp.concatenate(rows, axis=0)
    o_ref[...] = (o_all * _silu(g_ref[...])).astype(BF16)


def _hgrn_dec_call(z, row0, n_rows, lbp, gnorm_w, s_all, j, y_prev, s_prev, sb=SUBLANES):
    d_b = HG_HEADS * HG_DIM
    rb0 = row0 // sb
    fix2 = lambda i: (0, 0)
    col = lambda c: pl.BlockSpec((sb, d_b), functools.partial(lambda i, c: (rb0 + i, c), c=c))
    vec = pl.BlockSpec((1, d_b), fix2)
    sblk = pl.BlockSpec((1, sb, HG_HEADS, HG_DIM, HG_DIM), lambda i: (j, i, 0, 0, 0))
    ins = [z, z, z, z, lbp["log_lb"], lbp["log1m_lb"], lbp["one_m_lb"], gnorm_w.reshape(1, -1), s_all,
           y_prev]
    specs = [col(1), col(2), col(3), col(4), vec, vec, vec, pl.BlockSpec((1, HG_DIM), fix2), sblk,
             pl.BlockSpec(memory_space=pl.ANY)]
    aliases = {9: 0}
    if s_prev is not None:
        ins.append(s_prev)
        specs.append(pl.BlockSpec(memory_space=pl.ANY))
        aliases[10] = 1
    else:
        ins.append(jnp.zeros((SUBLANES, LANES), F32))
        specs.append(pl.BlockSpec((SUBLANES, LANES), fix2))
    return pl.pallas_call(
        functools.partial(_hgrn_dec_kernel, aliased=True),
        out_shape=(jax.ShapeDtypeStruct(y_prev.shape, BF16), jax.ShapeDtypeStruct(s_all.shape, F32)),
        grid=(n_rows // sb,),
        in_specs=specs,
        out_specs=(pl.BlockSpec((sb, d_b), lambda i: (rb0 + i, 0)), sblk),
        input_output_aliases=aliases,
        compiler_params=_cp("parallel"),
        name="hgrn_dec",
    )(*ins)


def _lru_coeffs(xc, wg_ref, bga_ref, bgx_ref, lam_ref, a_ref, b_ref):
    sw = LRU_SUPER
    for j in range(xc.shape[1] // sw):
        cs = slice(j * sw, (j + 1) * sw)
        xj = xc[:, cs]
        gj = _dot(xj.astype(BF16), wg_ref[j])
        rg = jax.nn.sigmoid(gj[:, :sw] + bga_ref[:, cs])
        ig = jax.nn.sigmoid(gj[:, sw:] + bgx_ref[:, cs])
        log_a = LRU_C * rg * _log_sigmoid(lam_ref[:, cs])
        a = jnp.exp(log_a)
        a_ref[:, cs] = a
        b_ref[:, cs] = jnp.sqrt(-jnp.tanh(log_a) * (a * a + 1.0)) * ig * xj


def _lru_kernel(gb_ref, xr_ref, cw_ref, cb_ref, wg_ref, bga_ref, bgx_ref, lam_ref, c0_ref, h0_ref,
                o_ref, hl_ref, xs_ref, a_ref, b_ref, *, nb):
    r = xr_ref.shape[0]
    hdr = xs_ref.shape[0] - r
    kw = cw_ref.shape[0]

    @pl.when(pl.program_id(0) == 0)
    def _():
        xs_ref[0:hdr, :] = c0_ref[...]
        hl_ref[...] = h0_ref[...]

    xs_ref[hdr:hdr + r, :] = xr_ref[...]
    xc = cb_ref[...] + xs_ref[hdr - (kw - 1) * nb:hdr - (kw - 1) * nb + r, :] * cw_ref[0:1, :]
    for j in range(1, kw):
        xc = xc + xs_ref[hdr - (kw - 1 - j) * nb:hdr - (kw - 1 - j) * nb + r, :] * cw_ref[j:j + 1, :]
    xs_ref[0:hdr, :] = xs_ref[r:r + hdr, :]

    _lru_coeffs(xc, wg_ref, bga_ref, bgx_ref, lam_ref, a_ref, b_ref)

    lower = lax.broadcasted_iota(jnp.int32, (SUBLANES, xc.shape[1]), 0) < nb

    def pair(k, h):
        rows = pl.ds(pl.multiple_of(k * SUBLANES, SUBLANES), SUBLANES)
        a8 = a_ref[rows, :]
        b8 = b_ref[rows, :]
        h_a = a8 * h + b8
        h_b = a8 * pltpu.roll(h_a, nb, 0) + b8
        b_ref[rows, :] = jnp.where(lower, h_a, h_b)
        return pltpu.roll(h_b, nb, 0)

    hl_ref[...] = lax.fori_loop(0, r // SUBLANES, pair, hl_ref[...], unroll=2)
    o_ref[...] = (_gelu(gb_ref[...]) * b_ref[...]).astype(BF16)


def _lru_call(z, prm, conv0, h0, nb, n_chunks):
    m = z.shape[0]
    d_rnn = prm["lam"].shape[1]
    r = nb * LRU_CHUNK_T
    hdr = conv0.shape[0]
    fix2 = lambda i: (0, 0)
    fix3 = lambda i: (0, 0, 0)
    vec = pl.BlockSpec((1, d_rnn), fix2)
    return pl.pallas_call(
        functools.partial(_lru_kernel, nb=nb),
        out_shape=(jax.ShapeDtypeStruct((m, d_rnn), BF16),
                   jax.ShapeDtypeStruct((SUBLANES, d_rnn), F32)),
        grid=(n_chunks,),
        in_specs=[pl.BlockSpec((r, d_rnn), lambda i: (i, 0)),
                  pl.BlockSpec((r, d_rnn), lambda i: (i, 1)),
                  pl.BlockSpec(prm["cw"].shape, fix2), vec,
                  pl.BlockSpec(prm["wg"].shape, fix3), vec, vec, vec,
                  pl.BlockSpec((hdr, d_rnn), fix2), pl.BlockSpec((SUBLANES, d_rnn), fix2)],
        out_specs=(pl.BlockSpec((r, d_rnn), lambda i: (i, 0)),
                   pl.BlockSpec((SUBLANES, d_rnn), fix2)),
        scratch_shapes=[pltpu.VMEM((hdr + r, d_rnn), F32), pltpu.VMEM((r, d_rnn), F32),
                        pltpu.VMEM((r, d_rnn), F32)],
        compiler_params=_cp("arbitrary"),
        name="lru",
    )(z, z, prm["cw"], prm["cb"], prm["wg"], prm["bga"], prm["bgx"], prm["lam"], conv0, h0)


def _lru_dec_kernel(gb_ref, xr_ref, buf_ref, cw_ref, cb_ref, wg_ref, bga_ref, bgx_ref, lam_ref,
                    h0_ref, yp_ref, o_ref, h_ref, a_ref, b_ref):
    kw = cw_ref.shape[0]
    xc = cb_ref[...] + xr_ref[...] * cw_ref[kw - 1:kw, :]
    for j in range(kw - 1):
        xc = xc + buf_ref[j] * cw_ref[j:j + 1, :]
    _lru_coeffs(xc, wg_ref, bga_ref, bgx_ref, lam_ref, a_ref, b_ref)
    h = a_ref[...] * h0_ref[...] + b_ref[...]
    h_ref[...] = h
    o_ref[...] = (_gelu(gb_ref[...]) * h).astype(BF16)


def _lru_dec_call(z, row0, prm, buf, h0, y_prev):
    n = h0.shape[0]
    d_rnn = prm["lam"].shape[1]
    rb0 = row0 // n
    fix2 = lambda i: (0, 0)
    fix3 = lambda i: (0, 0, 0)
    vec = pl.BlockSpec((1, d_rnn), fix2)
    full = pl.BlockSpec((n, d_rnn), fix2)
    return pl.pallas_call(
        _lru_dec_kernel,
        out_shape=(jax.ShapeDtypeStruct(y_prev.shape, BF16), jax.ShapeDtypeStruct((n, d_rnn), F32)),
        grid=(1,),
        in_specs=[pl.BlockSpec((n, d_rnn), lambda i: (rb0, 0)),
                  pl.BlockSpec((n, d_rnn), lambda i: (rb0, 1)),
                  pl.BlockSpec(buf.shape, fix3),
                  pl.BlockSpec(prm["cw"].shape, fix2), vec,
                  pl.BlockSpec(prm["wg"].shape, fix3), vec, vec, vec, full,
                  pl.BlockSpec(memory_space=pl.ANY)],
        out_specs=(pl.BlockSpec((n, d_rnn), lambda i: (rb0, 0)), full),
        scratch_shapes=[pltpu.VMEM((n, d_rnn), F32), pltpu.VMEM((n, d_rnn), F32)],
        input_output_aliases={10: 0},
        compiler_params=_cp("arbitrary"),
        name="lru_dec",
    )(z, z, buf, prm["cw"], prm["cb"], prm["wg"], prm["bga"], prm["bgx"], prm["lam"], h0, y_prev)


def _lru_params(conv_w, conv_b, w_ga, b_ga, w_gx, b_gx, lam):
    nblk, bs = w_ga.shape[0], w_ga.shape[1]
    per = LRU_SUPER // bs
    ns = nblk // per
    eye = jnp.eye(per, dtype=F32)

    def sup(w):
        return jnp.einsum("ab,sajk->sajbk", eye, w.reshape(ns, per, bs, bs)).reshape(ns, per * bs, per * bs)

    return dict(cw=conv_w, cb=conv_b.reshape(1, -1),
                wg=jnp.concatenate([sup(w_ga), sup(w_gx)], axis=-1).astype(BF16),
                bga=b_ga.reshape(1, -1), bgx=b_gx.reshape(1, -1), lam=lam.reshape(1, -1))


def kernel(x_prompt, x_sample, state_s5_re, state_s5_im, state_hgrn, state_lru, state_conv,
           c_prompt, c_sample, norm_w, final_norm_w, w_ada, b_ada, w_ffn_gu, w_ffn_d,
           w_in_ab, s5_lam_re, s5_lam_im, s5_b_re, s5_b_im, s5_c_re, s5_c_im, s5_d, s5_log_step,
           s5_w_glu, s5_b_glu, hg_lb_logits, hg_norm_w, w_out_ab, w_in_c, conv_w, conv_b,
           w_gate_a, b_gate_a, w_gate_x, b_gate_x, lru_lambda, w_out_c):
    bsz, seq, d = x_prompt.shape
    nsm = x_sample.shape[0]
    depth = w_ada.shape[0]
    n_ab, n_c = w_in_ab.shape[0], w_in_c.shape[0]
    g_a, n_a = s5_lam_re.shape[1], s5_lam_re.shape[2]
    n_state = g_a * n_a
    d_rnn = lru_lambda.shape[1]
    kw = conv_w.shape[1]
    mp_rows = seq * bsz
    assert 2 * bsz == SUBLANES and mp_rows % nsm == 0 and (mp_rows + nsm) % ROW_TILE == 0

    s5p = [_s5_params(s5_lam_re[j], s5_lam_im[j], s5_b_re[j], s5_b_im[j], s5_c_re[j], s5_c_im[j],
                      s5_d[j], s5_log_step[j], s5_w_glu[j], s5_b_glu[j]) for j in range(n_ab)]
    lb_all = jnp.cumsum(jax.nn.softmax(hg_lb_logits.astype(F32), axis=0), axis=0)
    lb_all = lb_all - lb_all[0:1]
    lbp = [dict(log_lb=jnp.log(lb_all[j]).reshape(1, -1),
                log1m_lb=jnp.log1p(-lb_all[j]).reshape(1, -1),
                one_m_lb=(1.0 - lb_all[j]).reshape(1, -1)) for j in range(n_ab)]
    lrup = [_lru_params(conv_w[j], conv_b[j], w_gate_a[j], b_gate_a[j], w_gate_x[j], b_gate_x[j],
                        lru_lambda[j]) for j in range(n_c)]
    w_out_ab_bf = w_out_ab.astype(BF16)
    w_out_c_bf = w_out_c.astype(BF16)

    pad = (-bsz) % SUBLANES
    c_all = jnp.concatenate([c_prompt, jnp.zeros((pad, d), F32), c_sample], axis=0)
    mods = _ada_call(c_all, w_ada, b_ada)
    mods_p = jnp.concatenate([mods[:, :bsz]] * (SUBLANES // bsz), axis=1)
    mods_s = mods[:, bsz + pad:]

    x = jnp.concatenate([jnp.transpose(x_prompt, (1, 0, 2)).reshape(mp_rows, d),
                         x_sample.reshape(nsm, d)], axis=0)
    zeros_state = jnp.zeros((SUBLANES, n_state), F32)
    s5r_p, s5i_p, hg_p, lru_p, conv_p = [], [], [], [], []
    s5r_s, s5i_s, lru_s, conv_s = [], [], [], []
    hg_s = None
    for l in range(depth):
        j = l // 2
        mvp = [mods_p[l, :, k * d:(k + 1) * d] for k in range(N_MOD)]
        mvs = [mods_s[l, :, k * d:(k + 1) * d] for k in range(N_MOD)]
        nw = norm_w[l]
        x = _ffn_call(x, nw[0:1], mvp[0:3], mvs[0:3], w_ffn_gu, w_ffn_d, l, 0, ROW_TILE, 256)
        if l % 2 == 0:
            z = _inproj_call(x, nw[1:2], mvp[3:5], mvs[3:5], w_in_ab, j, ROW_TILE, 1024)
            ya, hr, hi = _s5_call(z, s5p[j], zeros_state, zeros_state, bsz, S5_CHUNK_T,
                                  0, seq // S5_CHUNK_T)
            ya, hrs, his = _s5_call(z, s5p[j], state_s5_re[j].reshape(nsm, n_state),
                                    state_s5_im[j].reshape(nsm, n_state), nsm, 1,
                                    mp_rows // nsm, 1, y_prev=ya)
            yb, hg = _hgrn_call(z, lbp[j], hg_norm_w[j],
                                jnp.zeros((HG_HEADS, bsz * HG_DIM, HG_DIM), F32), bsz, seq // HG_CHUNK_T)
            yb, hg_s = _hgrn_dec_call(z, mp_rows, nsm, lbp[j], hg_norm_w[j], state_hgrn, j, yb, hg_s)
            s5r_p.append(hr[:bsz].reshape(bsz, g_a, n_a))
            s5i_p.append(hi[:bsz].reshape(bsz, g_a, n_a))
            s5r_s.append(hrs.reshape(nsm, g_a, n_a))
            s5i_s.append(his.reshape(nsm, g_a, n_a))
            hg_p.append(jnp.transpose(hg.reshape(HG_HEADS, bsz, HG_DIM, HG_DIM), (1, 0, 3, 2)))
            x = _outproj_call(x, mvp[5], mvs[5], [ya, yb], w_out_ab_bf[j], OUT_ROW_TILE)
        else:
            z = _inproj_call(x, nw[1:2], mvp[3:5], mvs[3:5], w_in_c, j, ROW_TILE, 1024)
            y, hl = _lru_call(z, lrup[j], jnp.zeros((4 * bsz, d_rnn), F32),
                              jnp.zeros((SUBLANES, d_rnn), F32), bsz, seq // LRU_CHUNK_T)
            y, hls = _lru_dec_call(z, mp_rows, lrup[j], jnp.transpose(state_conv[j], (1, 0, 2)),
                                   state_lru[j], y)
            lru_p.append(hl[:bsz])
            lru_s.append(hls)
            tail = z[mp_rows - (kw - 1) * bsz:mp_rows, d_rnn:]
            conv_p.append(jnp.transpose(tail.reshape(kw - 1, bsz, d_rnn), (1, 0, 2)))
            conv_s.append(jnp.concatenate([state_conv[j][:, 1:], z[mp_rows:, None, d_rnn:]], axis=1))
            x = _outproj_call(x, mvp[5], mvs[5], [y], w_out_c_bf[j], OUT_ROW_TILE)
        x = _ffn_call(x, nw[2:3], mvp[6:9], mvs[6:9], w_ffn_gu, w_ffn_d, l, 1, ROW_TILE, 256)
    y_all = _final_norm_call(x, final_norm_w.reshape(1, -1), ROW_TILE)
    y_prompt = jnp.transpose(y_all[:mp_rows].reshape(seq, bsz, d), (1, 0, 2))
    y_sample = y_all[mp_rows:].reshape(nsm, 1, d)
    return (y_prompt, y_sample, jnp.stack(s5r_p), jnp.stack(s5i_p), jnp.stack(hg_p),
            jnp.stack(lru_p), jnp.stack(conv_p),
            jnp.stack(s5r_s), jnp.stack(s5i_s), hg_s, jnp.stack(lru_s), jnp.stack(conv_s))
```

```python
import functools

import numpy as np
import jax
import jax.numpy as jnp
from jax import lax
from jax.experimental import pallas as pl
from jax.experimental.pallas import tpu as pltpu

F32 = jnp.float32
BF16 = jnp.bfloat16
EPS = 1e-6
LRU_C = 8.0

VMEM_LIMIT_BYTES = 60 * 1024 * 1024
SUBLANES = 8
LANES = 128
BF16_ROWS = 16

S5_COLS = 8
S5_CHUNK_T = 128
HG_HEADS = 8
HG_DIM = 128
HG_CHUNK_T = 64
LRU_CHUNK_T = 64
LRU_SUPER = 640
N_MOD = 9
ROW_TILE = 1040
OUT_ROW_TILE = 640
FINAL_ROW_TILE = 1024


def _cp(*sem):
    return pltpu.CompilerParams(dimension_semantics=sem, vmem_limit_bytes=VMEM_LIMIT_BYTES)


def _dot(a, b):
    return jnp.dot(a, b, preferred_element_type=F32)


def _dot_nt(a, b):
    return lax.dot_general(a, b, (((1,), (1,)), ((), ())), preferred_element_type=F32)


def _dot_tn(a, b):
    return lax.dot_general(a, b, (((0,), (0,)), ((), ())), preferred_element_type=F32)


def _silu(x):
    return x * jax.nn.sigmoid(x)


def _gelu(x):
    return jax.nn.gelu(x, approximate=True)


def _log_sigmoid(x):
    return jnp.minimum(x, 0.0) - jnp.log1p(jnp.exp(-jnp.abs(x)))


def _split3(x):
    hi = x.astype(BF16)
    r1 = x - hi.astype(F32)
    mid = r1.astype(BF16)
    lo = (r1 - mid.astype(F32)).astype(BF16)
    return hi, mid, lo


def _fma_rows(y, mul, add):
    rm = mul.shape[0]
    r, d = y.shape
    if rm == r:
        return y * mul + add
    y3 = y.reshape(r // rm, rm, d)
    return (y3 * mul[None] + add[None]).reshape(r, d)


def _mul_rows(y, mul):
    rm = mul.shape[0]
    r, d = y.shape
    if rm == r:
        return y * mul
    return (y.reshape(r // rm, rm, d) * mul[None]).reshape(r, d)


def _norm_mod_rows(x_ref, nw_ref, sh_ref, sc_ref, h_ref, row0, nrows):
    slab = BF16_ROWS
    per_row = sh_ref.shape[0] > SUBLANES
    nw = nw_ref[...]

    def body(s, carry):
        r0 = pl.multiple_of(row0 + s * slab, slab)
        x = x_ref[pl.ds(r0, slab), :]
        ms = jnp.mean(x * x, axis=-1, keepdims=True)
        y = x * lax.rsqrt(ms + EPS) * nw
        if per_row:
            m0 = pl.multiple_of(s * slab, slab)
            h = y * (1.0 + sc_ref[pl.ds(m0, slab), :]) + sh_ref[pl.ds(m0, slab), :]
        else:
            h = _fma_rows(y, 1.0 + sc_ref[...], sh_ref[...])
        h_ref[pl.ds(r0, slab), :] = h.astype(BF16)
        return carry

    lax.fori_loop(0, nrows // slab, body, 0, unroll=8)


def _residual_rows(o_ref, x_ref, acc, gt_ref, gts_ref, scale):
    tm = o_ref.shape[0]
    ns = gts_ref.shape[0]
    np_ = tm - ns
    is_last = pl.program_id(0) == pl.num_programs(0) - 1
    gp = scale * gt_ref[...]
    o_ref[0:np_, :] = x_ref[0:np_, :] + _mul_rows(acc(0, np_), gp)

    @pl.when(jnp.logical_not(is_last))
    def _():
        o_ref[np_:tm, :] = x_ref[np_:tm, :] + _mul_rows(acc(np_, tm), gp)

    @pl.when(is_last)
    def _():
        o_ref[np_:tm, :] = x_ref[np_:tm, :] + acc(np_, tm) * (scale * gts_ref[...])


def _ada_kernel(c_ref, w_ref, b_ref, o_ref):
    sc = _silu(c_ref[...]).astype(BF16)
    o_ref[0] = _dot(sc, w_ref[0].astype(BF16)) + b_ref[0]


def _ada_call(c_all, w_ada, b_ada, tn=1024):
    depth, d, n = w_ada.shape
    r = c_all.shape[0]
    return pl.pallas_call(
        _ada_kernel,
        out_shape=jax.ShapeDtypeStruct((depth, r, n), F32),
        grid=(depth, n // tn),
        in_specs=[pl.BlockSpec((r, d), lambda l, j: (0, 0)),
                  pl.BlockSpec((1, d, tn), lambda l, j: (l, 0, j)),
                  pl.BlockSpec((1, 1, tn), lambda l, j: (l, 0, j))],
        out_specs=pl.BlockSpec((1, r, tn), lambda l, j: (l, 0, j)),
        compiler_params=_cp("parallel", "parallel"),
        name="ada",
    )(c_all, w_ada, b_ada.reshape(depth, 1, n))


def _ffn_kernel(x_ref, nw_ref, sh_ref, sc_ref, gt_ref, shs_ref, scs_ref, gts_ref,
                wg_ref, wu_ref, wd_ref, o_ref, h_ref):
    i, f = pl.program_id(0), pl.program_id(1)
    tm = x_ref.shape[0]
    ns = shs_ref.shape[0]
    is_last = i == pl.num_programs(0) - 1

    @pl.when(f == 0)
    def _():
        _norm_mod_rows(x_ref, nw_ref, sh_ref, sc_ref, h_ref, 0, tm)
        o_ref[...] = jnp.zeros_like(o_ref)

    @pl.when(jnp.logical_and(f == 0, is_last))
    def _():
        _norm_mod_rows(x_ref, nw_ref, shs_ref, scs_ref, h_ref, tm - ns, ns)

    h = h_ref[...]
    g = _dot(h, wg_ref[0, 0].astype(BF16))
    u = _dot(h, wu_ref[0, 0].astype(BF16))
    a = (_silu(g) * u).astype(BF16)
    o_ref[...] += _dot(a, wd_ref[0, 0].astype(BF16))

    @pl.when(f == pl.num_programs(1) - 1)
    def _():
        _residual_rows(o_ref, x_ref, lambda a0, a1: o_ref[a0:a1, :], gt_ref, gts_ref, 0.5)


def _ffn_call(x, nw, mp, ms, w_gu, w_d, l, s, tm, tf):
    m, d = x.shape
    ff = w_d.shape[2]
    rm, ns = mp[0].shape[0], ms[0].shape[0]
    nf = ff // tf
    row = lambda i, f: (i, 0)
    fix = lambda i, f: (0, 0)
    pat = pl.BlockSpec((rm, d), fix)
    smp = pl.BlockSpec((ns, d), fix)
    return pl.pallas_call(
        _ffn_kernel,
        out_shape=jax.ShapeDtypeStruct((m, d), F32),
        grid=(m // tm, nf),
        in_specs=[pl.BlockSpec((tm, d), row), pl.BlockSpec((1, d), fix),
                  pat, pat, pat, smp, smp, smp,
                  pl.BlockSpec((1, 1, d, tf), lambda i, f: (l, s, 0, f)),
                  pl.BlockSpec((1, 1, d, tf), lambda i, f: (l, s, 0, f + nf)),
                  pl.BlockSpec((1, 1, tf, d), lambda i, f: (l, s, f, 0))],
        out_specs=pl.BlockSpec((tm, d), row),
        scratch_shapes=[pltpu.VMEM((tm, d), BF16)],
        compiler_params=_cp("parallel", "arbitrary"),
        name="ffn",
    )(x, nw, *mp, *ms, w_gu, w_gu, w_d)


def _inproj_kernel(x_ref, nw_ref, sh_ref, sc_ref, shs_ref, scs_ref, w_ref, o_ref, h_ref):
    i, k = pl.program_id(0), pl.program_id(1)
    tm = x_ref.shape[0]
    ns = shs_ref.shape[0]

    @pl.when(k == 0)
    def _():
        _norm_mod_rows(x_ref, nw_ref, sh_ref, sc_ref, h_ref, 0, tm)

    @pl.when(jnp.logical_and(k == 0, i == pl.num_programs(0) - 1))
    def _():
        _norm_mod_rows(x_ref, nw_ref, shs_ref, scs_ref, h_ref, tm - ns, ns)

    o_ref[...] = _dot(h_ref[...], w_ref[0].astype(BF16))


def _inproj_call(x, nw, mp, ms, w, j, tm, tn):
    m, d = x.shape
    n = w.shape[2]
    rm, ns = mp[0].shape[0], ms[0].shape[0]
    fix = lambda i, k: (0, 0)
    pat = pl.BlockSpec((rm, d), fix)
    smp = pl.BlockSpec((ns, d), fix)
    return pl.pallas_call(
        _inproj_kernel,
        out_shape=jax.ShapeDtypeStruct((m, n), F32),
        grid=(m // tm, n // tn),
        in_specs=[pl.BlockSpec((tm, d), lambda i, k: (i, 0)), pl.BlockSpec((1, d), fix),
                  pat, pat, smp, smp,
                  pl.BlockSpec((1, d, tn), lambda i, k: (j, 0, k))],
        out_specs=pl.BlockSpec((tm, tn), lambda i, k: (i, k)),
        scratch_shapes=[pltpu.VMEM((tm, d), BF16)],
        compiler_params=_cp("parallel", "arbitrary"),
        name="inproj",
    )(x, nw, mp[0], mp[1], ms[0], ms[1], w)


def _outproj_kernel(*refs, n_in):
    x_ref, gt_ref, gts_ref = refs[0], refs[1], refs[2]
    a_refs = refs[3:3 + n_in]
    w_refs = refs[3 + n_in:3 + 2 * n_in]
    o_ref = refs[3 + 2 * n_in]
    acc = _dot(a_refs[0][...], w_refs[0][...])
    for a_ref, w_ref in zip(a_refs[1:], w_refs[1:]):
        acc = acc + _dot(a_ref[...], w_ref[...])
    _residual_rows(o_ref, x_ref, lambda a0, a1: acc[a0:a1, :], gt_ref, gts_ref, 1.0)


def _outproj_call(x, gate_p, gate_s, acts, w_bf, tm):
    m, d = x.shape
    rm, ns = gate_p.shape[0], gate_s.shape[0]
    n_in = len(acts)
    ka = acts[0].shape[1]
    row = lambda i: (i, 0)
    fix = lambda i: (0, 0)
    in_specs = [pl.BlockSpec((tm, d), row), pl.BlockSpec((rm, d), fix), pl.BlockSpec((ns, d), fix)]
    in_specs += [pl.BlockSpec((tm, ka), row) for _ in acts]
    in_specs += [pl.BlockSpec((ka, d), functools.partial(lambda i, p: (p, 0), p=p)) for p in range(n_in)]
    return pl.pallas_call(
        functools.partial(_outproj_kernel, n_in=n_in),
        out_shape=jax.ShapeDtypeStruct((m, d), F32),
        grid=(m // tm,),
        in_specs=in_specs,
        out_specs=pl.BlockSpec((tm, d), row),
        compiler_params=_cp("parallel"),
        name="outproj",
    )(x, gate_p, gate_s, *acts, *([w_bf] * n_in))


def _final_norm_kernel(x_ref, w_ref, yp_ref, ys_ref, *, n_tiles):
    i = pl.program_id(0)
    w = w_ref[...]

    def norm(x):
        ms = jnp.mean(x * x, axis=-1, keepdims=True)
        return x * lax.rsqrt(ms + EPS) * w

    @pl.when(i < n_tiles)
    def _():
        yp_ref[...] = norm(x_ref[...])

    @pl.when(i == n_tiles)
    def _():
        ys_ref[...] = norm(x_ref[0:ys_ref.shape[0], :])


def _final_norm_call(x, w, mp_rows, ns, tm):
    d = x.shape[1]
    n_tiles = mp_rows // tm
    return pl.pallas_call(
        functools.partial(_final_norm_kernel, n_tiles=n_tiles),
        out_shape=(jax.ShapeDtypeStruct((mp_rows, d), F32), jax.ShapeDtypeStruct((ns, d), F32)),
        grid=(n_tiles + 1,),
        in_specs=[pl.BlockSpec((tm, d), lambda i: (i, 0)),
                  pl.BlockSpec((1, d), lambda i: (0, 0))],
        out_specs=(pl.BlockSpec((tm, d), lambda i: (jnp.minimum(i, n_tiles - 1), 0)),
                   pl.BlockSpec((ns, d), lambda i: (0, 0))),
        compiler_params=_cp("arbitrary"),
        name="final_norm",
    )(x, w)


def _s5_kernel(*refs, nb, tc, aliased):
    (u_ref, wb_ref, ar_ref, ai_ref, wcr_ref, wci_ref, d_ref, wglu_ref, bglu_ref,
     h0r_ref, h0i_ref) = refs[:11]
    o_ref, hr_ref, hi_ref, xr_ref, xi_ref = refs[12:] if aliased else refs[11:]

    @pl.when(pl.program_id(0) == 0)
    def _():
        hr_ref[...] = h0r_ref[...]
        hi_ref[...] = h0i_ref[...]

    u = u_ref[...]
    ub = u.astype(BF16)
    n_state = xr_ref.shape[1]
    cw = n_state // S5_COLS
    for c in range(S5_COLS):
        xc = _dot(ub[:, c * LANES:(c + 1) * LANES], wb_ref[c])
        xr_ref[:, c * cw:(c + 1) * cw] = xc[:, :cw]
        xi_ref[:, c * cw:(c + 1) * cw] = xc[:, cw:]

    lane_group = 1024
    for lo_ in range(0, n_state, lane_group):
        ls = pl.ds(lo_, lane_group)
        ar = ar_ref[:, ls]
        ai = ai_ref[:, ls]
        if tc == 1:
            hr, hi = hr_ref[:, ls], hi_ref[:, ls]
            nr = ar * hr - ai * hi + xr_ref[:, ls]
            ni = ar * hi + ai * hr + xi_ref[:, ls]
            xr_ref[:, ls] = nr
            xi_ref[:, ls] = ni
        else:
            lower = lax.broadcasted_iota(jnp.int32, (SUBLANES, lane_group), 0) < nb

            def pair(k, carry, ls=ls, ar=ar, ai=ai, lower=lower):
                hr, hi = carry
                rows = pl.ds(pl.multiple_of(k * SUBLANES, SUBLANES), SUBLANES)
                x_r, x_i = xr_ref[rows, ls], xi_ref[rows, ls]
                ar_ = ar * hr - ai * hi + x_r
                ai_ = ar * hi + ai * hr + x_i
                sr, si = pltpu.roll(ar_, nb, 0), pltpu.roll(ai_, nb, 0)
                br_ = ar * sr - ai * si + x_r
                bi_ = ar * si + ai * sr + x_i
                xr_ref[rows, ls] = jnp.where(lower, ar_, br_)
                xi_ref[rows, ls] = jnp.where(lower, ai_, bi_)
                return pltpu.roll(br_, nb, 0), pltpu.roll(bi_, nb, 0)

            nr, ni = lax.fori_loop(0, tc // 2, pair, (hr_ref[:, ls], hi_ref[:, ls]), unroll=2)
        hr_ref[:, ls] = nr
        hi_ref[:, ls] = ni

    ys = []
    for c in range(S5_COLS):
        cs = pl.ds(c * cw, cw)
        ys.append(_dot(xr_ref[:, cs].astype(BF16), wcr_ref[c])
                  + _dot(xi_ref[:, cs].astype(BF16), wci_ref[c]))
    y = jnp.concatenate(ys, axis=1) + d_ref[...] * u
    y = _gelu(y)
    z = _dot(y.astype(BF16), wglu_ref[...]) + bglu_ref[...]
    o_ref[...] = (y * jax.nn.sigmoid(z)).astype(BF16)


def _s5_call(z, prm, h0r, h0i, nb, tc, row_block0, n_chunks, y_prev=None):
    m = z.shape[0]
    d_a = prm["d"].shape[1]
    n_state = prm["ar"].shape[1]
    r = nb * tc
    sr = h0r.shape[0]
    fix2 = lambda i: (0, 0)
    fix3 = lambda i: (0, 0, 0)
    blk = lambda i: (row_block0 + i, 0)
    ins = [z, prm["wb"], prm["ar"], prm["ai"], prm["wcr"], prm["wci"], prm["d"], prm["wglu"],
           prm["bglu"], h0r, h0i]
    specs = [pl.BlockSpec((r, d_a), blk),
             pl.BlockSpec(prm["wb"].shape, fix3),
             pl.BlockSpec((1, n_state), fix2), pl.BlockSpec((1, n_state), fix2),
             pl.BlockSpec(prm["wcr"].shape, fix3), pl.BlockSpec(prm["wci"].shape, fix3),
             pl.BlockSpec((1, d_a), fix2), pl.BlockSpec((d_a, d_a), fix2),
             pl.BlockSpec((1, d_a), fix2),
             pl.BlockSpec((sr, n_state), fix2), pl.BlockSpec((sr, n_state), fix2)]
    aliases = {}
    if y_prev is not None:
        ins.append(y_prev)
        specs.append(pl.BlockSpec(memory_space=pl.ANY))
        aliases = {len(ins) - 1: 0}
    return pl.pallas_call(
        functools.partial(_s5_kernel, nb=nb, tc=tc, aliased=y_prev is not None),
        out_shape=(jax.ShapeDtypeStruct((m, d_a), BF16),
                   jax.ShapeDtypeStruct((sr, n_state), F32),
                   jax.ShapeDtypeStruct((sr, n_state), F32)),
        grid=(n_chunks,),
        in_specs=specs,
        out_specs=(pl.BlockSpec((r, d_a), blk),
                   pl.BlockSpec((sr, n_state), fix2), pl.BlockSpec((sr, n_state), fix2)),
        scratch_shapes=[pltpu.VMEM((r, n_state), F32), pltpu.VMEM((r, n_state), F32)],
        input_output_aliases=aliases,
        compiler_params=_cp("arbitrary"),
        name="s5",
    )(*ins)


def _s5_params(lam_re, lam_im, b_re, b_im, c_re, c_im, d_skip, log_step, w_glu, b_glu):
    g, n = lam_re.shape
    gs = b_re.shape[2]
    step = jnp.exp(log_step)[:, None]
    mag = jnp.exp(lam_re * step)
    abar_r, abar_i = mag * jnp.cos(lam_im * step), mag * jnp.sin(lam_im * step)
    den = lam_re * lam_re + lam_im * lam_im
    pr, pim = abar_r - 1.0, abar_i
    zr = (pr * lam_re + pim * lam_im) / den
    zi = (pim * lam_re - pr * lam_im) / den
    bbr = zr[..., None] * b_re - zi[..., None] * b_im
    bbi = zr[..., None] * b_im + zi[..., None] * b_re
    gl = LANES // gs
    nc = g // gl
    eye = jnp.eye(gl, dtype=F32)

    def in_mat(bb):
        return jnp.einsum("ab,cank->cakbn", eye, bb.reshape(nc, gl, n, gs)).reshape(nc, gl * gs, gl * n)

    def out_mat(cc):
        return jnp.einsum("ab,cakn->canbk", eye, cc.reshape(nc, gl, gs, n)).reshape(nc, gl * n, gl * gs)

    return dict(
        wb=jnp.concatenate([in_mat(bbr), in_mat(bbi)], axis=-1).astype(BF16),
        wcr=out_mat(c_re).astype(BF16),
        wci=out_mat(-c_im).astype(BF16),
        ar=abar_r.reshape(1, g * n), ai=abar_i.reshape(1, g * n),
        d=d_skip.reshape(1, -1), wglu=w_glu.astype(BF16), bglu=b_glu.reshape(1, -1))


def _log_f(fz, log_lb, log1m_lb):
    b = log1m_lb + _log_sigmoid(fz)
    a = jnp.broadcast_to(log_lb, b.shape)
    return jnp.maximum(a, b) + jnp.log1p(jnp.exp(-jnp.abs(a - b)))


def _hgrn_consts(tc, nb):
    r = tc * nb
    levels = int(np.log2(tc))
    t = np.arange(r) // nb
    b = np.arange(r) % nb
    same_b = b[:, None] == b[None, :]
    tt, ts = t[:, None], t[None, :]
    expo = np.zeros((levels + 2, r, r), np.float32)
    mask = np.zeros((levels + 1, r, r), np.float32)
    for l in range(levels):
        blk, half = 2 << l, 1 << l
        split = (t // blk) * blk + half - 1
        upper = (t % blk) >= half
        sp = split[:, None]
        up = upper[:, None]
        expo[l] = same_b & np.where(up, (ts > sp) & (ts <= tt), (ts > tt) & (ts <= sp))
        mask[l] = same_b & ((tt // blk) == (ts // blk)) & up & ~upper[None, :]
    expo[levels] = same_b & (ts <= tt)
    expo[levels + 1] = same_b & (ts > tt)
    mask[levels] = np.eye(r, dtype=np.float32)
    bmask = np.zeros((nb, r, LANES), np.float32)
    for k in range(nb):
        bmask[k, b == k, :] = 1.0
    return expo, mask, bmask


def _hgrn_kernel(q_ref, fz_ref, v_ref, g_ref, llb_ref, l1m_ref, om_ref, gw_ref, expo_ref, mask_ref,
                 bm_ref, s0_ref, o_ref, st_ref, sc_ref, gs_ref, *, nb):
    levels = mask_ref.shape[0] - 1
    hd = HG_DIM

    @pl.when(pl.program_id(0) == 0)
    def _():
        st_ref[...] = s0_ref[...]

    fz = fz_ref[...]
    lf = _log_f(fz, llb_ref[...], l1m_ref[...])
    kk = om_ref[...] * jax.nn.sigmoid(-fz)
    q = q_ref[...]
    lf_hi = lf.astype(BF16)
    lf_lo = (lf - lf_hi.astype(F32)).astype(BF16)

    def expo(l):
        e = expo_ref[l]
        return _dot(e, lf_hi) + _dot(e, lf_lo)

    qb, kb = q.astype(BF16), kk.astype(BF16)
    for h in range(HG_HEADS):
        hs = slice(h * hd, (h + 1) * hd)
        sc_ref[h] = mask_ref[levels] * _dot_nt(qb[:, hs], kb[:, hs])
    for l in range(levels):
        e = jnp.exp(expo(l))
        qt = (q * e).astype(BF16)
        kt = (kk * e).astype(BF16)
        for h in range(HG_HEADS):
            hs = slice(h * hd, (h + 1) * hd)
            sc_ref[h] += mask_ref[l] * _dot_nt(qt[:, hs], kt[:, hs])

    gcum = expo(levels)
    gs_ref[...] = gcum
    qg = (q * jnp.exp(gcum)).astype(BF16)
    kend = (kk * jnp.exp(expo(levels + 1))).astype(BF16)
    v = v_ref[...]
    vb = v.astype(BF16)
    r = q.shape[0]
    gw = gw_ref[...]
    for h in range(HG_HEADS):
        hs = slice(h * hd, (h + 1) * hd)
        st = st_ref[h]
        o = _dot(sc_ref[h].astype(BF16), vb[:, hs])
        oi = _dot_nt(qg[:, hs], st.astype(BF16))
        for k in range(nb):
            o = o + bm_ref[k] * oi[:, k * hd:(k + 1) * hd]
        vcat = jnp.concatenate([(v[:, hs] * bm_ref[k]).astype(BF16) for k in range(nb)], axis=1)
        upd = _dot_tn(vcat, kend[:, hs])
        dec = jnp.concatenate(
            [jnp.broadcast_to(jnp.exp(gs_ref[r - nb + k:r - nb + k + 1, hs]), (hd, hd)) for k in range(nb)],
            axis=0)
        st_ref[h] = dec * st + upd
        o = o * lax.rsqrt(jnp.mean(o * o, axis=-1, keepdims=True) + EPS) * gw
        o_ref[:, hs] = (o * _silu(g_ref[:, hs])).astype(BF16)


def _hgrn_call(z, lbp, gnorm_w, s0t, nb, n_chunks):
    m = z.shape[0]
    d_b = HG_HEADS * HG_DIM
    r = HG_CHUNK_T * nb
    expo, mask, bmask = _hgrn_consts(HG_CHUNK_T, nb)
    fix2 = lambda i: (0, 0)
    fix3 = lambda i: (0, 0, 0)
    col = lambda c: pl.BlockSpec((r, d_b), functools.partial(lambda i, c: (i, c), c=c))
    vec = pl.BlockSpec((1, d_b), fix2)
    return pl.pallas_call(
        functools.partial(_hgrn_kernel, nb=nb),
        out_shape=(jax.ShapeDtypeStruct((m, d_b), BF16),
                   jax.ShapeDtypeStruct(s0t.shape, F32)),
        grid=(n_chunks,),
        in_specs=[col(1), col(2), col(3), col(4), vec, vec, vec,
                  pl.BlockSpec((1, HG_DIM), fix2),
                  pl.BlockSpec(expo.shape, fix3), pl.BlockSpec(mask.shape, fix3),
                  pl.BlockSpec(bmask.shape, fix3), pl.BlockSpec(s0t.shape, fix3)],
        out_specs=(pl.BlockSpec((r, d_b), lambda i: (i, 0)), pl.BlockSpec(s0t.shape, fix3)),
        scratch_shapes=[pltpu.VMEM((HG_HEADS, r, r), F32), pltpu.VMEM((r, d_b), F32)],
        compiler_params=_cp("arbitrary"),
        name="hgrn",
    )(z, z, z, z, lbp["log_lb"], lbp["log1m_lb"], lbp["one_m_lb"], gnorm_w.reshape(1, -1),
      jnp.asarray(expo, BF16), jnp.asarray(mask, F32), jnp.asarray(bmask, F32), s0t)


def _hgrn_dec_kernel(*refs, aliased):
    q_ref, fz_ref, v_ref, g_ref, llb_ref, l1m_ref, om_ref, gw_ref, fsel_ref, qsel_ref, s_ref = refs[:11]
    o_ref, so_ref = refs[13:] if aliased else refs[11:]
    hd = HG_DIM
    sb = q_ref.shape[0]
    fz = fz_ref[...]
    f = jnp.exp(_log_f(fz, llb_ref[...], l1m_ref[...]))
    kk = om_ref[...] * jax.nn.sigmoid(-fz)
    v = v_ref[...]
    gw = gw_ref[...]
    parts = [p.astype(F32) for p in _split3(f)]
    parts += [kk.astype(BF16).astype(F32), q_ref[...].astype(BF16).astype(F32)]
    n_f = 3 * sb
    zpad = jnp.zeros((hd - len(parts) * sb, hd), F32)
    wide = sb * hd
    own = (lax.broadcasted_iota(jnp.int32, (sb, wide), 0)
           == lax.shift_right_logical(lax.broadcasted_iota(jnp.int32, (sb, wide), 1), 7))
    outs = []
    for h in range(HG_HEADS):
        hs = slice(h * hd, (h + 1) * hd)
        pt = jnp.concatenate([p[:, hs] for p in parts] + [zpad], axis=0).T.astype(BF16)
        vt = jnp.where(own, jnp.concatenate([v[:, hs]] * sb, axis=1), 0.0)
        vall = jnp.concatenate([jnp.zeros((n_f, wide), F32), vt,
                                jnp.zeros((hd - n_f - sb, wide), F32)], axis=0).astype(BF16)
        prod = _dot(pt, jnp.concatenate([fsel_ref[...], vall, qsel_ref[...]], axis=1))
        o_rows = []
        for j in range(sb):
            js = slice(j * hd, (j + 1) * hd)
            s_new = prod[:, js] * s_ref[0, j, h] + prod[:, wide:2 * wide][:, js]
            so_ref[0, j, h] = s_new
            o_rows.append(jnp.sum(prod[:, 2 * wide:][:, js] * s_new, axis=0, keepdims=True))
        o = jnp.concatenate(o_rows, axis=0)
        outs.append(o * lax.rsqrt(jnp.mean(o * o, axis=-1, keepdims=True) + EPS) * gw)
    o_all = jnp.concatenate(outs, axis=1)
    o_ref[...] = (o_all * _silu(g_ref[...])).astype(BF16)


def _hgrn_dec_call(z, row0, n_rows, lbp, gnorm_w, s_all, j, y_prev, s_prev, sb=SUBLANES):
    d_b = HG_HEADS * HG_DIM
    rb0 = row0 // sb
    fix2 = lambda i: (0, 0)
    col = lambda c: pl.BlockSpec((sb, d_b), functools.partial(lambda i, c: (rb0 + i, c), c=c))
    vec = pl.BlockSpec((1, d_b), fix2)
    sblk = pl.BlockSpec((1, sb, HG_HEADS, HG_DIM, HG_DIM), lambda i: (j, i, 0, 0, 0))
    fsel = np.zeros((HG_DIM, sb * HG_DIM), np.float32)
    qsel = np.zeros((HG_DIM, sb * HG_DIM), np.float32)
    for jj in range(sb):
        fsel[jj:3 * sb:sb, jj * HG_DIM:(jj + 1) * HG_DIM] = 1.0
        qsel[4 * sb + jj, jj * HG_DIM:(jj + 1) * HG_DIM] = 1.0
    sel = pl.BlockSpec(fsel.shape, fix2)
    ins = [z, z, z, z, lbp["log_lb"], lbp["log1m_lb"], lbp["one_m_lb"], gnorm_w.reshape(1, -1),
           jnp.asarray(fsel, BF16), jnp.asarray(qsel, BF16), s_all, y_prev]
    specs = [col(1), col(2), col(3), col(4), vec, vec, vec, pl.BlockSpec((1, HG_DIM), fix2),
             sel, sel, sblk, pl.BlockSpec(memory_space=pl.ANY)]
    aliases = {11: 0}
    if s_prev is not None:
        ins.append(s_prev)
        specs.append(pl.BlockSpec(memory_space=pl.ANY))
        aliases[12] = 1
    else:
        ins.append(jnp.zeros((SUBLANES, LANES), F32))
        specs.append(pl.BlockSpec((SUBLANES, LANES), fix2))
    return pl.pallas_call(
        functools.partial(_hgrn_dec_kernel, aliased=True),
        out_shape=(jax.ShapeDtypeStruct(y_prev.shape, BF16), jax.ShapeDtypeStruct(s_all.shape, F32)),
        grid=(n_rows // sb,),
        in_specs=specs,
        out_specs=(pl.BlockSpec((sb, d_b), lambda i: (rb0 + i, 0)), sblk),
        input_output_aliases=aliases,
        compiler_params=_cp("parallel"),
        name="hgrn_dec",
    )(*ins)


def _lru_coeffs(xc, wg_ref, bga_ref, bgx_ref, lam_ref, a_ref, b_ref):
    sw = LRU_SUPER
    for j in range(xc.shape[1] // sw):
        cs = slice(j * sw, (j + 1) * sw)
        xj = xc[:, cs]
        gj = _dot(xj.astype(BF16), wg_ref[j])
        rg = jax.nn.sigmoid(gj[:, :sw] + bga_ref[:, cs])
        ig = jax.nn.sigmoid(gj[:, sw:] + bgx_ref[:, cs])
        log_a = LRU_C * rg * _log_sigmoid(lam_ref[:, cs])
        a = jnp.exp(log_a)
        a_ref[:, cs] = a
        b_ref[:, cs] = jnp.sqrt(-jnp.tanh(log_a) * (a * a + 1.0)) * ig * xj


def _lru_kernel(gb_ref, xr_ref, cw_ref, cb_ref, wg_ref, bga_ref, bgx_ref, lam_ref, c0_ref, h0_ref,
                o_ref, hl_ref, xs_ref, a_ref, b_ref, *, nb):
    r = xr_ref.shape[0]
    hdr = xs_ref.shape[0] - r
    kw = cw_ref.shape[0]

    @pl.when(pl.program_id(0) == 0)
    def _():
        xs_ref[0:hdr, :] = c0_ref[...]
        hl_ref[...] = h0_ref[...]

    xs_ref[hdr:hdr + r, :] = xr_ref[...]
    xc = cb_ref[...] + xs_ref[hdr - (kw - 1) * nb:hdr - (kw - 1) * nb + r, :] * cw_ref[0:1, :]
    for j in range(1, kw):
        xc = xc + xs_ref[hdr - (kw - 1 - j) * nb:hdr - (kw - 1 - j) * nb + r, :] * cw_ref[j:j + 1, :]
    xs_ref[0:hdr, :] = xs_ref[r:r + hdr, :]

    _lru_coeffs(xc, wg_ref, bga_ref, bgx_ref, lam_ref, a_ref, b_ref)

    lower = lax.broadcasted_iota(jnp.int32, (SUBLANES, xc.shape[1]), 0) < nb

    def pair(k, h):
        rows = pl.ds(pl.multiple_of(k * SUBLANES, SUBLANES), SUBLANES)
        a8 = a_ref[rows, :]
        b8 = b_ref[rows, :]
        h_a = a8 * h + b8
        h_b = a8 * pltpu.roll(h_a, nb, 0) + b8
        b_ref[rows, :] = jnp.where(lower, h_a, h_b)
        return pltpu.roll(h_b, nb, 0)

    hl_ref[...] = lax.fori_loop(0, r // SUBLANES, pair, hl_ref[...], unroll=2)
    o_ref[...] = (_gelu(gb_ref[...]) * b_ref[...]).astype(BF16)


def _lru_call(z, prm, conv0, h0, nb, n_chunks):
    m = z.shape[0]
    d_rnn = prm["lam"].shape[1]
    r = nb * LRU_CHUNK_T
    hdr = conv0.shape[0]
    fix2 = lambda i: (0, 0)
    fix3 = lambda i: (0, 0, 0)
    vec = pl.BlockSpec((1, d_rnn), fix2)
    return pl.pallas_call(
        functools.partial(_lru_kernel, nb=nb),
        out_shape=(jax.ShapeDtypeStruct((m, d_rnn), BF16),
                   jax.ShapeDtypeStruct((SUBLANES, d_rnn), F32)),
        grid=(n_chunks,),
        in_specs=[pl.BlockSpec((r, d_rnn), lambda i: (i, 0)),
                  pl.BlockSpec((r, d_rnn), lambda i: (i, 1)),
                  pl.BlockSpec(prm["cw"].shape, fix2), vec,
                  pl.BlockSpec(prm["wg"].shape, fix3), vec, vec, vec,
                  pl.BlockSpec((hdr, d_rnn), fix2), pl.BlockSpec((SUBLANES, d_rnn), fix2)],
        out_specs=(pl.BlockSpec((r, d_rnn), lambda i: (i, 0)),
                   pl.BlockSpec((SUBLANES, d_rnn), fix2)),
        scratch_shapes=[pltpu.VMEM((hdr + r, d_rnn), F32), pltpu.VMEM((r, d_rnn), F32),
                        pltpu.VMEM((r, d_rnn), F32)],
        compiler_params=_cp("arbitrary"),
        name="lru",
    )(z, z, prm["cw"], prm["cb"], prm["wg"], prm["bga"], prm["bgx"], prm["lam"], conv0, h0)


def _lru_dec_kernel(gb_ref, xr_ref, buf_ref, cw_ref, cb_ref, wg_ref, bga_ref, bgx_ref, lam_ref,
                    h0_ref, yp_ref, o_ref, h_ref, a_ref, b_ref):
    kw = cw_ref.shape[0]
    xc = cb_ref[...] + xr_ref[...] * cw_ref[kw - 1:kw, :]
    for j in range(kw - 1):
        xc = xc + buf_ref[j] * cw_ref[j:j + 1, :]
    _lru_coeffs(xc, wg_ref, bga_ref, bgx_ref, lam_ref, a_ref, b_ref)
    h = a_ref[...] * h0_ref[...] + b_ref[...]
    h_ref[...] = h
    o_ref[...] = (_gelu(gb_ref[...]) * h).astype(BF16)


def _lru_dec_call(z, row0, prm, buf, h0, y_prev):
    n = h0.shape[0]
    d_rnn = prm["lam"].shape[1]
    rb0 = row0 // n
    fix2 = lambda i: (0, 0)
    fix3 = lambda i: (0, 0, 0)
    vec = pl.BlockSpec((1, d_rnn), fix2)
    full = pl.BlockSpec((n, d_rnn), fix2)
    return pl.pallas_call(
        _lru_dec_kernel,
        out_shape=(jax.ShapeDtypeStruct(y_prev.shape, BF16), jax.ShapeDtypeStruct((n, d_rnn), F32)),
        grid=(1,),
        in_specs=[pl.BlockSpec((n, d_rnn), lambda i: (rb0, 0)),
                  pl.BlockSpec((n, d_rnn), lambda i: (rb0, 1)),
                  pl.BlockSpec(buf.shape, fix3),
                  pl.BlockSpec(prm["cw"].shape, fix2), vec,
                  pl.BlockSpec(prm["wg"].shape, fix3), vec, vec, vec, full,
                  pl.BlockSpec(memory_space=pl.ANY)],
        out_specs=(pl.BlockSpec((n, d_rnn), lambda i: (rb0, 0)), full),
        scratch_shapes=[pltpu.VMEM((n, d_rnn), F32), pltpu.VMEM((n, d_rnn), F32)],
        input_output_aliases={10: 0},
        compiler_params=_cp("arbitrary"),
        name="lru_dec",
    )(z, z, buf, prm["cw"], prm["cb"], prm["wg"], prm["bga"], prm["bgx"], prm["lam"], h0, y_prev)


def _lru_params(conv_w, conv_b, w_ga, b_ga, w_gx, b_gx, lam):
    nblk, bs = w_ga.shape[0], w_ga.shape[1]
    per = LRU_SUPER // bs
    ns = nblk // per
    eye = jnp.eye(per, dtype=F32)

    def sup(w):
        return jnp.einsum("ab,sajk->sajbk", eye, w.reshape(ns, per, bs, bs)).reshape(ns, per * bs, per * bs)

    return dict(cw=conv_w, cb=conv_b.reshape(1, -1),
                wg=jnp.concatenate([sup(w_ga), sup(w_gx)], axis=-1).astype(BF16),
                bga=b_ga.reshape(1, -1), bgx=b_gx.reshape(1, -1), lam=lam.reshape(1, -1))


def kernel(x_prompt, x_sample, state_s5_re, state_s5_im, state_hgrn, state_lru, state_conv,
           c_prompt, c_sample, norm_w, final_norm_w, w_ada, b_ada, w_ffn_gu, w_ffn_d,
           w_in_ab, s5_lam_re, s5_lam_im, s5_b_re, s5_b_im, s5_c_re, s5_c_im, s5_d, s5_log_step,
           s5_w_glu, s5_b_glu, hg_lb_logits, hg_norm_w, w_out_ab, w_in_c, conv_w, conv_b,
           w_gate_a, b_gate_a, w_gate_x, b_gate_x, lru_lambda, w_out_c):
    bsz, seq, d = x_prompt.shape
    nsm = x_sample.shape[0]
    depth = w_ada.shape[0]
    n_ab, n_c = w_in_ab.shape[0], w_in_c.shape[0]
    g_a, n_a = s5_lam_re.shape[1], s5_lam_re.shape[2]
    n_state = g_a * n_a
    d_rnn = lru_lambda.shape[1]
    kw = conv_w.shape[1]
    mp_rows = seq * bsz
    assert 2 * bsz == SUBLANES and mp_rows % nsm == 0 and (mp_rows + nsm) % ROW_TILE == 0

    s5p = [_s5_params(s5_lam_re[j], s5_lam_im[j], s5_b_re[j], s5_b_im[j], s5_c_re[j], s5_c_im[j],
                      s5_d[j], s5_log_step[j], s5_w_glu[j], s5_b_glu[j]) for j in range(n_ab)]
    lb_all = jnp.cumsum(jax.nn.softmax(hg_lb_logits.astype(F32), axis=0), axis=0)
    lb_all = lb_all - lb_all[0:1]
    lbp = [dict(log_lb=jnp.log(lb_all[j]).reshape(1, -1),
                log1m_lb=jnp.log1p(-lb_all[j]).reshape(1, -1),
                one_m_lb=(1.0 - lb_all[j]).reshape(1, -1)) for j in range(n_ab)]
    lrup = [_lru_params(conv_w[j], conv_b[j], w_gate_a[j], b_gate_a[j], w_gate_x[j], b_gate_x[j],
                        lru_lambda[j]) for j in range(n_c)]
    w_out_ab_bf = w_out_ab.astype(BF16)
    w_out_c_bf = w_out_c.astype(BF16)

    pad = (-bsz) % SUBLANES
    c_all = jnp.concatenate([c_prompt, jnp.zeros((pad, d), F32), c_sample], axis=0)
    mods = _ada_call(c_all, w_ada, b_ada)
    mods_p = jnp.concatenate([mods[:, :bsz]] * (SUBLANES // bsz), axis=1)
    mods_s = mods[:, bsz + pad:]

    x = jnp.concatenate([jnp.transpose(x_prompt, (1, 0, 2)).reshape(mp_rows, d),
                         x_sample.reshape(nsm, d)], axis=0)
    zeros_state = jnp.zeros((SUBLANES, n_state), F32)
    s5r_p, s5i_p, hg_p, lru_p, conv_p = [], [], [], [], []
    s5r_s, s5i_s, lru_s, conv_s = [], [], [], []
    hg_s = None
    for l in range(depth):
        j = l // 2
        mvp = [mods_p[l, :, k * d:(k + 1) * d] for k in range(N_MOD)]
        mvs = [mods_s[l, :, k * d:(k + 1) * d] for k in range(N_MOD)]
        nw = norm_w[l]
        x = _ffn_call(x, nw[0:1], mvp[0:3], mvs[0:3], w_ffn_gu, w_ffn_d, l, 0, ROW_TILE, 256)
        if l % 2 == 0:
            z = _inproj_call(x, nw[1:2], mvp[3:5], mvs[3:5], w_in_ab, j, ROW_TILE, 1024)
            ya, hr, hi = _s5_call(z, s5p[j], zeros_state, zeros_state, bsz, S5_CHUNK_T,
                                  0, seq // S5_CHUNK_T)
            ya, hrs, his = _s5_call(z, s5p[j], state_s5_re[j].reshape(nsm, n_state),
                                    state_s5_im[j].reshape(nsm, n_state), nsm, 1,
                                    mp_rows // nsm, 1, y_prev=ya)
            yb, hg = _hgrn_call(z, lbp[j], hg_norm_w[j],
                                jnp.zeros((HG_HEADS, bsz * HG_DIM, HG_DIM), F32), bsz, seq // HG_CHUNK_T)
            yb, hg_s = _hgrn_dec_call(z, mp_rows, nsm, lbp[j], hg_norm_w[j], state_hgrn, j, yb, hg_s)
            s5r_p.append(hr[:bsz].reshape(bsz, g_a, n_a))
            s5i_p.append(hi[:bsz].reshape(bsz, g_a, n_a))
            s5r_s.append(hrs.reshape(nsm, g_a, n_a))
            s5i_s.append(his.reshape(nsm, g_a, n_a))
            hg_p.append(jnp.transpose(hg.reshape(HG_HEADS, bsz, HG_DIM, HG_DIM), (1, 0, 3, 2)))
            x = _outproj_call(x, mvp[5], mvs[5], [ya, yb], w_out_ab_bf[j], OUT_ROW_TILE)
        else:
            z = _inproj_call(x, nw[1:2], mvp[3:5], mvs[3:5], w_in_c, j, ROW_TILE, 1024)
            y, hl = _lru_call(z, lrup[j], jnp.zeros((4 * bsz, d_rnn), F32),
                              jnp.zeros((SUBLANES, d_rnn), F32), bsz, seq // LRU_CHUNK_T)
            y, hls = _lru_dec_call(z, mp_rows, lrup[j], jnp.transpose(state_conv[j], (1, 0, 2)),
                                   state_lru[j], y)
            lru_p.append(hl[:bsz])
            lru_s.append(hls)
            tail = z[mp_rows - (kw - 1) * bsz:mp_rows, d_rnn:]
            conv_p.append(jnp.transpose(tail.reshape(kw - 1, bsz, d_rnn), (1, 0, 2)))
            conv_s.append(jnp.concatenate([state_conv[j][:, 1:], z[mp_rows:, None, d_rnn:]], axis=1))
            x = _outproj_call(x, mvp[5], mvs[5], [y], w_out_c_bf[j], OUT_ROW_TILE)
        x = _ffn_call(x, nw[2:3], mvp[6:9], mvs[6:9], w_ffn_gu, w_ffn_d, l, 1, ROW_TILE, 256)
    yp_rows, y_sample = _final_norm_call(x, final_norm_w.reshape(1, -1), mp_rows, nsm, FINAL_ROW_TILE)
    y_prompt = jnp.transpose(yp_rows.reshape(seq, bsz, d), (1, 0, 2))
    return (y_prompt, y_sample.reshape(nsm, 1, d), jnp.stack(s5r_p), jnp.stack(s5i_p), jnp.stack(hg_p),
            jnp.stack(lru_p), jnp.stack(conv_p),
            jnp.stack(s5r_s), jnp.stack(s5i_s), hg_s, jnp.stack(lru_s), jnp.stack(conv_s))
```

```python
import functools

import numpy as np
import jax
import jax.numpy as jnp
from jax import lax
from jax.experimental import pallas as pl
from jax.experimental.pallas import tpu as pltpu

F32 = jnp.float32
BF16 = jnp.bfloat16
EPS = 1e-6
LRU_C = 8.0

VMEM_LIMIT_BYTES = 60 * 1024 * 1024
SUBLANES = 8
LANES = 128
BF16_ROWS = 16

S5_COLS = 8
S5_CHUNK_T = 128
HG_HEADS = 8
HG_DIM = 128
HG_CHUNK_T = 64
LRU_CHUNK_T = 64
LRU_SUPER = 640
N_MOD = 9
ROW_TILE = 1040
OUT_ROW_TILE = 640
FINAL_TILE_T = 256
MXU_COLS = 256


def _cp(*sem):
    return pltpu.CompilerParams(dimension_semantics=sem, vmem_limit_bytes=VMEM_LIMIT_BYTES)


def _dot(a, b):
    return jnp.dot(a, b, preferred_element_type=F32)


def _dot_nt(a, b):
    return lax.dot_general(a, b, (((1,), (1,)), ((), ())), preferred_element_type=F32)


def _dot_tn(a, b):
    return lax.dot_general(a, b, (((0,), (0,)), ((), ())), preferred_element_type=F32)


def _silu(x):
    return x * jax.nn.sigmoid(x)


def _gelu(x):
    return jax.nn.gelu(x, approximate=True)


def _log_sigmoid(x):
    return jnp.minimum(x, 0.0) - jnp.log1p(jnp.exp(-jnp.abs(x)))


def _block_diag_mask(nblk, rows, cols):
    r = np.arange(nblk * rows)[:, None] // rows
    c = np.arange(nblk * cols)[None, :] // cols
    return jnp.asarray(r == c, F32)


def _split3(x):
    hi = x.astype(BF16)
    r1 = x - hi.astype(F32)
    mid = r1.astype(BF16)
    lo = (r1 - mid.astype(F32)).astype(BF16)
    return hi, mid, lo


def _fma_rows(y, mul, add):
    rm = mul.shape[0]
    r, d = y.shape
    if rm == r:
        return y * mul + add
    y3 = y.reshape(r // rm, rm, d)
    return (y3 * mul[None] + add[None]).reshape(r, d)


def _mul_rows(y, mul):
    rm = mul.shape[0]
    r, d = y.shape
    if rm == r:
        return y * mul
    return (y.reshape(r // rm, rm, d) * mul[None]).reshape(r, d)


def _norm_mod_rows(x_ref, nw_ref, sh_ref, sc_ref, h_ref, row0, nrows):
    slab = BF16_ROWS
    per_row = sh_ref.shape[0] > SUBLANES
    nw = nw_ref[...]

    def body(s, carry):
        r0 = pl.multiple_of(row0 + s * slab, slab)
        x = x_ref[pl.ds(r0, slab), :]
        ms = jnp.mean(x * x, axis=-1, keepdims=True)
        y = x * lax.rsqrt(ms + EPS) * nw
        if per_row:
            m0 = pl.multiple_of(s * slab, slab)
            h = y * (1.0 + sc_ref[pl.ds(m0, slab), :]) + sh_ref[pl.ds(m0, slab), :]
        else:
            h = _fma_rows(y, 1.0 + sc_ref[...], sh_ref[...])
        h_ref[pl.ds(r0, slab), :] = h.astype(BF16)
        return carry

    lax.fori_loop(0, nrows // slab, body, 0, unroll=8)


def _residual_rows(o_ref, x_ref, acc, gt_ref, gts_ref, scale):
    tm = o_ref.shape[0]
    ns = gts_ref.shape[0]
    np_ = tm - ns
    is_last = pl.program_id(0) == pl.num_programs(0) - 1
    gp = scale * gt_ref[...]
    o_ref[0:np_, :] = x_ref[0:np_, :] + _mul_rows(acc(0, np_), gp)

    @pl.when(jnp.logical_not(is_last))
    def _():
        o_ref[np_:tm, :] = x_ref[np_:tm, :] + _mul_rows(acc(np_, tm), gp)

    @pl.when(is_last)
    def _():
        o_ref[np_:tm, :] = x_ref[np_:tm, :] + acc(np_, tm) * (scale * gts_ref[...])


def _ada_kernel(c_ref, w_ref, b_ref, o_ref):
    sc = _silu(c_ref[...]).astype(BF16)
    o_ref[0] = _dot(sc, w_ref[0].astype(BF16)) + b_ref[0]


def _ada_call(c_all, w_ada, b_ada, tn=1024):
    depth, d, n = w_ada.shape
    r = c_all.shape[0]
    return pl.pallas_call(
        _ada_kernel,
        out_shape=jax.ShapeDtypeStruct((depth, r, n), F32),
        grid=(depth, n // tn),
        in_specs=[pl.BlockSpec((r, d), lambda l, j: (0, 0)),
                  pl.BlockSpec((1, d, tn), lambda l, j: (l, 0, j)),
                  pl.BlockSpec((1, 1, tn), lambda l, j: (l, 0, j))],
        out_specs=pl.BlockSpec((1, r, tn), lambda l, j: (l, 0, j)),
        compiler_params=_cp("parallel", "parallel"),
        name="ada",
    )(c_all, w_ada, b_ada.reshape(depth, 1, n))


def _ffn_kernel(x_ref, nw_ref, sh_ref, sc_ref, gt_ref, shs_ref, scs_ref, gts_ref,
                wg_ref, wu_ref, wd_ref, o_ref, h_ref):
    i, f = pl.program_id(0), pl.program_id(1)
    nw_ref, sh_ref, sc_ref, gt_ref, shs_ref, scs_ref, gts_ref = (
        r.at[0] for r in (nw_ref, sh_ref, sc_ref, gt_ref, shs_ref, scs_ref, gts_ref))
    tm = x_ref.shape[0]
    ns = shs_ref.shape[0]
    is_last = i == pl.num_programs(0) - 1

    @pl.when(f == 0)
    def _():
        _norm_mod_rows(x_ref, nw_ref, sh_ref, sc_ref, h_ref, 0, tm)
        o_ref[...] = jnp.zeros_like(o_ref)

    @pl.when(jnp.logical_and(f == 0, is_last))
    def _():
        _norm_mod_rows(x_ref, nw_ref, shs_ref, scs_ref, h_ref, tm - ns, ns)

    h = h_ref[...]
    g = _dot(h, wg_ref[0, 0].astype(BF16))
    u = _dot(h, wu_ref[0, 0].astype(BF16))
    a = (_silu(g) * u).astype(BF16)
    o_ref[...] += _dot(a, wd_ref[0, 0].astype(BF16))

    @pl.when(f == pl.num_programs(1) - 1)
    def _():
        _residual_rows(o_ref, x_ref, lambda a0, a1: o_ref[a0:a1, :], gt_ref, gts_ref, 0.5)


def _mod_specs(mods, ns, d, l, ks):
    def pat(k):
        return pl.BlockSpec((1, SUBLANES, d), lambda *g: (l, ns // SUBLANES, k))

    def smp(k):
        return pl.BlockSpec((1, ns, d), lambda *g: (l, 0, k))

    return [pat(k) for k in ks] + [smp(k) for k in ks], [mods] * (2 * len(ks))


def _ffn_call(x, norm_w, mods, ns, w_gu, w_d, l, s, sub, tm, tf):
    m, d = x.shape
    ff = w_d.shape[2]
    nf = ff // tf
    row = lambda i, f: (i, 0)
    mspecs, margs = _mod_specs(mods, ns, d, l, (3 * sub, 3 * sub + 1, 3 * sub + 2))
    return pl.pallas_call(
        _ffn_kernel,
        out_shape=jax.ShapeDtypeStruct((m, d), F32),
        grid=(m // tm, nf),
        in_specs=[pl.BlockSpec((tm, d), row), pl.BlockSpec((1, 1, d), lambda i, f: (3 * l + sub, 0, 0))]
                 + mspecs +
                 [pl.BlockSpec((1, 1, d, tf), lambda i, f: (l, s, 0, f)),
                  pl.BlockSpec((1, 1, d, tf), lambda i, f: (l, s, 0, f + nf)),
                  pl.BlockSpec((1, 1, tf, d), lambda i, f: (l, s, f, 0))],
        out_specs=pl.BlockSpec((tm, d), row),
        scratch_shapes=[pltpu.VMEM((tm, d), BF16)],
        compiler_params=_cp("parallel", "arbitrary"),
        name="ffn",
    )(x, norm_w, *margs, w_gu, w_gu, w_d)


def _inproj_kernel(x_ref, nw_ref, sh_ref, sc_ref, shs_ref, scs_ref, w_ref, o_ref, h_ref):
    i, k = pl.program_id(0), pl.program_id(1)
    nw_ref, sh_ref, sc_ref, shs_ref, scs_ref = (
        r.at[0] for r in (nw_ref, sh_ref, sc_ref, shs_ref, scs_ref))
    tm = x_ref.shape[0]
    ns = shs_ref.shape[0]

    @pl.when(k == 0)
    def _():
        _norm_mod_rows(x_ref, nw_ref, sh_ref, sc_ref, h_ref, 0, tm)

    @pl.when(jnp.logical_and(k == 0, i == pl.num_programs(0) - 1))
    def _():
        _norm_mod_rows(x_ref, nw_ref, shs_ref, scs_ref, h_ref, tm - ns, ns)

    h = h_ref[...]
    for c in range(0, o_ref.shape[1], MXU_COLS):
        o_ref[:, c:c + MXU_COLS] = _dot(h, w_ref[0, :, c:c + MXU_COLS].astype(BF16))


def _inproj_call(x, norm_w, mods, ns, w, l, j, tm, tn):
    m, d = x.shape
    n = w.shape[2]
    mspecs, margs = _mod_specs(mods, ns, d, l, (3, 4))
    return pl.pallas_call(
        _inproj_kernel,
        out_shape=jax.ShapeDtypeStruct((m, n), F32),
        grid=(m // tm, n // tn),
        in_specs=[pl.BlockSpec((tm, d), lambda i, k: (i, 0)),
                  pl.BlockSpec((1, 1, d), lambda i, k: (3 * l + 1, 0, 0))]
                 + mspecs + [pl.BlockSpec((1, d, tn), lambda i, k: (j, 0, k))],
        out_specs=pl.BlockSpec((tm, tn), lambda i, k: (i, k)),
        scratch_shapes=[pltpu.VMEM((tm, d), BF16)],
        compiler_params=_cp("parallel", "arbitrary"),
        name="inproj",
    )(x, norm_w, *margs, w)


def _outproj_kernel(*refs, n_in):
    x_ref, gt_ref, gts_ref = refs[0], refs[1].at[0], refs[2].at[0]
    a_refs = refs[3:3 + n_in]
    w_refs = refs[3 + n_in:3 + 2 * n_in]
    o_ref = refs[3 + 2 * n_in]
    acc = _dot(a_refs[0][...], w_refs[0][...])
    for a_ref, w_ref in zip(a_refs[1:], w_refs[1:]):
        acc = acc + _dot(a_ref[...], w_ref[...])
    _residual_rows(o_ref, x_ref, lambda a0, a1: acc[a0:a1, :], gt_ref, gts_ref, 1.0)


def _outproj_call(x, mods, ns, l, acts, w_bf, tm):
    m, d = x.shape
    n_in = len(acts)
    ka = acts[0].shape[1]
    row = lambda i: (i, 0)
    mspecs, margs = _mod_specs(mods, ns, d, l, (5,))
    in_specs = [pl.BlockSpec((tm, d), row)] + mspecs
    in_specs += [pl.BlockSpec((tm, ka), row) for _ in acts]
    in_specs += [pl.BlockSpec((ka, d), functools.partial(lambda i, p: (p, 0), p=p)) for p in range(n_in)]
    return pl.pallas_call(
        functools.partial(_outproj_kernel, n_in=n_in),
        out_shape=jax.ShapeDtypeStruct((m, d), F32),
        grid=(m // tm,),
        in_specs=in_specs,
        out_specs=pl.BlockSpec((tm, d), row),
        compiler_params=_cp("parallel"),
        name="outproj",
    )(x, *margs, *acts, *([w_bf] * n_in))


def _pack_kernel(xp_ref, xs_ref, o_ref, slab_ref, *, n_tiles):
    i = pl.program_id(0)
    nb, tt, d = xp_ref.shape

    @pl.when(i < n_tiles)
    def _():
        for c in range(d // LANES):
            cs = slice(c * LANES, (c + 1) * LANES)
            for b in range(nb):
                slab_ref[c, pl.ds(b, tt, stride=nb), :] = xp_ref[b, :, cs]
            o_ref[:, cs] = slab_ref[c]

    @pl.when(i == n_tiles)
    def _():
        o_ref[0:xs_ref.shape[0], :] = xs_ref[...]


def _pack_call(x_prompt, xs, tt):
    nb, seq, d = x_prompt.shape
    ns = xs.shape[0]
    n_tiles = seq // tt
    return pl.pallas_call(
        functools.partial(_pack_kernel, n_tiles=n_tiles),
        out_shape=jax.ShapeDtypeStruct((seq * nb + ns, d), F32),
        grid=(n_tiles + 1,),
        in_specs=[pl.BlockSpec((nb, tt, d), lambda i: (0, jnp.minimum(i, n_tiles - 1), 0)),
                  pl.BlockSpec((ns, d), lambda i: (0, 0))],
        out_specs=pl.BlockSpec((tt * nb, d), lambda i: (i, 0)),
        scratch_shapes=[pltpu.VMEM((d // LANES, tt * nb, LANES), F32)],
        compiler_params=_cp("arbitrary"),
        name="pack",
    )(x_prompt, xs)


def _final_norm_kernel(x_ref, w_ref, yp_ref, ys_ref, slab_ref, *, n_tiles):
    i = pl.program_id(0)
    nb, tt, d = yp_ref.shape
    w = w_ref[...]

    def inv_rms(x):
        return lax.rsqrt(jnp.mean(x * x, axis=-1, keepdims=True) + EPS)

    @pl.when(i < n_tiles)
    def _():
        inv = inv_rms(x_ref[...])
        for c in range(d // LANES):
            cs = slice(c * LANES, (c + 1) * LANES)
            slab_ref[c] = x_ref[:, cs] * inv * w[:, cs]
            for b in range(nb):
                yp_ref[b, :, cs] = slab_ref[c, pl.ds(b, tt, stride=nb), :]

    @pl.when(i == n_tiles)
    def _():
        x = x_ref[0:ys_ref.shape[0], :]
        ys_ref[...] = x * inv_rms(x) * w


def _final_norm_call(x, w, nb, seq, ns, tt):
    d = x.shape[1]
    n_tiles = seq // tt
    return pl.pallas_call(
        functools.partial(_final_norm_kernel, n_tiles=n_tiles),
        out_shape=(jax.ShapeDtypeStruct((nb, seq, d), F32), jax.ShapeDtypeStruct((ns, d), F32)),
        grid=(n_tiles + 1,),
        in_specs=[pl.BlockSpec((tt * nb, d), lambda i: (i, 0)),
                  pl.BlockSpec((1, d), lambda i: (0, 0))],
        out_specs=(pl.BlockSpec((nb, tt, d), lambda i: (0, jnp.minimum(i, n_tiles - 1), 0)),
                   pl.BlockSpec((ns, d), lambda i: (0, 0))),
        scratch_shapes=[pltpu.VMEM((d // LANES, tt * nb, LANES), F32)],
        compiler_params=_cp("arbitrary"),
        name="final_norm",
    )(x, w)


def _s5_kernel(*refs, nb, tc, aliased):
    (u_ref, wb_ref, ar_ref, ai_ref, wcr_ref, wci_ref, d_ref, wglu_ref, bglu_ref,
     h0r_ref, h0i_ref) = refs[:11]
    o_ref, hr_ref, hi_ref, xr_ref, xi_ref = refs[12:] if aliased else refs[11:]

    @pl.when(pl.program_id(0) == 0)
    def _():
        hr_ref[...] = h0r_ref[...]
        hi_ref[...] = h0i_ref[...]

    u = u_ref[...]
    ub = u.astype(BF16)
    n_state = xr_ref.shape[1]
    cw = n_state // S5_COLS
    for c in range(S5_COLS):
        xc = _dot(ub[:, c * LANES:(c + 1) * LANES], wb_ref[c])
        xr_ref[:, c * cw:(c + 1) * cw] = xc[:, :cw]
        xi_ref[:, c * cw:(c + 1) * cw] = xc[:, cw:]

    lane_group = 1024
    for lo_ in range(0, n_state, lane_group):
        ls = pl.ds(lo_, lane_group)
        ar = ar_ref[:, ls]
        ai = ai_ref[:, ls]
        if tc == 1:
            hr, hi = hr_ref[:, ls], hi_ref[:, ls]
            nr = ar * hr - ai * hi + xr_ref[:, ls]
            ni = ar * hi + ai * hr + xi_ref[:, ls]
            xr_ref[:, ls] = nr
            xi_ref[:, ls] = ni
        else:
            lower = lax.broadcasted_iota(jnp.int32, (SUBLANES, lane_group), 0) < nb

            def pair(k, carry, ls=ls, ar=ar, ai=ai, lower=lower):
                hr, hi = carry
                rows = pl.ds(pl.multiple_of(k * SUBLANES, SUBLANES), SUBLANES)
                x_r, x_i = xr_ref[rows, ls], xi_ref[rows, ls]
                ar_ = ar * hr - ai * hi + x_r
                ai_ = ar * hi + ai * hr + x_i
                sr, si = pltpu.roll(ar_, nb, 0), pltpu.roll(ai_, nb, 0)
                br_ = ar * sr - ai * si + x_r
                bi_ = ar * si + ai * sr + x_i
                xr_ref[rows, ls] = jnp.where(lower, ar_, br_)
                xi_ref[rows, ls] = jnp.where(lower, ai_, bi_)
                return pltpu.roll(br_, nb, 0), pltpu.roll(bi_, nb, 0)

            nr, ni = lax.fori_loop(0, tc // 2, pair, (hr_ref[:, ls], hi_ref[:, ls]), unroll=2)
        hr_ref[:, ls] = nr
        hi_ref[:, ls] = ni

    ys = []
    for c in range(S5_COLS):
        cs = pl.ds(c * cw, cw)
        ys.append(_dot(xr_ref[:, cs].astype(BF16), wcr_ref[c])
                  + _dot(xi_ref[:, cs].astype(BF16), wci_ref[c]))
    y = jnp.concatenate(ys, axis=1) + d_ref[...] * u
    y = _gelu(y)
    z = _dot(y.astype(BF16), wglu_ref[...]) + bglu_ref[...]
    o_ref[...] = (y * jax.nn.sigmoid(z)).astype(BF16)


def _s5_call(z, prm, h0r, h0i, nb, tc, row_block0, n_chunks, y_prev=None):
    m = z.shape[0]
    d_a = prm["d"].shape[1]
    n_state = prm["ar"].shape[1]
    r = nb * tc
    sr = h0r.shape[0]
    fix2 = lambda i: (0, 0)
    fix3 = lambda i: (0, 0, 0)
    blk = lambda i: (row_block0 + i, 0)
    ins = [z, prm["wb"], prm["ar"], prm["ai"], prm["wcr"], prm["wci"], prm["d"], prm["wglu"],
           prm["bglu"], h0r, h0i]
    specs = [pl.BlockSpec((r, d_a), blk),
             pl.BlockSpec(prm["wb"].shape, fix3),
             pl.BlockSpec((1, n_state), fix2), pl.BlockSpec((1, n_state), fix2),
             pl.BlockSpec(prm["wcr"].shape, fix3), pl.BlockSpec(prm["wci"].shape, fix3),
             pl.BlockSpec((1, d_a), fix2), pl.BlockSpec((d_a, d_a), fix2),
             pl.BlockSpec((1, d_a), fix2),
             pl.BlockSpec((sr, n_state), fix2), pl.BlockSpec((sr, n_state), fix2)]
    aliases = {}
    if y_prev is not None:
        ins.append(y_prev)
        specs.append(pl.BlockSpec(memory_space=pl.ANY))
        aliases = {len(ins) - 1: 0}
    return pl.pallas_call(
        functools.partial(_s5_kernel, nb=nb, tc=tc, aliased=y_prev is not None),
        out_shape=(jax.ShapeDtypeStruct((m, d_a), BF16),
                   jax.ShapeDtypeStruct((sr, n_state), F32),
                   jax.ShapeDtypeStruct((sr, n_state), F32)),
        grid=(n_chunks,),
        in_specs=specs,
        out_specs=(pl.BlockSpec((r, d_a), blk),
                   pl.BlockSpec((sr, n_state), fix2), pl.BlockSpec((sr, n_state), fix2)),
        scratch_shapes=[pltpu.VMEM((r, n_state), F32), pltpu.VMEM((r, n_state), F32)],
        input_output_aliases=aliases,
        compiler_params=_cp("arbitrary"),
        name="s5",
    )(*ins)


def _s5_params(lam_re, lam_im, b_re, b_im, c_re, c_im, d_skip, log_step, w_glu, b_glu):
    g, n = lam_re.shape
    gs = b_re.shape[2]
    step = jnp.exp(log_step)[:, None]
    mag = jnp.exp(lam_re * step)
    abar_r, abar_i = mag * jnp.cos(lam_im * step), mag * jnp.sin(lam_im * step)
    den = lam_re * lam_re + lam_im * lam_im
    pr, pim = abar_r - 1.0, abar_i
    zr = (pr * lam_re + pim * lam_im) / den
    zi = (pim * lam_re - pr * lam_im) / den
    bbr = zr[..., None] * b_re - zi[..., None] * b_im
    bbi = zr[..., None] * b_im + zi[..., None] * b_re
    gl = LANES // gs
    nc = g // gl

    def in_mat(bb):
        a = jnp.transpose(bb.reshape(nc, gl, n, gs), (0, 1, 3, 2)).reshape(nc, gl * gs, n)
        return jnp.tile(a, (1, 1, gl)) * _block_diag_mask(gl, gs, n)

    def out_mat(cc):
        a = jnp.transpose(cc.reshape(nc, gl, gs, n), (0, 1, 3, 2)).reshape(nc, gl * n, gs)
        return jnp.tile(a, (1, 1, gl)) * _block_diag_mask(gl, n, gs)

    return dict(
        wb=jnp.concatenate([in_mat(bbr), in_mat(bbi)], axis=-1).astype(BF16),
        wcr=out_mat(c_re).astype(BF16),
        wci=out_mat(-c_im).astype(BF16),
        ar=abar_r.reshape(1, g * n), ai=abar_i.reshape(1, g * n),
        d=d_skip.reshape(1, -1), wglu=w_glu.astype(BF16), bglu=b_glu.reshape(1, -1))


def _log_f(fz, log_lb, log1m_lb):
    b = log1m_lb + _log_sigmoid(fz)
    a = jnp.broadcast_to(log_lb, b.shape)
    return jnp.maximum(a, b) + jnp.log1p(jnp.exp(-jnp.abs(a - b)))


def _hgrn_consts(tc, nb):
    r = tc * nb
    levels = int(np.log2(tc))
    t = np.arange(r) // nb
    b = np.arange(r) % nb
    same_b = b[:, None] == b[None, :]
    tt, ts = t[:, None], t[None, :]
    expo = np.zeros((levels + 2, r, r), np.float32)
    mask = np.zeros((levels + 1, r, r), np.float32)
    for l in range(levels):
        blk, half = 2 << l, 1 << l
        split = (t // blk) * blk + half - 1
        upper = (t % blk) >= half
        sp = split[:, None]
        up = upper[:, None]
        expo[l] = same_b & np.where(up, (ts > sp) & (ts <= tt), (ts > tt) & (ts <= sp))
        mask[l] = same_b & ((tt // blk) == (ts // blk)) & up & ~upper[None, :]
    expo[levels] = same_b & (ts <= tt)
    expo[levels + 1] = same_b & (ts > tt)
    mask[levels] = np.eye(r, dtype=np.float32)
    bmask = np.zeros((nb, r, LANES), np.float32)
    for k in range(nb):
        bmask[k, b == k, :] = 1.0
    return expo, mask, bmask


def _hgrn_kernel(q_ref, fz_ref, v_ref, g_ref, llb_ref, l1m_ref, om_ref, gw_ref, expo_ref, mask_ref,
                 bm_ref, s0_ref, o_ref, st_ref, sc_ref, gs_ref, *, nb):
    levels = mask_ref.shape[0] - 1
    hd = HG_DIM

    @pl.when(pl.program_id(0) == 0)
    def _():
        st_ref[...] = s0_ref[...]

    fz = fz_ref[...]
    lf = _log_f(fz, llb_ref[...], l1m_ref[...])
    kk = om_ref[...] * jax.nn.sigmoid(-fz)
    q = q_ref[...]
    lf_hi = lf.astype(BF16)
    lf_lo = (lf - lf_hi.astype(F32)).astype(BF16)

    def expo(l):
        e = expo_ref[l]
        return _dot(e, lf_hi) + _dot(e, lf_lo)

    qb, kb = q.astype(BF16), kk.astype(BF16)
    for h in range(HG_HEADS):
        hs = slice(h * hd, (h + 1) * hd)
        sc_ref[h] = mask_ref[levels] * _dot_nt(qb[:, hs], kb[:, hs])
    for l in range(levels):
        e = jnp.exp(expo(l))
        qt = (q * e).astype(BF16)
        kt = (kk * e).astype(BF16)
        for h in range(HG_HEADS):
            hs = slice(h * hd, (h + 1) * hd)
            sc_ref[h] += mask_ref[l] * _dot_nt(qt[:, hs], kt[:, hs])

    gcum = expo(levels)
    gs_ref[...] = gcum
    qg = (q * jnp.exp(gcum)).astype(BF16)
    kend = (kk * jnp.exp(expo(levels + 1))).astype(BF16)
    v = v_ref[...]
    vb = v.astype(BF16)
    r = q.shape[0]
    gw = gw_ref[...]
    for h in range(HG_HEADS):
        hs = slice(h * hd, (h + 1) * hd)
        st = st_ref[h]
        o = _dot(sc_ref[h].astype(BF16), vb[:, hs])
        oi = _dot_nt(qg[:, hs], st.astype(BF16))
        for k in range(nb):
            o = o + bm_ref[k] * oi[:, k * hd:(k + 1) * hd]
        vcat = jnp.concatenate([(v[:, hs] * bm_ref[k]).astype(BF16) for k in range(nb)], axis=1)
        upd = _dot_tn(vcat, kend[:, hs])
        dec = jnp.concatenate(
            [jnp.broadcast_to(jnp.exp(gs_ref[r - nb + k:r - nb + k + 1, hs]), (hd, hd)) for k in range(nb)],
            axis=0)
        st_ref[h] = dec * st + upd
        o = o * lax.rsqrt(jnp.mean(o * o, axis=-1, keepdims=True) + EPS) * gw
        o_ref[:, hs] = (o * _silu(g_ref[:, hs])).astype(BF16)


def _hgrn_call(z, lbp, gnorm_w, s0t, nb, n_chunks):
    m = z.shape[0]
    d_b = HG_HEADS * HG_DIM
    r = HG_CHUNK_T * nb
    expo, mask, bmask = _hgrn_consts(HG_CHUNK_T, nb)
    fix2 = lambda i: (0, 0)
    fix3 = lambda i: (0, 0, 0)
    col = lambda c: pl.BlockSpec((r, d_b), functools.partial(lambda i, c: (i, c), c=c))
    vec = pl.BlockSpec((1, d_b), fix2)
    return pl.pallas_call(
        functools.partial(_hgrn_kernel, nb=nb),
        out_shape=(jax.ShapeDtypeStruct((m, d_b), BF16),
                   jax.ShapeDtypeStruct(s0t.shape, F32)),
        grid=(n_chunks,),
        in_specs=[col(1), col(2), col(3), col(4), vec, vec, vec,
                  pl.BlockSpec((1, HG_DIM), fix2),
                  pl.BlockSpec(expo.shape, fix3), pl.BlockSpec(mask.shape, fix3),
                  pl.BlockSpec(bmask.shape, fix3), pl.BlockSpec(s0t.shape, fix3)],
        out_specs=(pl.BlockSpec((r, d_b), lambda i: (i, 0)), pl.BlockSpec(s0t.shape, fix3)),
        scratch_shapes=[pltpu.VMEM((HG_HEADS, r, r), F32), pltpu.VMEM((r, d_b), F32)],
        compiler_params=_cp("arbitrary"),
        name="hgrn",
    )(z, z, z, z, lbp["log_lb"], lbp["log1m_lb"], lbp["one_m_lb"], gnorm_w.reshape(1, -1),
      jnp.asarray(expo, BF16), jnp.asarray(mask, F32), jnp.asarray(bmask, F32), s0t)


def _hgrn_dec_kernel(*refs, aliased):
    q_ref, fz_ref, v_ref, g_ref, llb_ref, l1m_ref, om_ref, gw_ref, fsel_ref, qsel_ref, s_ref = refs[:11]
    o_ref, so_ref = refs[13:] if aliased else refs[11:]
    hd = HG_DIM
    sb = q_ref.shape[0]
    fz = fz_ref[...]
    f = jnp.exp(_log_f(fz, llb_ref[...], l1m_ref[...]))
    kk = om_ref[...] * jax.nn.sigmoid(-fz)
    v = v_ref[...]
    gw = gw_ref[...]
    parts = [p.astype(F32) for p in _split3(f)]
    parts += [kk.astype(BF16).astype(F32), q_ref[...].astype(BF16).astype(F32)]
    n_f = 3 * sb
    zpad = jnp.zeros((hd - len(parts) * sb, hd), F32)
    wide = sb * hd
    own = (lax.broadcasted_iota(jnp.int32, (sb, wide), 0)
           == lax.shift_right_logical(lax.broadcasted_iota(jnp.int32, (sb, wide), 1), 7))
    outs = []
    for h in range(HG_HEADS):
        hs = slice(h * hd, (h + 1) * hd)
        pt = jnp.concatenate([p[:, hs] for p in parts] + [zpad], axis=0).T.astype(BF16)
        vt = jnp.where(own, jnp.concatenate([v[:, hs]] * sb, axis=1), 0.0)
        vall = jnp.concatenate([jnp.zeros((n_f, wide), F32), vt,
                                jnp.zeros((hd - n_f - sb, wide), F32)], axis=0).astype(BF16)
        prod = _dot(pt, jnp.concatenate([fsel_ref[...], vall, qsel_ref[...]], axis=1))
        o_rows = []
        for j in range(sb):
            js = slice(j * hd, (j + 1) * hd)
            s_new = prod[:, js] * s_ref[0, j, h] + prod[:, wide:2 * wide][:, js]
            so_ref[0, j, h] = s_new
            o_rows.append(jnp.sum(prod[:, 2 * wide:][:, js] * s_new, axis=0, keepdims=True))
        o = jnp.concatenate(o_rows, axis=0)
        outs.append(o * lax.rsqrt(jnp.mean(o * o, axis=-1, keepdims=True) + EPS) * gw)
    o_all = jnp.concatenate(outs, axis=1)
    o_ref[...] = (o_all * _silu(g_ref[...])).astype(BF16)


def _hgrn_dec_call(z, row0, n_rows, lbp, gnorm_w, s_all, j, y_prev, s_prev, sb=SUBLANES):
    d_b = HG_HEADS * HG_DIM
    rb0 = row0 // sb
    fix2 = lambda i: (0, 0)
    col = lambda c: pl.BlockSpec((sb, d_b), functools.partial(lambda i, c: (rb0 + i, c), c=c))
    vec = pl.BlockSpec((1, d_b), fix2)
    sblk = pl.BlockSpec((1, sb, HG_HEADS, HG_DIM, HG_DIM), lambda i: (j, i, 0, 0, 0))
    fsel = np.zeros((HG_DIM, sb * HG_DIM), np.float32)
    qsel = np.zeros((HG_DIM, sb * HG_DIM), np.float32)
    for jj in range(sb):
        fsel[jj:3 * sb:sb, jj * HG_DIM:(jj + 1) * HG_DIM] = 1.0
        qsel[4 * sb + jj, jj * HG_DIM:(jj + 1) * HG_DIM] = 1.0
    sel = pl.BlockSpec(fsel.shape, fix2)
    ins = [z, z, z, z, lbp["log_lb"], lbp["log1m_lb"], lbp["one_m_lb"], gnorm_w.reshape(1, -1),
           jnp.asarray(fsel, BF16), jnp.asarray(qsel, BF16), s_all, y_prev]
    specs = [col(1), col(2), col(3), col(4), vec, vec, vec, pl.BlockSpec((1, HG_DIM), fix2),
             sel, sel, sblk, pl.BlockSpec(memory_space=pl.ANY)]
    aliases = {11: 0}
    if s_prev is not None:
        ins.append(s_prev)
        specs.append(pl.BlockSpec(memory_space=pl.ANY))
        aliases[12] = 1
    else:
        ins.append(jnp.zeros((SUBLANES, LANES), F32))
        specs.append(pl.BlockSpec((SUBLANES, LANES), fix2))
    return pl.pallas_call(
        functools.partial(_hgrn_dec_kernel, aliased=True),
        out_shape=(jax.ShapeDtypeStruct(y_prev.shape, BF16), jax.ShapeDtypeStruct(s_all.shape, F32)),
        grid=(n_rows // sb,),
        in_specs=specs,
        out_specs=(pl.BlockSpec((sb, d_b), lambda i: (rb0 + i, 0)), sblk),
        input_output_aliases=aliases,
        compiler_params=_cp("parallel"),
        name="hgrn_dec",
    )(*ins)


def _lru_coeffs(xc, wg_ref, bga_ref, bgx_ref, lam_ref, a_ref, b_ref):
    sw = LRU_SUPER
    for j in range(xc.shape[1] // sw):
        cs = slice(j * sw, (j + 1) * sw)
        xj = xc[:, cs]
        gj = _dot(xj.astype(BF16), wg_ref[j])
        rg = jax.nn.sigmoid(gj[:, :sw] + bga_ref[:, cs])
        ig = jax.nn.sigmoid(gj[:, sw:] + bgx_ref[:, cs])
        log_a = LRU_C * rg * _log_sigmoid(lam_ref[:, cs])
        a = jnp.exp(log_a)
        a_ref[:, cs] = a
        b_ref[:, cs] = jnp.sqrt(-jnp.tanh(log_a) * (a * a + 1.0)) * ig * xj


def _lru_kernel(gb_ref, xr_ref, cw_ref, cb_ref, wg_ref, bga_ref, bgx_ref, lam_ref, c0_ref, h0_ref,
                o_ref, hl_ref, xs_ref, a_ref, b_ref, *, nb):
    r = xr_ref.shape[0]
    hdr = xs_ref.shape[0] - r
    kw = cw_ref.shape[0]

    @pl.when(pl.program_id(0) == 0)
    def _():
        xs_ref[0:hdr, :] = c0_ref[...]
        hl_ref[...] = h0_ref[...]

    xs_ref[hdr:hdr + r, :] = xr_ref[...]
    odd = xs_ref[hdr - (kw - 1) * nb:hdr - nb + r, :]
    taps = [odd[0:r], xs_ref[hdr - 2 * nb:hdr - 2 * nb + r, :], odd[2 * nb:2 * nb + r], xs_ref[hdr:hdr + r, :]]
    xc = cb_ref[...] + taps[0] * cw_ref[0:1, :]
    for j in range(1, kw):
        xc = xc + taps[j] * cw_ref[j:j + 1, :]
    xs_ref[0:hdr, :] = xs_ref[r:r + hdr, :]

    _lru_coeffs(xc, wg_ref, bga_ref, bgx_ref, lam_ref, a_ref, b_ref)

    lower = lax.broadcasted_iota(jnp.int32, (SUBLANES, xc.shape[1]), 0) < nb

    def pair(k, h):
        rows = pl.ds(pl.multiple_of(k * SUBLANES, SUBLANES), SUBLANES)
        a8 = a_ref[rows, :]
        b8 = b_ref[rows, :]
        h_a = a8 * h + b8
        h_b = a8 * pltpu.roll(h_a, nb, 0) + b8
        b_ref[rows, :] = jnp.where(lower, h_a, h_b)
        return pltpu.roll(h_b, nb, 0)

    hl_ref[...] = lax.fori_loop(0, r // SUBLANES, pair, hl_ref[...], unroll=2)
    o_ref[...] = (_gelu(gb_ref[...]) * b_ref[...]).astype(BF16)


def _lru_call(z, prm, conv0, h0, nb, n_chunks):
    m = z.shape[0]
    d_rnn = prm["lam"].shape[1]
    r = nb * LRU_CHUNK_T
    hdr = conv0.shape[0]
    fix2 = lambda i: (0, 0)
    fix3 = lambda i: (0, 0, 0)
    vec = pl.BlockSpec((1, d_rnn), fix2)
    return pl.pallas_call(
        functools.partial(_lru_kernel, nb=nb),
        out_shape=(jax.ShapeDtypeStruct((m, d_rnn), BF16),
                   jax.ShapeDtypeStruct((SUBLANES, d_rnn), F32)),
        grid=(n_chunks,),
        in_specs=[pl.BlockSpec((r, d_rnn), lambda i: (i, 0)),
                  pl.BlockSpec((r, d_rnn), lambda i: (i, 1)),
                  pl.BlockSpec(prm["cw"].shape, fix2), vec,
                  pl.BlockSpec(prm["wg"].shape, fix3), vec, vec, vec,
                  pl.BlockSpec((hdr, d_rnn), fix2), pl.BlockSpec((SUBLANES, d_rnn), fix2)],
        out_specs=(pl.BlockSpec((r, d_rnn), lambda i: (i, 0)),
                   pl.BlockSpec((SUBLANES, d_rnn), fix2)),
        scratch_shapes=[pltpu.VMEM((hdr + r, d_rnn), F32), pltpu.VMEM((r, d_rnn), F32),
                        pltpu.VMEM((r, d_rnn), F32)],
        compiler_params=_cp("arbitrary"),
        name="lru",
    )(z, z, prm["cw"], prm["cb"], prm["wg"], prm["bga"], prm["bgx"], prm["lam"], conv0, h0)


def _lru_dec_kernel(gb_ref, xr_ref, buf_ref, cw_ref, cb_ref, wg_ref, bga_ref, bgx_ref, lam_ref,
                    h0_ref, yp_ref, o_ref, h_ref, a_ref, b_ref):
    kw = cw_ref.shape[0]
    xc = cb_ref[...] + xr_ref[...] * cw_ref[kw - 1:kw, :]
    for j in range(kw - 1):
        xc = xc + buf_ref[j] * cw_ref[j:j + 1, :]
    _lru_coeffs(xc, wg_ref, bga_ref, bgx_ref, lam_ref, a_ref, b_ref)
    h = a_ref[...] * h0_ref[...] + b_ref[...]
    h_ref[...] = h
    o_ref[...] = (_gelu(gb_ref[...]) * h).astype(BF16)


def _lru_dec_call(z, row0, prm, buf, h0, y_prev):
    n = h0.shape[0]
    d_rnn = prm["lam"].shape[1]
    rb0 = row0 // n
    fix2 = lambda i: (0, 0)
    fix3 = lambda i: (0, 0, 0)
    vec = pl.BlockSpec((1, d_rnn), fix2)
    full = pl.BlockSpec((n, d_rnn), fix2)
    return pl.pallas_call(
        _lru_dec_kernel,
        out_shape=(jax.ShapeDtypeStruct(y_prev.shape, BF16), jax.ShapeDtypeStruct((n, d_rnn), F32)),
        grid=(1,),
        in_specs=[pl.BlockSpec((n, d_rnn), lambda i: (rb0, 0)),
                  pl.BlockSpec((n, d_rnn), lambda i: (rb0, 1)),
                  pl.BlockSpec(buf.shape, fix3),
                  pl.BlockSpec(prm["cw"].shape, fix2), vec,
                  pl.BlockSpec(prm["wg"].shape, fix3), vec, vec, vec, full,
                  pl.BlockSpec(memory_space=pl.ANY)],
        out_specs=(pl.BlockSpec((n, d_rnn), lambda i: (rb0, 0)), full),
        scratch_shapes=[pltpu.VMEM((n, d_rnn), F32), pltpu.VMEM((n, d_rnn), F32)],
        input_output_aliases={10: 0},
        compiler_params=_cp("arbitrary"),
        name="lru_dec",
    )(z, z, buf, prm["cw"], prm["cb"], prm["wg"], prm["bga"], prm["bgx"], prm["lam"], h0, y_prev)


def _lru_params(conv_w, conv_b, w_ga, b_ga, w_gx, b_gx, lam):
    nblk, bs = w_ga.shape[0], w_ga.shape[1]
    per = LRU_SUPER // bs
    ns = nblk // per

    def sup(w):
        return jnp.tile(w.reshape(ns, per * bs, bs), (1, 1, per)) * _block_diag_mask(per, bs, bs)

    return dict(cw=conv_w, cb=conv_b.reshape(1, -1),
                wg=jnp.concatenate([sup(w_ga), sup(w_gx)], axis=-1).astype(BF16),
                bga=b_ga.reshape(1, -1), bgx=b_gx.reshape(1, -1), lam=lam.reshape(1, -1))


def kernel(x_prompt, x_sample, state_s5_re, state_s5_im, state_hgrn, state_lru, state_conv,
           c_prompt, c_sample, norm_w, final_norm_w, w_ada, b_ada, w_ffn_gu, w_ffn_d,
           w_in_ab, s5_lam_re, s5_lam_im, s5_b_re, s5_b_im, s5_c_re, s5_c_im, s5_d, s5_log_step,
           s5_w_glu, s5_b_glu, hg_lb_logits, hg_norm_w, w_out_ab, w_in_c, conv_w, conv_b,
           w_gate_a, b_gate_a, w_gate_x, b_gate_x, lru_lambda, w_out_c):
    bsz, seq, d = x_prompt.shape
    nsm = x_sample.shape[0]
    depth = w_ada.shape[0]
    n_ab, n_c = w_in_ab.shape[0], w_in_c.shape[0]
    g_a, n_a = s5_lam_re.shape[1], s5_lam_re.shape[2]
    n_state = g_a * n_a
    d_rnn = lru_lambda.shape[1]
    kw = conv_w.shape[1]
    mp_rows = seq * bsz
    assert 2 * bsz == SUBLANES and mp_rows % nsm == 0 and (mp_rows + nsm) % ROW_TILE == 0

    s5p = [_s5_params(s5_lam_re[j], s5_lam_im[j], s5_b_re[j], s5_b_im[j], s5_c_re[j], s5_c_im[j],
                      s5_d[j], s5_log_step[j], s5_w_glu[j], s5_b_glu[j]) for j in range(n_ab)]
    lb_all = jnp.cumsum(jax.nn.softmax(hg_lb_logits.astype(F32), axis=0), axis=0)
    lb_all = lb_all - lb_all[0:1]
    lbp = [dict(log_lb=jnp.log(lb_all[j]).reshape(1, -1),
                log1m_lb=jnp.log1p(-lb_all[j]).reshape(1, -1),
                one_m_lb=(1.0 - lb_all[j]).reshape(1, -1)) for j in range(n_ab)]
    lrup = [_lru_params(conv_w[j], conv_b[j], w_gate_a[j], b_gate_a[j], w_gate_x[j], b_gate_x[j],
                        lru_lambda[j]) for j in range(n_c)]
    w_out_ab_bf = w_out_ab.astype(BF16)
    w_out_c_bf = w_out_c.astype(BF16)
    norm_w = norm_w.reshape(depth * 3, 1, d)

    c_all = jnp.concatenate([c_sample] + [c_prompt] * (SUBLANES // bsz), axis=0)
    mods = _ada_call(c_all, w_ada, b_ada)

    x = _pack_call(x_prompt, x_sample.reshape(nsm, d), FINAL_TILE_T)
    zeros_state = jnp.zeros((SUBLANES, n_state), F32)
    s5r_p, s5i_p, hg_p, lru_p, conv_p = [], [], [], [], []
    s5r_s, s5i_s, lru_s, conv_s = [], [], [], []
    hg_s = None
    for l in range(depth):
        j = l // 2
        x = _ffn_call(x, norm_w, mods, nsm, w_ffn_gu, w_ffn_d, l, 0, 0, ROW_TILE, 256)
        if l % 2 == 0:
            z = _inproj_call(x, norm_w, mods, nsm, w_in_ab, l, j, ROW_TILE, 1024)
            ya, hr, hi = _s5_call(z, s5p[j], zeros_state, zeros_state, bsz, S5_CHUNK_T,
                                  0, seq // S5_CHUNK_T)
            ya, hrs, his = _s5_call(z, s5p[j], state_s5_re[j].reshape(nsm, n_state),
                                    state_s5_im[j].reshape(nsm, n_state), nsm, 1,
                                    mp_rows // nsm, 1, y_prev=ya)
            yb, hg = _hgrn_call(z, lbp[j], hg_norm_w[j],
                                jnp.zeros((HG_HEADS, bsz * HG_DIM, HG_DIM), F32), bsz, seq // HG_CHUNK_T)
            yb, hg_s = _hgrn_dec_call(z, mp_rows, nsm, lbp[j], hg_norm_w[j], state_hgrn, j, yb, hg_s)
            s5r_p.append(hr[:bsz].reshape(bsz, g_a, n_a))
            s5i_p.append(hi[:bsz].reshape(bsz, g_a, n_a))
            s5r_s.append(hrs.reshape(nsm, g_a, n_a))
            s5i_s.append(his.reshape(nsm, g_a, n_a))
            hg_p.append(jnp.transpose(hg.reshape(HG_HEADS, bsz, HG_DIM, HG_DIM), (1, 0, 3, 2)))
            x = _outproj_call(x, mods, nsm, l, [ya, yb], w_out_ab_bf[j], OUT_ROW_TILE)
        else:
            z = _inproj_call(x, norm_w, mods, nsm, w_in_c, l, j, ROW_TILE, 1024)
            y, hl = _lru_call(z, lrup[j], jnp.zeros((4 * bsz, d_rnn), F32),
                              jnp.zeros((SUBLANES, d_rnn), F32), bsz, seq // LRU_CHUNK_T)
            y, hls = _lru_dec_call(z, mp_rows, lrup[j], jnp.transpose(state_conv[j], (1, 0, 2)),
                                   state_lru[j], y)
            lru_p.append(hl[:bsz])
            lru_s.append(hls)
            tail = z[mp_rows - (kw - 1) * bsz:mp_rows, d_rnn:]
            conv_p.append(jnp.transpose(tail.reshape(kw - 1, bsz, d_rnn), (1, 0, 2)))
            conv_s.append(jnp.concatenate([state_conv[j][:, 1:], z[mp_rows:, None, d_rnn:]], axis=1))
            x = _outproj_call(x, mods, nsm, l, [y], w_out_c_bf[j], OUT_ROW_TILE)
        x = _ffn_call(x, norm_w, mods, nsm, w_ffn_gu, w_ffn_d, l, 1, 2, ROW_TILE, 256)
    y_prompt, y_sample = _final_norm_call(x, final_norm_w.reshape(1, -1), bsz, seq, nsm, FINAL_TILE_T)
    return (y_prompt, y_sample.reshape(nsm, 1, d), jnp.stack(s5r_p), jnp.stack(s5i_p), jnp.stack(hg_p),
            jnp.stack(lru_p), jnp.stack(conv_p),
            jnp.stack(s5r_s), jnp.stack(s5i_s), hg_s, jnp.stack(lru_s), jnp.stack(conv_s))
```

```python
import functools

import numpy as np
import jax
import jax.numpy as jnp
from jax import lax
from jax.experimental import pallas as pl
from jax.experimental.pallas import tpu as pltpu

F32 = jnp.float32
BF16 = jnp.bfloat16
EPS = 1e-6
LRU_C = 8.0

VMEM_LIMIT_BYTES = 60 * 1024 * 1024
SUBLANES = 8
LANES = 128
BF16_ROWS = 16

S5_COLS = 8
S5_CHUNK_T = 128
HG_HEADS = 8
HG_DIM = 128
HG_CHUNK_T = 64
LRU_CHUNK_T = 64
LRU_SUPER = 640
N_MOD = 9
ROW_TILE = 1040
OUT_ROW_TILE = 640
IN_ROW_TILE = 1664
IN_COL_TILE = 512
FINAL_TILE_T = 256
MXU_COLS = 256


def _cp(*sem):
    return pltpu.CompilerParams(dimension_semantics=sem, vmem_limit_bytes=VMEM_LIMIT_BYTES)


def _dot(a, b):
    return jnp.dot(a, b, preferred_element_type=F32)


def _dot_nt(a, b):
    return lax.dot_general(a, b, (((1,), (1,)), ((), ())), preferred_element_type=F32)


def _dot_tn(a, b):
    return lax.dot_general(a, b, (((0,), (0,)), ((), ())), preferred_element_type=F32)


def _silu(x):
    return x * jax.nn.sigmoid(x)


def _gelu(x):
    return jax.nn.gelu(x, approximate=True)


def _log_sigmoid(x):
    return jnp.minimum(x, 0.0) - jnp.log1p(jnp.exp(-jnp.abs(x)))


def _block_diag_mask(nblk, rows, cols):
    r = np.arange(nblk * rows)[:, None] // rows
    c = np.arange(nblk * cols)[None, :] // cols
    return jnp.asarray(r == c, F32)


def _split3(x):
    hi = x.astype(BF16)
    r1 = x - hi.astype(F32)
    mid = r1.astype(BF16)
    lo = (r1 - mid.astype(F32)).astype(BF16)
    return hi, mid, lo


def _fma_rows(y, mul, add):
    rm = mul.shape[0]
    r, d = y.shape
    if rm == r:
        return y * mul + add
    y3 = y.reshape(r // rm, rm, d)
    return (y3 * mul[None] + add[None]).reshape(r, d)


def _mul_rows(y, mul):
    rm = mul.shape[0]
    r, d = y.shape
    if rm == r:
        return y * mul
    return (y.reshape(r // rm, rm, d) * mul[None]).reshape(r, d)


def _norm_mod_rows(x_ref, nw_ref, sh_ref, sc_ref, h_ref, row0, nrows):
    slab = BF16_ROWS
    per_row = sh_ref.shape[0] > SUBLANES
    nw = nw_ref[...]

    def body(s, carry):
        r0 = pl.multiple_of(row0 + s * slab, slab)
        x = x_ref[pl.ds(r0, slab), :]
        ms = jnp.mean(x * x, axis=-1, keepdims=True)
        y = x * lax.rsqrt(ms + EPS) * nw
        if per_row:
            m0 = pl.multiple_of(s * slab, slab)
            h = y * (1.0 + sc_ref[pl.ds(m0, slab), :]) + sh_ref[pl.ds(m0, slab), :]
        else:
            h = _fma_rows(y, 1.0 + sc_ref[...], sh_ref[...])
        h_ref[pl.ds(r0, slab), :] = h.astype(BF16)
        return carry

    lax.fori_loop(0, nrows // slab, body, 0, unroll=8)


def _residual_rows(o_ref, x_ref, acc, gt_ref, gts_ref, scale):
    tm = o_ref.shape[0]
    ns = gts_ref.shape[0]
    np_ = tm - ns
    is_last = pl.program_id(0) == pl.num_programs(0) - 1
    gp = scale * gt_ref[...]
    o_ref[0:np_, :] = x_ref[0:np_, :] + _mul_rows(acc(0, np_), gp)

    @pl.when(jnp.logical_not(is_last))
    def _():
        o_ref[np_:tm, :] = x_ref[np_:tm, :] + _mul_rows(acc(np_, tm), gp)

    @pl.when(is_last)
    def _():
        o_ref[np_:tm, :] = x_ref[np_:tm, :] + acc(np_, tm) * (scale * gts_ref[...])


def _ada_kernel(c_ref, w_ref, b_ref, o_ref):
    sc = _silu(c_ref[...]).astype(BF16)
    o_ref[0] = _dot(sc, w_ref[0].astype(BF16)) + b_ref[0]


def _ada_call(c_all, w_ada, b_ada, tn=1024):
    depth, d, n = w_ada.shape
    r = c_all.shape[0]
    return pl.pallas_call(
        _ada_kernel,
        out_shape=jax.ShapeDtypeStruct((depth, r, n), F32),
        grid=(depth, n // tn),
        in_specs=[pl.BlockSpec((r, d), lambda l, j: (0, 0)),
                  pl.BlockSpec((1, d, tn), lambda l, j: (l, 0, j)),
                  pl.BlockSpec((1, 1, tn), lambda l, j: (l, 0, j))],
        out_specs=pl.BlockSpec((1, r, tn), lambda l, j: (l, 0, j)),
        compiler_params=_cp("parallel", "parallel"),
        name="ada",
    )(c_all, w_ada, b_ada.reshape(depth, 1, n))


def _ffn_kernel(x_ref, nw_ref, sh_ref, sc_ref, gt_ref, shs_ref, scs_ref, gts_ref,
                wg_ref, wu_ref, wd_ref, o_ref, h_ref):
    i, f = pl.program_id(0), pl.program_id(1)
    nw_ref, sh_ref, sc_ref, gt_ref, shs_ref, scs_ref, gts_ref = (
        r.at[0] for r in (nw_ref, sh_ref, sc_ref, gt_ref, shs_ref, scs_ref, gts_ref))
    tm = x_ref.shape[0]
    ns = shs_ref.shape[0]
    is_last = i == pl.num_programs(0) - 1

    @pl.when(f == 0)
    def _():
        _norm_mod_rows(x_ref, nw_ref, sh_ref, sc_ref, h_ref, 0, tm)
        o_ref[...] = jnp.zeros_like(o_ref)

    @pl.when(jnp.logical_and(f == 0, is_last))
    def _():
        _norm_mod_rows(x_ref, nw_ref, shs_ref, scs_ref, h_ref, tm - ns, ns)

    h = h_ref[...]
    g = _dot(h, wg_ref[0, 0].astype(BF16))
    u = _dot(h, wu_ref[0, 0].astype(BF16))
    a = (_silu(g) * u).astype(BF16)
    o_ref[...] += _dot(a, wd_ref[0, 0].astype(BF16))

    @pl.when(f == pl.num_programs(1) - 1)
    def _():
        _residual_rows(o_ref, x_ref, lambda a0, a1: o_ref[a0:a1, :], gt_ref, gts_ref, 0.5)


def _mod_specs(mods, ns, d, l, ks):
    def pat(k):
        return pl.BlockSpec((1, SUBLANES, d), lambda *g: (l, ns // SUBLANES, k))

    def smp(k):
        return pl.BlockSpec((1, ns, d), lambda *g: (l, 0, k))

    return [pat(k) for k in ks] + [smp(k) for k in ks], [mods] * (2 * len(ks))


def _ffn_call(x, norm_w, mods, ns, w_gu, w_d, l, s, sub, tm, tf):
    m, d = x.shape
    ff = w_d.shape[2]
    nf = ff // tf
    row = lambda i, f: (i, 0)
    mspecs, margs = _mod_specs(mods, ns, d, l, (3 * sub, 3 * sub + 1, 3 * sub + 2))
    return pl.pallas_call(
        _ffn_kernel,
        out_shape=jax.ShapeDtypeStruct((m, d), F32),
        grid=(m // tm, nf),
        in_specs=[pl.BlockSpec((tm, d), row), pl.BlockSpec((1, 1, d), lambda i, f: (3 * l + sub, 0, 0))]
                 + mspecs +
                 [pl.BlockSpec((1, 1, d, tf), lambda i, f: (l, s, 0, f)),
                  pl.BlockSpec((1, 1, d, tf), lambda i, f: (l, s, 0, f + nf)),
                  pl.BlockSpec((1, 1, tf, d), lambda i, f: (l, s, f, 0))],
        out_specs=pl.BlockSpec((tm, d), row),
        scratch_shapes=[pltpu.VMEM((tm, d), BF16)],
        compiler_params=_cp("parallel", "arbitrary"),
        name="ffn",
    )(x, norm_w, *margs, w_gu, w_gu, w_d)


def _inproj_kernel(x_ref, nw_ref, sh_ref, sc_ref, shs_ref, scs_ref, w_ref, o_ref, h_ref):
    i, k = pl.program_id(0), pl.program_id(1)
    nw_ref, sh_ref, sc_ref, shs_ref, scs_ref = (
        r.at[0] for r in (nw_ref, sh_ref, sc_ref, shs_ref, scs_ref))
    tm = x_ref.shape[0]
    ns = shs_ref.shape[0]

    @pl.when(k == 0)
    def _():
        _norm_mod_rows(x_ref, nw_ref, sh_ref, sc_ref, h_ref, 0, tm)

    @pl.when(jnp.logical_and(k == 0, i == pl.num_programs(0) - 1))
    def _():
        _norm_mod_rows(x_ref, nw_ref, shs_ref, scs_ref, h_ref, tm - ns, ns)

    h = h_ref[...]
    for c in range(0, o_ref.shape[1], MXU_COLS):
        o_ref[:, c:c + MXU_COLS] = _dot(h, w_ref[0, :, c:c + MXU_COLS].astype(BF16))


def _inproj_call(x, norm_w, mods, ns, w, l, j, tm, tn):
    m, d = x.shape
    n = w.shape[2]
    mspecs, margs = _mod_specs(mods, ns, d, l, (3, 4))
    return pl.pallas_call(
        _inproj_kernel,
        out_shape=jax.ShapeDtypeStruct((m, n), F32),
        grid=(m // tm, n // tn),
        in_specs=[pl.BlockSpec((tm, d), lambda i, k: (i, 0)),
                  pl.BlockSpec((1, 1, d), lambda i, k: (3 * l + 1, 0, 0))]
                 + mspecs + [pl.BlockSpec((1, d, tn), lambda i, k: (j, 0, k))],
        out_specs=pl.BlockSpec((tm, tn), lambda i, k: (i, k)),
        scratch_shapes=[pltpu.VMEM((tm, d), BF16)],
        compiler_params=_cp("parallel", "arbitrary"),
        name="inproj",
    )(x, norm_w, *margs, w)


def _outproj_kernel(*refs, n_in):
    x_ref, gt_ref, gts_ref = refs[0], refs[1].at[0], refs[2].at[0]
    a_refs = refs[3:3 + n_in]
    w_refs = refs[3 + n_in:3 + 2 * n_in]
    o_ref = refs[3 + 2 * n_in]
    acc = _dot(a_refs[0][...], w_refs[0][0])
    for a_ref, w_ref in zip(a_refs[1:], w_refs[1:]):
        acc = acc + _dot(a_ref[...], w_ref[0])
    _residual_rows(o_ref, x_ref, lambda a0, a1: acc[a0:a1, :], gt_ref, gts_ref, 1.0)


def _outproj_call(x, mods, ns, l, acts, w_bf, j, tm):
    m, d = x.shape
    n_in = len(acts)
    ka = acts[0].shape[1]
    row = lambda i: (i, 0)
    mspecs, margs = _mod_specs(mods, ns, d, l, (5,))
    in_specs = [pl.BlockSpec((tm, d), row)] + mspecs
    in_specs += [pl.BlockSpec((tm, ka), row) for _ in acts]
    in_specs += [pl.BlockSpec((1, ka, d), functools.partial(lambda i, p: (j, p, 0), p=p)) for p in range(n_in)]
    return pl.pallas_call(
        functools.partial(_outproj_kernel, n_in=n_in),
        out_shape=jax.ShapeDtypeStruct((m, d), F32),
        grid=(m // tm,),
        in_specs=in_specs,
        out_specs=pl.BlockSpec((tm, d), row),
        compiler_params=_cp("parallel"),
        name="outproj",
    )(x, *margs, *acts, *([w_bf] * n_in))


def _pack_kernel(xp_ref, xs_ref, o_ref, slab_ref, *, n_tiles):
    i = pl.program_id(0)
    nb, tt, d = xp_ref.shape

    @pl.when(i < n_tiles)
    def _():
        for c in range(d // LANES):
            cs = slice(c * LANES, (c + 1) * LANES)
            for b in range(nb):
                slab_ref[c, pl.ds(b, tt, stride=nb), :] = xp_ref[b, :, cs]
            o_ref[:, cs] = slab_ref[c]

    @pl.when(i == n_tiles)
    def _():
        o_ref[0:xs_ref.shape[0], :] = xs_ref[...]


def _pack_call(x_prompt, xs, tt):
    nb, seq, d = x_prompt.shape
    ns = xs.shape[0]
    n_tiles = seq // tt
    return pl.pallas_call(
        functools.partial(_pack_kernel, n_tiles=n_tiles),
        out_shape=jax.ShapeDtypeStruct((seq * nb + ns, d), F32),
        grid=(n_tiles + 1,),
        in_specs=[pl.BlockSpec((nb, tt, d), lambda i: (0, jnp.minimum(i, n_tiles - 1), 0)),
                  pl.BlockSpec((ns, d), lambda i: (0, 0))],
        out_specs=pl.BlockSpec((tt * nb, d), lambda i: (i, 0)),
        scratch_shapes=[pltpu.VMEM((d // LANES, tt * nb, LANES), F32)],
        compiler_params=_cp("arbitrary"),
        name="pack",
    )(x_prompt, xs)


def _final_norm_kernel(x_ref, w_ref, yp_ref, ys_ref, slab_ref, *, n_tiles):
    i = pl.program_id(0)
    nb, tt, d = yp_ref.shape
    w = w_ref[...]

    def inv_rms(x):
        return lax.rsqrt(jnp.mean(x * x, axis=-1, keepdims=True) + EPS)

    @pl.when(i < n_tiles)
    def _():
        inv = inv_rms(x_ref[...])
        for c in range(d // LANES):
            cs = slice(c * LANES, (c + 1) * LANES)
            slab_ref[c] = x_ref[:, cs] * inv * w[:, cs]
            for b in range(nb):
                yp_ref[b, :, cs] = slab_ref[c, pl.ds(b, tt, stride=nb), :]

    @pl.when(i == n_tiles)
    def _():
        x = x_ref[0:ys_ref.shape[0], :]
        ys_ref[...] = x * inv_rms(x) * w


def _final_norm_call(x, w, nb, seq, ns, tt):
    d = x.shape[1]
    n_tiles = seq // tt
    return pl.pallas_call(
        functools.partial(_final_norm_kernel, n_tiles=n_tiles),
        out_shape=(jax.ShapeDtypeStruct((nb, seq, d), F32), jax.ShapeDtypeStruct((ns, d), F32)),
        grid=(n_tiles + 1,),
        in_specs=[pl.BlockSpec((tt * nb, d), lambda i: (i, 0)),
                  pl.BlockSpec((1, d), lambda i: (0, 0))],
        out_specs=(pl.BlockSpec((nb, tt, d), lambda i: (0, jnp.minimum(i, n_tiles - 1), 0)),
                   pl.BlockSpec((ns, d), lambda i: (0, 0))),
        scratch_shapes=[pltpu.VMEM((d // LANES, tt * nb, LANES), F32)],
        compiler_params=_cp("arbitrary"),
        name="final_norm",
    )(x, w)


def _s5_kernel(*refs, nb, tc, aliased):
    (u_ref, wb_ref, ar_ref, ai_ref, wcr_ref, wci_ref, d_ref, wglu_ref, bglu_ref,
     h0r_ref, h0i_ref) = refs[:11]
    o_ref, hr_ref, hi_ref, xr_ref, xi_ref = refs[12:] if aliased else refs[11:]

    @pl.when(pl.program_id(0) == 0)
    def _():
        hr_ref[...] = h0r_ref[...]
        hi_ref[...] = h0i_ref[...]

    u = u_ref[...]
    ub = u.astype(BF16)
    n_state = xr_ref.shape[1]
    cw = n_state // S5_COLS
    for c in range(S5_COLS):
        xc = _dot(ub[:, c * LANES:(c + 1) * LANES], wb_ref[c])
        xr_ref[:, c * cw:(c + 1) * cw] = xc[:, :cw]
        xi_ref[:, c * cw:(c + 1) * cw] = xc[:, cw:]

    lane_group = 1024
    for lo_ in range(0, n_state, lane_group):
        ls = pl.ds(lo_, lane_group)
        ar = ar_ref[:, ls]
        ai = ai_ref[:, ls]
        if tc == 1:
            hr, hi = hr_ref[:, ls], hi_ref[:, ls]
            nr = ar * hr - ai * hi + xr_ref[:, ls]
            ni = ar * hi + ai * hr + xi_ref[:, ls]
            xr_ref[:, ls] = nr
            xi_ref[:, ls] = ni
        else:
            lower = lax.broadcasted_iota(jnp.int32, (SUBLANES, lane_group), 0) < nb

            def pair(k, carry, ls=ls, ar=ar, ai=ai, lower=lower):
                hr, hi = carry
                rows = pl.ds(pl.multiple_of(k * SUBLANES, SUBLANES), SUBLANES)
                x_r, x_i = xr_ref[rows, ls], xi_ref[rows, ls]
                ar_ = ar * hr - ai * hi + x_r
                ai_ = ar * hi + ai * hr + x_i
                sr, si = pltpu.roll(ar_, nb, 0), pltpu.roll(ai_, nb, 0)
                br_ = ar * sr - ai * si + x_r
                bi_ = ar * si + ai * sr + x_i
                xr_ref[rows, ls] = jnp.where(lower, ar_, br_)
                xi_ref[rows, ls] = jnp.where(lower, ai_, bi_)
                return pltpu.roll(br_, nb, 0), pltpu.roll(bi_, nb, 0)

            nr, ni = lax.fori_loop(0, tc // 2, pair, (hr_ref[:, ls], hi_ref[:, ls]), unroll=2)
        hr_ref[:, ls] = nr
        hi_ref[:, ls] = ni

    ys = []
    for c in range(S5_COLS):
        cs = pl.ds(c * cw, cw)
        ys.append(_dot(xr_ref[:, cs].astype(BF16), wcr_ref[c])
                  + _dot(xi_ref[:, cs].astype(BF16), wci_ref[c]))
    y = jnp.concatenate(ys, axis=1) + d_ref[...] * u
    y = _gelu(y)
    z = _dot(y.astype(BF16), wglu_ref[...]) + bglu_ref[...]
    o_ref[...] = (y * jax.nn.sigmoid(z)).astype(BF16)


def _s5_call(z, prm, h0r, h0i, nb, tc, row_block0, n_chunks, y_prev=None):
    m = z.shape[0]
    d_a = prm["d"].shape[1]
    n_state = prm["ar"].shape[1]
    r = nb * tc
    sr = h0r.shape[0]
    fix2 = lambda i: (0, 0)
    fix3 = lambda i: (0, 0, 0)
    blk = lambda i: (row_block0 + i, 0)
    ins = [z, prm["wb"], prm["ar"], prm["ai"], prm["wcr"], prm["wci"], prm["d"], prm["wglu"],
           prm["bglu"], h0r, h0i]
    specs = [pl.BlockSpec((r, d_a), blk),
             pl.BlockSpec(prm["wb"].shape, fix3),
             pl.BlockSpec((1, n_state), fix2), pl.BlockSpec((1, n_state), fix2),
             pl.BlockSpec(prm["wcr"].shape, fix3), pl.BlockSpec(prm["wci"].shape, fix3),
             pl.BlockSpec((1, d_a), fix2), pl.BlockSpec((d_a, d_a), fix2),
             pl.BlockSpec((1, d_a), fix2),
             pl.BlockSpec((sr, n_state), fix2), pl.BlockSpec((sr, n_state), fix2)]
    aliases = {}
    if y_prev is not None:
        ins.append(y_prev)
        specs.append(pl.BlockSpec(memory_space=pl.ANY))
        aliases = {len(ins) - 1: 0}
    return pl.pallas_call(
        functools.partial(_s5_kernel, nb=nb, tc=tc, aliased=y_prev is not None),
        out_shape=(jax.ShapeDtypeStruct((m, d_a), BF16),
                   jax.ShapeDtypeStruct((sr, n_state), F32),
                   jax.ShapeDtypeStruct((sr, n_state), F32)),
        grid=(n_chunks,),
        in_specs=specs,
        out_specs=(pl.BlockSpec((r, d_a), blk),
                   pl.BlockSpec((sr, n_state), fix2), pl.BlockSpec((sr, n_state), fix2)),
        scratch_shapes=[pltpu.VMEM((r, n_state), F32), pltpu.VMEM((r, n_state), F32)],
        input_output_aliases=aliases,
        compiler_params=_cp("arbitrary"),
        name="s5",
    )(*ins)


def _s5_params(lam_re, lam_im, b_re, b_im, c_re, c_im, d_skip, log_step, w_glu, b_glu):
    g, n = lam_re.shape
    gs = b_re.shape[2]
    step = jnp.exp(log_step)[:, None]
    mag = jnp.exp(lam_re * step)
    abar_r, abar_i = mag * jnp.cos(lam_im * step), mag * jnp.sin(lam_im * step)
    den = lam_re * lam_re + lam_im * lam_im
    pr, pim = abar_r - 1.0, abar_i
    zr = (pr * lam_re + pim * lam_im) / den
    zi = (pim * lam_re - pr * lam_im) / den
    bbr = zr[..., None] * b_re - zi[..., None] * b_im
    bbi = zr[..., None] * b_im + zi[..., None] * b_re
    gl = LANES // gs
    nc = g // gl

    def in_mat(bb):
        a = jnp.transpose(bb.reshape(nc, gl, n, gs), (0, 1, 3, 2)).reshape(nc, gl * gs, n)
        return jnp.tile(a, (1, 1, gl)) * _block_diag_mask(gl, gs, n)

    def out_mat(cc):
        a = jnp.transpose(cc.reshape(nc, gl, gs, n), (0, 1, 3, 2)).reshape(nc, gl * n, gs)
        return jnp.tile(a, (1, 1, gl)) * _block_diag_mask(gl, n, gs)

    return dict(
        wb=jnp.concatenate([in_mat(bbr), in_mat(bbi)], axis=-1).astype(BF16),
        wcr=out_mat(c_re).astype(BF16),
        wci=out_mat(-c_im).astype(BF16),
        ar=abar_r.reshape(1, g * n), ai=abar_i.reshape(1, g * n),
        d=d_skip.reshape(1, -1), wglu=w_glu.astype(BF16), bglu=b_glu.reshape(1, -1))


def _log_f(fz, log_lb, log1m_lb):
    b = log1m_lb + _log_sigmoid(fz)
    a = jnp.broadcast_to(log_lb, b.shape)
    return jnp.maximum(a, b) + jnp.log1p(jnp.exp(-jnp.abs(a - b)))


def _hgrn_consts(tc, nb):
    r = tc * nb
    levels = int(np.log2(tc))
    t = np.arange(r) // nb
    b = np.arange(r) % nb
    same_b = b[:, None] == b[None, :]
    tt, ts = t[:, None], t[None, :]
    expo = np.zeros((levels + 2, r, r), np.float32)
    mask = np.zeros((levels + 1, r, r), np.float32)
    for l in range(levels):
        blk, half = 2 << l, 1 << l
        split = (t // blk) * blk + half - 1
        upper = (t % blk) >= half
        sp = split[:, None]
        up = upper[:, None]
        expo[l] = same_b & np.where(up, (ts > sp) & (ts <= tt), (ts > tt) & (ts <= sp))
        mask[l] = same_b & ((tt // blk) == (ts // blk)) & up & ~upper[None, :]
    expo[levels] = same_b & (ts <= tt)
    expo[levels + 1] = same_b & (ts > tt)
    mask[levels] = np.eye(r, dtype=np.float32)
    bmask = np.zeros((nb, r, LANES), np.float32)
    for k in range(nb):
        bmask[k, b == k, :] = 1.0
    return expo, mask, bmask


def _hgrn_kernel(q_ref, fz_ref, v_ref, g_ref, llb_ref, l1m_ref, om_ref, gw_ref, expo_ref, mask_ref,
                 bm_ref, s0_ref, o_ref, st_ref, sc_ref, gs_ref, *, nb):
    levels = mask_ref.shape[0] - 1
    hd = HG_DIM

    @pl.when(pl.program_id(0) == 0)
    def _():
        st_ref[...] = s0_ref[...]

    fz = fz_ref[...]
    lf = _log_f(fz, llb_ref[...], l1m_ref[...])
    kk = om_ref[...] * jax.nn.sigmoid(-fz)
    q = q_ref[...]
    lf_hi = lf.astype(BF16)
    lf_lo = (lf - lf_hi.astype(F32)).astype(BF16)

    def expo(l):
        e = expo_ref[l]
        return _dot(e, lf_hi) + _dot(e, lf_lo)

    qb, kb = q.astype(BF16), kk.astype(BF16)
    for h in range(HG_HEADS):
        hs = slice(h * hd, (h + 1) * hd)
        sc_ref[h] = mask_ref[levels] * _dot_nt(qb[:, hs], kb[:, hs])
    for l in range(levels):
        e = jnp.exp(expo(l))
        qt = (q * e).astype(BF16)
        kt = (kk * e).astype(BF16)
        for h in range(HG_HEADS):
            hs = slice(h * hd, (h + 1) * hd)
            sc_ref[h] += mask_ref[l] * _dot_nt(qt[:, hs], kt[:, hs])

    gcum = expo(levels)
    gs_ref[...] = gcum
    qg = (q * jnp.exp(gcum)).astype(BF16)
    kend = (kk * jnp.exp(expo(levels + 1))).astype(BF16)
    v = v_ref[...]
    vb = v.astype(BF16)
    r = q.shape[0]
    gw = gw_ref[...]
    for h in range(HG_HEADS):
        hs = slice(h * hd, (h + 1) * hd)
        st = st_ref[h]
        o = _dot(sc_ref[h].astype(BF16), vb[:, hs])
        oi = _dot_nt(qg[:, hs], st.astype(BF16))
        for k in range(nb):
            o = o + bm_ref[k] * oi[:, k * hd:(k + 1) * hd]
        vcat = jnp.concatenate([(v[:, hs] * bm_ref[k]).astype(BF16) for k in range(nb)], axis=1)
        upd = _dot_tn(vcat, kend[:, hs])
        dec = jnp.concatenate(
            [jnp.broadcast_to(jnp.exp(gs_ref[r - nb + k:r - nb + k + 1, hs]), (hd, hd)) for k in range(nb)],
            axis=0)
        st_ref[h] = dec * st + upd
        o = o * lax.rsqrt(jnp.mean(o * o, axis=-1, keepdims=True) + EPS) * gw
        o_ref[:, hs] = (o * _silu(g_ref[:, hs])).astype(BF16)


def _hgrn_call(z, lbp, gnorm_w, s0t, nb, n_chunks):
    m = z.shape[0]
    d_b = HG_HEADS * HG_DIM
    r = HG_CHUNK_T * nb
    expo, mask, bmask = _hgrn_consts(HG_CHUNK_T, nb)
    fix2 = lambda i: (0, 0)
    fix3 = lambda i: (0, 0, 0)
    col = lambda c: pl.BlockSpec((r, d_b), functools.partial(lambda i, c: (i, c), c=c))
    vec = pl.BlockSpec((1, d_b), fix2)
    return pl.pallas_call(
        functools.partial(_hgrn_kernel, nb=nb),
        out_shape=(jax.ShapeDtypeStruct((m, d_b), BF16),
                   jax.ShapeDtypeStruct(s0t.shape, F32)),
        grid=(n_chunks,),
        in_specs=[col(1), col(2), col(3), col(4), vec, vec, vec,
                  pl.BlockSpec((1, HG_DIM), fix2),
                  pl.BlockSpec(expo.shape, fix3), pl.BlockSpec(mask.shape, fix3),
                  pl.BlockSpec(bmask.shape, fix3), pl.BlockSpec(s0t.shape, fix3)],
        out_specs=(pl.BlockSpec((r, d_b), lambda i: (i, 0)), pl.BlockSpec(s0t.shape, fix3)),
        scratch_shapes=[pltpu.VMEM((HG_HEADS, r, r), F32), pltpu.VMEM((r, d_b), F32)],
        compiler_params=_cp("arbitrary"),
        name="hgrn",
    )(z, z, z, z, lbp["log_lb"], lbp["log1m_lb"], lbp["one_m_lb"], gnorm_w.reshape(1, -1),
      jnp.asarray(expo, BF16), jnp.asarray(mask, F32), jnp.asarray(bmask, F32), s0t)


def _hgrn_dec_kernel(*refs, aliased):
    q_ref, fz_ref, v_ref, g_ref, llb_ref, l1m_ref, om_ref, gw_ref, fsel_ref, qsel_ref, s_ref = refs[:11]
    o_ref, so_ref = refs[13:] if aliased else refs[11:]
    hd = HG_DIM
    sb = q_ref.shape[0]
    fz = fz_ref[...]
    f = jnp.exp(_log_f(fz, llb_ref[...], l1m_ref[...]))
    kk = om_ref[...] * jax.nn.sigmoid(-fz)
    v = v_ref[...]
    gw = gw_ref[...]
    parts = [p.astype(F32) for p in _split3(f)]
    parts += [kk.astype(BF16).astype(F32), q_ref[...].astype(BF16).astype(F32)]
    n_f = 3 * sb
    zpad = jnp.zeros((hd - len(parts) * sb, hd), F32)
    wide = sb * hd
    own = (lax.broadcasted_iota(jnp.int32, (sb, wide), 0)
           == lax.shift_right_logical(lax.broadcasted_iota(jnp.int32, (sb, wide), 1), 7))
    outs = []
    for h in range(HG_HEADS):
        hs = slice(h * hd, (h + 1) * hd)
        pt = jnp.concatenate([p[:, hs] for p in parts] + [zpad], axis=0).T.astype(BF16)
        vt = jnp.where(own, jnp.concatenate([v[:, hs]] * sb, axis=1), 0.0)
        vall = jnp.concatenate([jnp.zeros((n_f, wide), F32), vt,
                                jnp.zeros((hd - n_f - sb, wide), F32)], axis=0).astype(BF16)
        prod = _dot(pt, jnp.concatenate([fsel_ref[...], vall, qsel_ref[...]], axis=1))
        o_rows = []
        for j in range(sb):
            js = slice(j * hd, (j + 1) * hd)
            s_new = prod[:, js] * s_ref[0, j, h] + prod[:, wide:2 * wide][:, js]
            so_ref[0, j, h] = s_new
            o_rows.append(jnp.sum(prod[:, 2 * wide:][:, js] * s_new, axis=0, keepdims=True))
        o = jnp.concatenate(o_rows, axis=0)
        outs.append(o * lax.rsqrt(jnp.mean(o * o, axis=-1, keepdims=True) + EPS) * gw)
    o_all = jnp.concatenate(outs, axis=1)
    o_ref[...] = (o_all * _silu(g_ref[...])).astype(BF16)


def _hgrn_dec_call(z, row0, n_rows, lbp, gnorm_w, s_all, j, y_prev, s_prev, sb=SUBLANES):
    d_b = HG_HEADS * HG_DIM
    rb0 = row0 // sb
    fix2 = lambda i: (0, 0)
    col = lambda c: pl.BlockSpec((sb, d_b), functools.partial(lambda i, c: (rb0 + i, c), c=c))
    vec = pl.BlockSpec((1, d_b), fix2)
    sblk = pl.BlockSpec((1, sb, HG_HEADS, HG_DIM, HG_DIM), lambda i: (j, i, 0, 0, 0))
    fsel = np.zeros((HG_DIM, sb * HG_DIM), np.float32)
    qsel = np.zeros((HG_DIM, sb * HG_DIM), np.float32)
    for jj in range(sb):
        fsel[jj:3 * sb:sb, jj * HG_DIM:(jj + 1) * HG_DIM] = 1.0
        qsel[4 * sb + jj, jj * HG_DIM:(jj + 1) * HG_DIM] = 1.0
    sel = pl.BlockSpec(fsel.shape, fix2)
    ins = [z, z, z, z, lbp["log_lb"], lbp["log1m_lb"], lbp["one_m_lb"], gnorm_w.reshape(1, -1),
           jnp.asarray(fsel, BF16), jnp.asarray(qsel, BF16), s_all, y_prev]
    specs = [col(1), col(2), col(3), col(4), vec, vec, vec, pl.BlockSpec((1, HG_DIM), fix2),
             sel, sel, sblk, pl.BlockSpec(memory_space=pl.ANY)]
    aliases = {11: 0}
    if s_prev is not None:
        ins.append(s_prev)
        specs.append(pl.BlockSpec(memory_space=pl.ANY))
        aliases[12] = 1
    else:
        ins.append(jnp.zeros((SUBLANES, LANES), F32))
        specs.append(pl.BlockSpec((SUBLANES, LANES), fix2))
    return pl.pallas_call(
        functools.partial(_hgrn_dec_kernel, aliased=True),
        out_shape=(jax.ShapeDtypeStruct(y_prev.shape, BF16), jax.ShapeDtypeStruct(s_all.shape, F32)),
        grid=(n_rows // sb,),
        in_specs=specs,
        out_specs=(pl.BlockSpec((sb, d_b), lambda i: (rb0 + i, 0)), sblk),
        input_output_aliases=aliases,
        compiler_params=_cp("parallel"),
        name="hgrn_dec",
    )(*ins)


def _lru_coeffs(xc, wg_ref, bga_ref, bgx_ref, lam_ref, a_ref, b_ref):
    sw = LRU_SUPER
    for j in range(xc.shape[1] // sw):
        cs = slice(j * sw, (j + 1) * sw)
        xj = xc[:, cs]
        gj = _dot(xj.astype(BF16), wg_ref[j])
        rg = jax.nn.sigmoid(gj[:, :sw] + bga_ref[:, cs])
        ig = jax.nn.sigmoid(gj[:, sw:] + bgx_ref[:, cs])
        log_a = LRU_C * rg * _log_sigmoid(lam_ref[:, cs])
        a = jnp.exp(log_a)
        a_ref[:, cs] = a
        b_ref[:, cs] = jnp.sqrt(-jnp.tanh(log_a) * (a * a + 1.0)) * ig * xj


def _lru_kernel(gb_ref, xr_ref, cw_ref, cb_ref, wg_ref, bga_ref, bgx_ref, lam_ref, c0_ref, h0_ref,
                o_ref, hl_ref, xs_ref, a_ref, b_ref, *, nb):
    r = xr_ref.shape[0]
    hdr = xs_ref.shape[0] - r
    kw = cw_ref.shape[0]

    @pl.when(pl.program_id(0) == 0)
    def _():
        xs_ref[0:hdr, :] = c0_ref[...]
        hl_ref[...] = h0_ref[...]

    xs_ref[hdr:hdr + r, :] = xr_ref[...]
    odd = xs_ref[hdr - (kw - 1) * nb:hdr - nb + r, :]
    taps = [odd[0:r], xs_ref[hdr - 2 * nb:hdr - 2 * nb + r, :], odd[2 * nb:2 * nb + r], xs_ref[hdr:hdr + r, :]]
    xc = cb_ref[...] + taps[0] * cw_ref[0:1, :]
    for j in range(1, kw):
        xc = xc + taps[j] * cw_ref[j:j + 1, :]
    xs_ref[0:hdr, :] = xs_ref[r:r + hdr, :]

    _lru_coeffs(xc, wg_ref, bga_ref, bgx_ref, lam_ref, a_ref, b_ref)

    lower = lax.broadcasted_iota(jnp.int32, (SUBLANES, xc.shape[1]), 0) < nb

    def pair(k, h):
        rows = pl.ds(pl.multiple_of(k * SUBLANES, SUBLANES), SUBLANES)
        a8 = a_ref[rows, :]
        b8 = b_ref[rows, :]
        h_a = a8 * h + b8
        h_b = a8 * pltpu.roll(h_a, nb, 0) + b8
        b_ref[rows, :] = jnp.where(lower, h_a, h_b)
        return pltpu.roll(h_b, nb, 0)

    hl_ref[...] = lax.fori_loop(0, r // SUBLANES, pair, hl_ref[...], unroll=2)
    o_ref[...] = (_gelu(gb_ref[...]) * b_ref[...]).astype(BF16)


def _lru_call(z, prm, conv0, h0, nb, n_chunks):
    m = z.shape[0]
    d_rnn = prm["lam"].shape[1]
    r = nb * LRU_CHUNK_T
    hdr = conv0.shape[0]
    fix2 = lambda i: (0, 0)
    fix3 = lambda i: (0, 0, 0)
    vec = pl.BlockSpec((1, d_rnn), fix2)
    return pl.pallas_call(
        functools.partial(_lru_kernel, nb=nb),
        out_shape=(jax.ShapeDtypeStruct((m, d_rnn), BF16),
                   jax.ShapeDtypeStruct((SUBLANES, d_rnn), F32)),
        grid=(n_chunks,),
        in_specs=[pl.BlockSpec((r, d_rnn), lambda i: (i, 0)),
                  pl.BlockSpec((r, d_rnn), lambda i: (i, 1)),
                  pl.BlockSpec(prm["cw"].shape, fix2), vec,
                  pl.BlockSpec(prm["wg"].shape, fix3), vec, vec, vec,
                  pl.BlockSpec((hdr, d_rnn), fix2), pl.BlockSpec((SUBLANES, d_rnn), fix2)],
        out_specs=(pl.BlockSpec((r, d_rnn), lambda i: (i, 0)),
                   pl.BlockSpec((SUBLANES, d_rnn), fix2)),
        scratch_shapes=[pltpu.VMEM((hdr + r, d_rnn), F32), pltpu.VMEM((r, d_rnn), F32),
                        pltpu.VMEM((r, d_rnn), F32)],
        compiler_params=_cp("arbitrary"),
        name="lru",
    )(z, z, prm["cw"], prm["cb"], prm["wg"], prm["bga"], prm["bgx"], prm["lam"], conv0, h0)


def _lru_dec_kernel(gb_ref, xr_ref, buf_ref, cw_ref, cb_ref, wg_ref, bga_ref, bgx_ref, lam_ref,
                    h0_ref, yp_ref, o_ref, h_ref, a_ref, b_ref):
    kw = cw_ref.shape[0]
    xc = cb_ref[...] + xr_ref[...] * cw_ref[kw - 1:kw, :]
    for j in range(kw - 1):
        xc = xc + buf_ref[j] * cw_ref[j:j + 1, :]
    _lru_coeffs(xc, wg_ref, bga_ref, bgx_ref, lam_ref, a_ref, b_ref)
    h = a_ref[...] * h0_ref[...] + b_ref[...]
    h_ref[...] = h
    o_ref[...] = (_gelu(gb_ref[...]) * h).astype(BF16)


def _lru_dec_call(z, row0, prm, buf, h0, y_prev):
    n = h0.shape[0]
    d_rnn = prm["lam"].shape[1]
    rb0 = row0 // n
    fix2 = lambda i: (0, 0)
    fix3 = lambda i: (0, 0, 0)
    vec = pl.BlockSpec((1, d_rnn), fix2)
    full = pl.BlockSpec((n, d_rnn), fix2)
    return pl.pallas_call(
        _lru_dec_kernel,
        out_shape=(jax.ShapeDtypeStruct(y_prev.shape, BF16), jax.ShapeDtypeStruct((n, d_rnn), F32)),
        grid=(1,),
        in_specs=[pl.BlockSpec((n, d_rnn), lambda i: (rb0, 0)),
                  pl.BlockSpec((n, d_rnn), lambda i: (rb0, 1)),
                  pl.BlockSpec(buf.shape, fix3),
                  pl.BlockSpec(prm["cw"].shape, fix2), vec,
                  pl.BlockSpec(prm["wg"].shape, fix3), vec, vec, vec, full,
                  pl.BlockSpec(memory_space=pl.ANY)],
        out_specs=(pl.BlockSpec((n, d_rnn), lambda i: (rb0, 0)), full),
        scratch_shapes=[pltpu.VMEM((n, d_rnn), F32), pltpu.VMEM((n, d_rnn), F32)],
        input_output_aliases={10: 0},
        compiler_params=_cp("arbitrary"),
        name="lru_dec",
    )(z, z, buf, prm["cw"], prm["cb"], prm["wg"], prm["bga"], prm["bgx"], prm["lam"], h0, y_prev)


def _lru_params(conv_w, conv_b, w_ga, b_ga, w_gx, b_gx, lam):
    nblk, bs = w_ga.shape[0], w_ga.shape[1]
    per = LRU_SUPER // bs
    ns = nblk // per

    def sup(w):
        return jnp.tile(w.reshape(ns, per * bs, bs), (1, 1, per)) * _block_diag_mask(per, bs, bs)

    return dict(cw=conv_w, cb=conv_b.reshape(1, -1),
                wg=jnp.concatenate([sup(w_ga), sup(w_gx)], axis=-1).astype(BF16),
                bga=b_ga.reshape(1, -1), bgx=b_gx.reshape(1, -1), lam=lam.reshape(1, -1))


def kernel(x_prompt, x_sample, state_s5_re, state_s5_im, state_hgrn, state_lru, state_conv,
           c_prompt, c_sample, norm_w, final_norm_w, w_ada, b_ada, w_ffn_gu, w_ffn_d,
           w_in_ab, s5_lam_re, s5_lam_im, s5_b_re, s5_b_im, s5_c_re, s5_c_im, s5_d, s5_log_step,
           s5_w_glu, s5_b_glu, hg_lb_logits, hg_norm_w, w_out_ab, w_in_c, conv_w, conv_b,
           w_gate_a, b_gate_a, w_gate_x, b_gate_x, lru_lambda, w_out_c):
    bsz, seq, d = x_prompt.shape
    nsm = x_sample.shape[0]
    depth = w_ada.shape[0]
    n_ab, n_c = w_in_ab.shape[0], w_in_c.shape[0]
    g_a, n_a = s5_lam_re.shape[1], s5_lam_re.shape[2]
    n_state = g_a * n_a
    d_rnn = lru_lambda.shape[1]
    kw = conv_w.shape[1]
    mp_rows = seq * bsz
    assert 2 * bsz == SUBLANES and mp_rows % nsm == 0 and (mp_rows + nsm) % ROW_TILE == 0

    s5p = [_s5_params(s5_lam_re[j], s5_lam_im[j], s5_b_re[j], s5_b_im[j], s5_c_re[j], s5_c_im[j],
                      s5_d[j], s5_log_step[j], s5_w_glu[j], s5_b_glu[j]) for j in range(n_ab)]
    lb_all = jnp.cumsum(jax.nn.softmax(hg_lb_logits.astype(F32), axis=0), axis=0)
    lb_all = lb_all - lb_all[0:1]
    lbp = [dict(log_lb=jnp.log(lb_all[j]).reshape(1, -1),
                log1m_lb=jnp.log1p(-lb_all[j]).reshape(1, -1),
                one_m_lb=(1.0 - lb_all[j]).reshape(1, -1)) for j in range(n_ab)]
    lrup = [_lru_params(conv_w[j], conv_b[j], w_gate_a[j], b_gate_a[j], w_gate_x[j], b_gate_x[j],
                        lru_lambda[j]) for j in range(n_c)]
    w_out_ab_bf = w_out_ab.astype(BF16)
    w_out_c_bf = w_out_c.astype(BF16)
    norm_w = norm_w.reshape(depth * 3, 1, d)

    c_all = jnp.concatenate([c_sample] + [c_prompt] * (SUBLANES // bsz), axis=0)
    mods = _ada_call(c_all, w_ada, b_ada)

    x = _pack_call(x_prompt, x_sample.reshape(nsm, d), FINAL_TILE_T)
    zeros_state = jnp.zeros((SUBLANES, n_state), F32)
    s5r_p, s5i_p, hg_p, lru_p, conv_p = [], [], [], [], []
    s5r_s, s5i_s, lru_s, conv_s = [], [], [], []
    hg_s = None
    for l in range(depth):
        j = l // 2
        x = _ffn_call(x, norm_w, mods, nsm, w_ffn_gu, w_ffn_d, l, 0, 0, ROW_TILE, 256)
        if l % 2 == 0:
            z = _inproj_call(x, norm_w, mods, nsm, w_in_ab, l, j, IN_ROW_TILE, IN_COL_TILE)
            ya, hr, hi = _s5_call(z, s5p[j], zeros_state, zeros_state, bsz, S5_CHUNK_T,
                                  0, seq // S5_CHUNK_T)
            ya, hrs, his = _s5_call(z, s5p[j], state_s5_re[j].reshape(nsm, n_state),
                                    state_s5_im[j].reshape(nsm, n_state), nsm, 1,
                                    mp_rows // nsm, 1, y_prev=ya)
            yb, hg = _hgrn_call(z, lbp[j], hg_norm_w[j],
                                jnp.zeros((HG_HEADS, bsz * HG_DIM, HG_DIM), F32), bsz, seq // HG_CHUNK_T)
            yb, hg_s = _hgrn_dec_call(z, mp_rows, nsm, lbp[j], hg_norm_w[j], state_hgrn, j, yb, hg_s)
            s5r_p.append(hr[:bsz].reshape(bsz, g_a, n_a))
            s5i_p.append(hi[:bsz].reshape(bsz, g_a, n_a))
            s5r_s.append(hrs.reshape(nsm, g_a, n_a))
            s5i_s.append(his.reshape(nsm, g_a, n_a))
            hg_p.append(jnp.transpose(hg.reshape(HG_HEADS, bsz, HG_DIM, HG_DIM), (1, 0, 3, 2)))
            x = _outproj_call(x, mods, nsm, l, [ya, yb], w_out_ab_bf, j, OUT_ROW_TILE)
        else:
            z = _inproj_call(x, norm_w, mods, nsm, w_in_c, l, j, IN_ROW_TILE, IN_COL_TILE)
            y, hl = _lru_call(z, lrup[j], jnp.zeros((4 * bsz, d_rnn), F32),
                              jnp.zeros((SUBLANES, d_rnn), F32), bsz, seq // LRU_CHUNK_T)
            y, hls = _lru_dec_call(z, mp_rows, lrup[j], jnp.transpose(state_conv[j], (1, 0, 2)),
                                   state_lru[j], y)
            lru_p.append(hl[:bsz])
            lru_s.append(hls)
            tail = z[mp_rows - (kw - 1) * bsz:mp_rows, d_rnn:]
            conv_p.append(jnp.transpose(tail.reshape(kw - 1, bsz, d_rnn), (1, 0, 2)))
            conv_s.append(jnp.concatenate([state_conv[j][:, 1:], z[mp_rows:, None, d_rnn:]], axis=1))
            x = _outproj_call(x, mods, nsm, l, [y], w_out_c_bf, j, OUT_ROW_TILE)
        x = _ffn_call(x, norm_w, mods, nsm, w_ffn_gu, w_ffn_d, l, 1, 2, ROW_TILE, 256)
    y_prompt, y_sample = _final_norm_call(x, final_norm_w.reshape(1, -1), bsz, seq, nsm, FINAL_TILE_T)
    return (y_prompt, y_sample.reshape(nsm, 1, d), jnp.stack(s5r_p), jnp.stack(s5i_p), jnp.stack(hg_p),
            jnp.stack(lru_p), jnp.stack(conv_p),
            jnp.stack(s5r_s), jnp.stack(s5i_s), hg_s, jnp.stack(lru_s), jnp.stack(conv_s))
```

```python
import functools

import numpy as np
import jax
import jax.numpy as jnp
from jax import lax
from jax.experimental import pallas as pl
from jax.experimental.pallas import tpu as pltpu

F32 = jnp.float32
BF16 = jnp.bfloat16
EPS = 1e-6
LRU_C = 8.0

VMEM_LIMIT_BYTES = 60 * 1024 * 1024
SUBLANES = 8
LANES = 128
BF16_ROWS = 16

S5_COLS = 8
S5_CHUNK_T = 128
HG_HEADS = 8
HG_DIM = 128
HG_CHUNK_T = 64
LRU_CHUNK_T = 64
LRU_SUPER = 640
N_MOD = 9
ROW_TILE = 1040
OUT_ROW_TILE = 640
IN_ROW_TILE = 1664
IN_COL_TILE = 512
FINAL_TILE_T = 256
MXU_COLS = 256


def _cp(*sem):
    return pltpu.CompilerParams(dimension_semantics=sem, vmem_limit_bytes=VMEM_LIMIT_BYTES)


def _dot(a, b):
    return jnp.dot(a, b, preferred_element_type=F32)


def _dot_nt(a, b):
    return lax.dot_general(a, b, (((1,), (1,)), ((), ())), preferred_element_type=F32)


def _dot_tn(a, b):
    return lax.dot_general(a, b, (((0,), (0,)), ((), ())), preferred_element_type=F32)


def _silu(x):
    return x * jax.nn.sigmoid(x)


def _gelu(x):
    return jax.nn.gelu(x, approximate=True)


def _log_sigmoid(x):
    return jnp.minimum(x, 0.0) - jnp.log1p(jnp.exp(-jnp.abs(x)))


def _block_diag_mask(nblk, rows, cols):
    r = np.arange(nblk * rows)[:, None] // rows
    c = np.arange(nblk * cols)[None, :] // cols
    return jnp.asarray(r == c, F32)


def _split3(x):
    hi = x.astype(BF16)
    r1 = x - hi.astype(F32)
    mid = r1.astype(BF16)
    lo = (r1 - mid.astype(F32)).astype(BF16)
    return hi, mid, lo


def _fma_rows(y, mul, add):
    rm = mul.shape[0]
    r, d = y.shape
    if rm == r:
        return y * mul + add
    y3 = y.reshape(r // rm, rm, d)
    return (y3 * mul[None] + add[None]).reshape(r, d)


def _mul_rows(y, mul):
    rm = mul.shape[0]
    r, d = y.shape
    if rm == r:
        return y * mul
    return (y.reshape(r // rm, rm, d) * mul[None]).reshape(r, d)


def _norm_mod_rows(x_ref, nw_ref, sh_ref, sc_ref, h_ref, row0, nrows):
    slab = BF16_ROWS
    per_row = sh_ref.shape[0] > SUBLANES
    nw = nw_ref[...]

    def body(s, carry):
        r0 = pl.multiple_of(row0 + s * slab, slab)
        x = x_ref[pl.ds(r0, slab), :]
        ms = jnp.mean(x * x, axis=-1, keepdims=True)
        y = x * lax.rsqrt(ms + EPS)
        if per_row:
            m0 = pl.multiple_of(s * slab, slab)
            h = y * nw * (1.0 + sc_ref[pl.ds(m0, slab), :]) + sh_ref[pl.ds(m0, slab), :]
        else:
            h = _fma_rows(y, nw * (1.0 + sc_ref[...]), sh_ref[...])
        h_ref[pl.ds(r0, slab), :] = h.astype(BF16)
        return carry

    lax.fori_loop(0, nrows // slab, body, 0, unroll=8)


def _residual_rows(o_ref, x_ref, acc, gt_ref, gts_ref, scale):
    tm = o_ref.shape[0]
    ns = gts_ref.shape[0]
    np_ = tm - ns
    is_last = pl.program_id(0) == pl.num_programs(0) - 1
    gp = scale * gt_ref[...]
    o_ref[0:np_, :] = x_ref[0:np_, :] + _mul_rows(acc(0, np_), gp)

    @pl.when(jnp.logical_not(is_last))
    def _():
        o_ref[np_:tm, :] = x_ref[np_:tm, :] + _mul_rows(acc(np_, tm), gp)

    @pl.when(is_last)
    def _():
        o_ref[np_:tm, :] = x_ref[np_:tm, :] + acc(np_, tm) * (scale * gts_ref[...])


def _ada_kernel(c_ref, w_ref, b_ref, o_ref):
    sc = _silu(c_ref[...]).astype(BF16)
    o_ref[0] = _dot(sc, w_ref[0].astype(BF16)) + b_ref[0]


def _ada_call(c_all, w_ada, b_ada, tn=1024):
    depth, d, n = w_ada.shape
    r = c_all.shape[0]
    return pl.pallas_call(
        _ada_kernel,
        out_shape=jax.ShapeDtypeStruct((depth, r, n), F32),
        grid=(depth, n // tn),
        in_specs=[pl.BlockSpec((r, d), lambda l, j: (0, 0)),
                  pl.BlockSpec((1, d, tn), lambda l, j: (l, 0, j)),
                  pl.BlockSpec((1, 1, tn), lambda l, j: (l, 0, j))],
        out_specs=pl.BlockSpec((1, r, tn), lambda l, j: (l, 0, j)),
        compiler_params=_cp("parallel", "parallel"),
        name="ada",
    )(c_all, w_ada, b_ada.reshape(depth, 1, n))


def _ffn_kernel(x_ref, nw_ref, sh_ref, sc_ref, gt_ref, shs_ref, scs_ref, gts_ref,
                wg_ref, wu_ref, wd_ref, o_ref, h_ref):
    i, f = pl.program_id(0), pl.program_id(1)
    nw_ref, sh_ref, sc_ref, gt_ref, shs_ref, scs_ref, gts_ref = (
        r.at[0] for r in (nw_ref, sh_ref, sc_ref, gt_ref, shs_ref, scs_ref, gts_ref))
    tm = x_ref.shape[0]
    ns = shs_ref.shape[0]
    is_last = i == pl.num_programs(0) - 1

    @pl.when(f == 0)
    def _():
        _norm_mod_rows(x_ref, nw_ref, sh_ref, sc_ref, h_ref, 0, tm)
        o_ref[...] = jnp.zeros_like(o_ref)

    @pl.when(jnp.logical_and(f == 0, is_last))
    def _():
        _norm_mod_rows(x_ref, nw_ref, shs_ref, scs_ref, h_ref, tm - ns, ns)

    h = h_ref[...]
    g = _dot(h, wg_ref[0, 0].astype(BF16))
    u = _dot(h, wu_ref[0, 0].astype(BF16))
    a = (_silu(g) * u).astype(BF16)
    o_ref[...] += _dot(a, wd_ref[0, 0].astype(BF16))

    @pl.when(f == pl.num_programs(1) - 1)
    def _():
        _residual_rows(o_ref, x_ref, lambda a0, a1: o_ref[a0:a1, :], gt_ref, gts_ref, 0.5)


def _mod_specs(mods, ns, d, l, ks):
    def pat(k):
        return pl.BlockSpec((1, SUBLANES, d), lambda *g: (l, ns // SUBLANES, k))

    def smp(k):
        return pl.BlockSpec((1, ns, d), lambda *g: (l, 0, k))

    return [pat(k) for k in ks] + [smp(k) for k in ks], [mods] * (2 * len(ks))


def _ffn_call(x, norm_w, mods, ns, w_gu, w_d, l, s, sub, tm, tf):
    m, d = x.shape
    ff = w_d.shape[2]
    nf = ff // tf
    row = lambda i, f: (i, 0)
    mspecs, margs = _mod_specs(mods, ns, d, l, (3 * sub, 3 * sub + 1, 3 * sub + 2))
    return pl.pallas_call(
        _ffn_kernel,
        out_shape=jax.ShapeDtypeStruct((m, d), F32),
        grid=(m // tm, nf),
        in_specs=[pl.BlockSpec((tm, d), row), pl.BlockSpec((1, 1, d), lambda i, f: (3 * l + sub, 0, 0))]
                 + mspecs +
                 [pl.BlockSpec((1, 1, d, tf), lambda i, f: (l, s, 0, f)),
                  pl.BlockSpec((1, 1, d, tf), lambda i, f: (l, s, 0, f + nf)),
                  pl.BlockSpec((1, 1, tf, d), lambda i, f: (l, s, f, 0))],
        out_specs=pl.BlockSpec((tm, d), row),
        scratch_shapes=[pltpu.VMEM((tm, d), BF16)],
        compiler_params=_cp("parallel", "arbitrary"),
        name="ffn",
    )(x, norm_w, *margs, w_gu, w_gu, w_d)


def _inproj_kernel(x_ref, nw_ref, sh_ref, sc_ref, shs_ref, scs_ref, w_ref, o_ref, h_ref):
    i, k = pl.program_id(0), pl.program_id(1)
    nw_ref, sh_ref, sc_ref, shs_ref, scs_ref = (
        r.at[0] for r in (nw_ref, sh_ref, sc_ref, shs_ref, scs_ref))
    tm = x_ref.shape[0]
    ns = shs_ref.shape[0]

    @pl.when(k == 0)
    def _():
        _norm_mod_rows(x_ref, nw_ref, sh_ref, sc_ref, h_ref, 0, tm)

    @pl.when(jnp.logical_and(k == 0, i == pl.num_programs(0) - 1))
    def _():
        _norm_mod_rows(x_ref, nw_ref, shs_ref, scs_ref, h_ref, tm - ns, ns)

    h = h_ref[...]
    for c in range(0, o_ref.shape[1], MXU_COLS):
        o_ref[:, c:c + MXU_COLS] = _dot(h, w_ref[0, :, c:c + MXU_COLS].astype(BF16))


def _inproj_call(x, norm_w, mods, ns, w, l, j, tm, tn):
    m, d = x.shape
    n = w.shape[2]
    mspecs, margs = _mod_specs(mods, ns, d, l, (3, 4))
    return pl.pallas_call(
        _inproj_kernel,
        out_shape=jax.ShapeDtypeStruct((m, n), F32),
        grid=(m // tm, n // tn),
        in_specs=[pl.BlockSpec((tm, d), lambda i, k: (i, 0)),
                  pl.BlockSpec((1, 1, d), lambda i, k: (3 * l + 1, 0, 0))]
                 + mspecs + [pl.BlockSpec((1, d, tn), lambda i, k: (j, 0, k))],
        out_specs=pl.BlockSpec((tm, tn), lambda i, k: (i, k)),
        scratch_shapes=[pltpu.VMEM((tm, d), BF16)],
        compiler_params=_cp("parallel", "arbitrary"),
        name="inproj",
    )(x, norm_w, *margs, w)


def _outproj_kernel(*refs, n_in):
    x_ref, gt_ref, gts_ref = refs[0], refs[1].at[0], refs[2].at[0]
    a_refs = refs[3:3 + n_in]
    w_refs = refs[3 + n_in:3 + 2 * n_in]
    o_ref = refs[3 + 2 * n_in]
    acc = _dot(a_refs[0][...], w_refs[0][0])
    for a_ref, w_ref in zip(a_refs[1:], w_refs[1:]):
        acc = acc + _dot(a_ref[...], w_ref[0])
    _residual_rows(o_ref, x_ref, lambda a0, a1: acc[a0:a1, :], gt_ref, gts_ref, 1.0)


def _outproj_call(x, mods, ns, l, acts, w_bf, j, tm):
    m, d = x.shape
    n_in = len(acts)
    ka = acts[0].shape[1]
    row = lambda i: (i, 0)
    mspecs, margs = _mod_specs(mods, ns, d, l, (5,))
    in_specs = [pl.BlockSpec((tm, d), row)] + mspecs
    in_specs += [pl.BlockSpec((tm, ka), row) for _ in acts]
    in_specs += [pl.BlockSpec((1, ka, d), functools.partial(lambda i, p: (j, p, 0), p=p)) for p in range(n_in)]
    return pl.pallas_call(
        functools.partial(_outproj_kernel, n_in=n_in),
        out_shape=jax.ShapeDtypeStruct((m, d), F32),
        grid=(m // tm,),
        in_specs=in_specs,
        out_specs=pl.BlockSpec((tm, d), row),
        compiler_params=_cp("parallel"),
        name="outproj",
    )(x, *margs, *acts, *([w_bf] * n_in))


def _pack_kernel(xp_ref, xs_ref, o_ref, slab_ref, *, n_tiles):
    i = pl.program_id(0)
    nb, tt, d = xp_ref.shape

    @pl.when(i < n_tiles)
    def _():
        for c in range(d // LANES):
            cs = slice(c * LANES, (c + 1) * LANES)
            for b in range(nb):
                slab_ref[c, pl.ds(b, tt, stride=nb), :] = xp_ref[b, :, cs]
            o_ref[:, cs] = slab_ref[c]

    @pl.when(i == n_tiles)
    def _():
        o_ref[0:xs_ref.shape[0], :] = xs_ref[...]


def _pack_call(x_prompt, xs, tt):
    nb, seq, d = x_prompt.shape
    ns = xs.shape[0]
    n_tiles = seq // tt
    return pl.pallas_call(
        functools.partial(_pack_kernel, n_tiles=n_tiles),
        out_shape=jax.ShapeDtypeStruct((seq * nb + ns, d), F32),
        grid=(n_tiles + 1,),
        in_specs=[pl.BlockSpec((nb, tt, d), lambda i: (0, jnp.minimum(i, n_tiles - 1), 0)),
                  pl.BlockSpec((ns, d), lambda i: (0, 0))],
        out_specs=pl.BlockSpec((tt * nb, d), lambda i: (i, 0)),
        scratch_shapes=[pltpu.VMEM((d // LANES, tt * nb, LANES), F32)],
        compiler_params=_cp("arbitrary"),
        name="pack",
    )(x_prompt, xs)


def _final_norm_kernel(x_ref, w_ref, yp_ref, ys_ref, slab_ref, *, n_tiles):
    i = pl.program_id(0)
    nb, tt, d = yp_ref.shape
    w = w_ref[...]

    def inv_rms(x):
        return lax.rsqrt(jnp.mean(x * x, axis=-1, keepdims=True) + EPS)

    @pl.when(i < n_tiles)
    def _():
        inv = inv_rms(x_ref[...])
        for c in range(d // LANES):
            cs = slice(c * LANES, (c + 1) * LANES)
            slab_ref[c] = x_ref[:, cs] * inv * w[:, cs]
            for b in range(nb):
                yp_ref[b, :, cs] = slab_ref[c, pl.ds(b, tt, stride=nb), :]

    @pl.when(i == n_tiles)
    def _():
        x = x_ref[0:ys_ref.shape[0], :]
        ys_ref[...] = x * inv_rms(x) * w


def _final_norm_call(x, w, nb, seq, ns, tt):
    d = x.shape[1]
    n_tiles = seq // tt
    return pl.pallas_call(
        functools.partial(_final_norm_kernel, n_tiles=n_tiles),
        out_shape=(jax.ShapeDtypeStruct((nb, seq, d), F32), jax.ShapeDtypeStruct((ns, d), F32)),
        grid=(n_tiles + 1,),
        in_specs=[pl.BlockSpec((tt * nb, d), lambda i: (i, 0)),
                  pl.BlockSpec((1, d), lambda i: (0, 0))],
        out_specs=(pl.BlockSpec((nb, tt, d), lambda i: (0, jnp.minimum(i, n_tiles - 1), 0)),
                   pl.BlockSpec((ns, d), lambda i: (0, 0))),
        scratch_shapes=[pltpu.VMEM((d // LANES, tt * nb, LANES), F32)],
        compiler_params=_cp("arbitrary"),
        name="final_norm",
    )(x, w)


def _s5_kernel(*refs, nb, tc, aliased):
    (u_ref, wb_ref, ar_ref, ai_ref, wcr_ref, wci_ref, d_ref, wglu_ref, bglu_ref,
     h0r_ref, h0i_ref) = refs[:11]
    o_ref, hr_ref, hi_ref, xr_ref, xi_ref = refs[12:] if aliased else refs[11:]

    @pl.when(pl.program_id(0) == 0)
    def _():
        hr_ref[...] = h0r_ref[...]
        hi_ref[...] = h0i_ref[...]

    u = u_ref[...]
    ub = u.astype(BF16)
    n_state = xr_ref.shape[1]
    cw = n_state // S5_COLS
    for c in range(S5_COLS):
        xc = _dot(ub[:, c * LANES:(c + 1) * LANES], wb_ref[c])
        xr_ref[:, c * cw:(c + 1) * cw] = xc[:, :cw]
        xi_ref[:, c * cw:(c + 1) * cw] = xc[:, cw:]

    lane_group = 1024
    for lo_ in range(0, n_state, lane_group):
        ls = pl.ds(lo_, lane_group)
        ar = ar_ref[:, ls]
        ai = ai_ref[:, ls]
        if tc == 1:
            hr, hi = hr_ref[:, ls], hi_ref[:, ls]
            nr = ar * hr - ai * hi + xr_ref[:, ls]
            ni = ar * hi + ai * hr + xi_ref[:, ls]
            xr_ref[:, ls] = nr
            xi_ref[:, ls] = ni
        else:
            lower = lax.broadcasted_iota(jnp.int32, (SUBLANES, lane_group), 0) < nb

            def pair(k, carry, ls=ls, ar=ar, ai=ai, lower=lower):
                hr, hi = carry
                rows = pl.ds(pl.multiple_of(k * SUBLANES, SUBLANES), SUBLANES)
                x_r, x_i = xr_ref[rows, ls], xi_ref[rows, ls]
                ar_ = ar * hr - ai * hi + x_r
                ai_ = ar * hi + ai * hr + x_i
                sr, si = pltpu.roll(ar_, nb, 0), pltpu.roll(ai_, nb, 0)
                br_ = ar * sr - ai * si + x_r
                bi_ = ar * si + ai * sr + x_i
                xr_ref[rows, ls] = jnp.where(lower, ar_, br_)
                xi_ref[rows, ls] = jnp.where(lower, ai_, bi_)
                return pltpu.roll(br_, nb, 0), pltpu.roll(bi_, nb, 0)

            nr, ni = lax.fori_loop(0, tc // 2, pair, (hr_ref[:, ls], hi_ref[:, ls]), unroll=2)
        hr_ref[:, ls] = nr
        hi_ref[:, ls] = ni

    ys = []
    for c in range(S5_COLS):
        cs = pl.ds(c * cw, cw)
        ys.append(_dot(xr_ref[:, cs].astype(BF16), wcr_ref[c])
                  + _dot(xi_ref[:, cs].astype(BF16), wci_ref[c]))
    y = jnp.concatenate(ys, axis=1) + d_ref[...] * u
    y = _gelu(y)
    z = _dot(y.astype(BF16), wglu_ref[...]) + bglu_ref[...]
    o_ref[...] = (y * jax.nn.sigmoid(z)).astype(BF16)


def _s5_call(z, prm, h0r, h0i, nb, tc, row_block0, n_chunks, y_prev=None):
    m = z.shape[0]
    d_a = prm["d"].shape[1]
    n_state = prm["ar"].shape[1]
    r = nb * tc
    sr = h0r.shape[0]
    fix2 = lambda i: (0, 0)
    fix3 = lambda i: (0, 0, 0)
    blk = lambda i: (row_block0 + i, 0)
    ins = [z, prm["wb"], prm["ar"], prm["ai"], prm["wcr"], prm["wci"], prm["d"], prm["wglu"],
           prm["bglu"], h0r, h0i]
    specs = [pl.BlockSpec((r, d_a), blk),
             pl.BlockSpec(prm["wb"].shape, fix3),
             pl.BlockSpec((1, n_state), fix2), pl.BlockSpec((1, n_state), fix2),
             pl.BlockSpec(prm["wcr"].shape, fix3), pl.BlockSpec(prm["wci"].shape, fix3),
             pl.BlockSpec((1, d_a), fix2), pl.BlockSpec((d_a, d_a), fix2),
             pl.BlockSpec((1, d_a), fix2),
             pl.BlockSpec((sr, n_state), fix2), pl.BlockSpec((sr, n_state), fix2)]
    aliases = {}
    if y_prev is not None:
        ins.append(y_prev)
        specs.append(pl.BlockSpec(memory_space=pl.ANY))
        aliases = {len(ins) - 1: 0}
    return pl.pallas_call(
        functools.partial(_s5_kernel, nb=nb, tc=tc, aliased=y_prev is not None),
        out_shape=(jax.ShapeDtypeStruct((m, d_a), BF16),
                   jax.ShapeDtypeStruct((sr, n_state), F32),
                   jax.ShapeDtypeStruct((sr, n_state), F32)),
        grid=(n_chunks,),
        in_specs=specs,
        out_specs=(pl.BlockSpec((r, d_a), blk),
                   pl.BlockSpec((sr, n_state), fix2), pl.BlockSpec((sr, n_state), fix2)),
        scratch_shapes=[pltpu.VMEM((r, n_state), F32), pltpu.VMEM((r, n_state), F32)],
        input_output_aliases=aliases,
        compiler_params=_cp("arbitrary"),
        name="s5",
    )(*ins)


def _s5_params(lam_re, lam_im, b_re, b_im, c_re, c_im, d_skip, log_step, w_glu, b_glu):
    g, n = lam_re.shape
    gs = b_re.shape[2]
    step = jnp.exp(log_step)[:, None]
    mag = jnp.exp(lam_re * step)
    abar_r, abar_i = mag * jnp.cos(lam_im * step), mag * jnp.sin(lam_im * step)
    den = lam_re * lam_re + lam_im * lam_im
    pr, pim = abar_r - 1.0, abar_i
    zr = (pr * lam_re + pim * lam_im) / den
    zi = (pim * lam_re - pr * lam_im) / den
    bbr = zr[..., None] * b_re - zi[..., None] * b_im
    bbi = zr[..., None] * b_im + zi[..., None] * b_re
    gl = LANES // gs
    nc = g // gl

    def in_mat(bb):
        a = jnp.transpose(bb.reshape(nc, gl, n, gs), (0, 1, 3, 2)).reshape(nc, gl * gs, n)
        return jnp.tile(a, (1, 1, gl)) * _block_diag_mask(gl, gs, n)

    def out_mat(cc):
        a = jnp.transpose(cc.reshape(nc, gl, gs, n), (0, 1, 3, 2)).reshape(nc, gl * n, gs)
        return jnp.tile(a, (1, 1, gl)) * _block_diag_mask(gl, n, gs)

    return dict(
        wb=jnp.concatenate([in_mat(bbr), in_mat(bbi)], axis=-1).astype(BF16),
        wcr=out_mat(c_re).astype(BF16),
        wci=out_mat(-c_im).astype(BF16),
        ar=abar_r.reshape(1, g * n), ai=abar_i.reshape(1, g * n),
        d=d_skip.reshape(1, -1), wglu=w_glu.astype(BF16), bglu=b_glu.reshape(1, -1))


def _log_f(fz, log_lb, log1m_lb):
    b = log1m_lb + _log_sigmoid(fz)
    a = jnp.broadcast_to(log_lb, b.shape)
    return jnp.maximum(a, b) + jnp.log1p(jnp.exp(-jnp.abs(a - b)))


def _hgrn_consts(tc, nb):
    r = tc * nb
    levels = int(np.log2(tc))
    t = np.arange(r) // nb
    b = np.arange(r) % nb
    same_b = b[:, None] == b[None, :]
    tt, ts = t[:, None], t[None, :]
    mask = np.zeros((levels + 1, r, r), np.float32)
    for l in range(levels):
        blk, half = 2 << l, 1 << l
        upper = (t % blk) >= half
        mask[l] = same_b & ((tt // blk) == (ts // blk)) & upper[:, None] & ~upper[None, :]
    expo = (same_b & (ts <= tt)).astype(np.float32)
    mask[levels] = np.eye(r, dtype=np.float32)
    bmask = np.zeros((nb, r, LANES), np.float32)
    for k in range(nb):
        bmask[k, b == k, :] = 1.0
    return expo, mask, bmask


def _hgrn_kernel(q_ref, fz_ref, v_ref, g_ref, llb_ref, l1m_ref, om_ref, gw_ref, expo_ref, mask_ref,
                 bm_ref, s0_ref, o_ref, st_ref, sc_ref, gs_ref, lf_ref, *, nb):
    levels = mask_ref.shape[0] - 1
    hd = HG_DIM
    r = q_ref.shape[0]

    @pl.when(pl.program_id(0) == 0)
    def _():
        st_ref[...] = s0_ref[...]
        sc_ref[...] = jnp.zeros_like(sc_ref)

    lf = _log_f(fz_ref[...], llb_ref[...], l1m_ref[...])
    lf_ref[...] = lf
    lf_hi = lf.astype(BF16)
    lf_lo = (lf - lf_hi.astype(F32)).astype(BF16)
    gs_ref[...] = _dot(expo_ref[...], lf_hi) + _dot(expo_ref[...], lf_lo)

    tiles = r // SUBLANES
    lower = lax.broadcasted_iota(jnp.int32, (1, SUBLANES, 1), 1) < nb

    def both_halves(t):
        return jnp.where(lower, pltpu.roll(t, nb, 1), t)

    gw = gw_ref[...]
    for h in range(HG_HEADS):
        hs = slice(h * hd, (h + 1) * hd)
        q = q_ref[:, hs]
        kk = om_ref[:, hs] * jax.nn.sigmoid(-fz_ref[:, hs])
        gcum = gs_ref[:, hs]
        g3 = gcum.reshape(tiles, SUBLANES, hd)

        def level_expo(l, g3=g3, hs=hs):
            if l == 0:
                return jnp.where(lower, 0.0, lf_ref[:, hs].reshape(tiles, SUBLANES, hd)).reshape(r, hd)
            span = 1 << l
            g4 = g3.reshape(tiles // span, span, SUBLANES, hd)
            split = both_halves(g4[:, span // 2 - 1])
            return (-jnp.abs(g4 - split[:, None])).reshape(r, hd)

        halves = (slice(0, r // 2), slice(r // 2, r))
        qb, kb = q.astype(BF16), kk.astype(BF16)
        acc = [mask_ref[levels, hv, hv] * _dot_nt(qb[hv], kb[hv]) for hv in halves]
        for l in range(levels - 1):
            e = jnp.exp(level_expo(l))
            qt, kt = (q * e).astype(BF16), (kk * e).astype(BF16)
            acc = [a + mask_ref[l, hv, hv] * _dot_nt(qt[hv], kt[hv]) for a, hv in zip(acc, halves)]
        e = jnp.exp(level_expo(levels - 1))
        lo_, hi_ = halves
        sc_ref[lo_, lo_] = acc[0]
        sc_ref[hi_, hi_] = acc[1]
        sc_ref[hi_, lo_] = mask_ref[levels - 1, hi_, lo_] * _dot_nt((q[hi_] * e[hi_]).astype(BF16),
                                                                   (kk[lo_] * e[lo_]).astype(BF16))

        qg = (q * jnp.exp(gcum)).astype(BF16)
        to_end = (both_halves(g3[tiles - 1:tiles]) - g3).reshape(r, hd)
        kend = (kk * jnp.exp(to_end)).astype(BF16)
        v = v_ref[:, hs]
        st = st_ref[h]
        o = _dot(sc_ref[...].astype(BF16), v.astype(BF16))
        oi = _dot_nt(qg, st.astype(BF16))
        for k in range(nb):
            o = o + bm_ref[k] * oi[:, k * hd:(k + 1) * hd]
        vcat = jnp.concatenate([(v * bm_ref[k]).astype(BF16) for k in range(nb)], axis=1)
        upd = _dot_tn(vcat, kend)
        dec = jnp.concatenate(
            [jnp.broadcast_to(jnp.exp(gs_ref[r - nb + k:r - nb + k + 1, hs]), (hd, hd)) for k in range(nb)],
            axis=0)
        st_ref[h] = dec * st + upd
        o = o * lax.rsqrt(jnp.mean(o * o, axis=-1, keepdims=True) + EPS) * gw
        o_ref[:, hs] = (o * _silu(g_ref[:, hs])).astype(BF16)


def _hgrn_call(z, lbp, gnorm_w, s0t, nb, n_chunks):
    m = z.shape[0]
    d_b = HG_HEADS * HG_DIM
    r = HG_CHUNK_T * nb
    expo, mask, bmask = _hgrn_consts(HG_CHUNK_T, nb)
    fix2 = lambda i: (0, 0)
    fix3 = lambda i: (0, 0, 0)
    col = lambda c: pl.BlockSpec((r, d_b), functools.partial(lambda i, c: (i, c), c=c))
    vec = pl.BlockSpec((1, d_b), fix2)
    return pl.pallas_call(
        functools.partial(_hgrn_kernel, nb=nb),
        out_shape=(jax.ShapeDtypeStruct((m, d_b), BF16),
                   jax.ShapeDtypeStruct(s0t.shape, F32)),
        grid=(n_chunks,),
        in_specs=[col(1), col(2), col(3), col(4), vec, vec, vec,
                  pl.BlockSpec((1, HG_DIM), fix2),
                  pl.BlockSpec(expo.shape, fix2), pl.BlockSpec(mask.shape, fix3),
                  pl.BlockSpec(bmask.shape, fix3), pl.BlockSpec(s0t.shape, fix3)],
        out_specs=(pl.BlockSpec((r, d_b), lambda i: (i, 0)), pl.BlockSpec(s0t.shape, fix3)),
        scratch_shapes=[pltpu.VMEM((r, r), F32), pltpu.VMEM((r, d_b), F32), pltpu.VMEM((r, d_b), F32)],
        compiler_params=_cp("arbitrary"),
        name="hgrn",
    )(z, z, z, z, lbp["log_lb"], lbp["log1m_lb"], lbp["one_m_lb"], gnorm_w.reshape(1, -1),
      jnp.asarray(expo, BF16), jnp.asarray(mask, F32), jnp.asarray(bmask, F32), s0t)


def _hgrn_dec_kernel(*refs, aliased):
    q_ref, fz_ref, v_ref, g_ref, llb_ref, l1m_ref, om_ref, gw_ref, fsel_ref, qsel_ref, s_ref = refs[:11]
    o_ref, so_ref = refs[13:] if aliased else refs[11:]
    hd = HG_DIM
    sb = q_ref.shape[0]
    fz = fz_ref[...]
    f = jnp.exp(_log_f(fz, llb_ref[...], l1m_ref[...]))
    kk = om_ref[...] * jax.nn.sigmoid(-fz)
    v = v_ref[...]
    gw = gw_ref[...]
    parts = [p.astype(F32) for p in _split3(f)]
    parts += [kk.astype(BF16).astype(F32), q_ref[...].astype(BF16).astype(F32)]
    n_f = 3 * sb
    zpad = jnp.zeros((hd - len(parts) * sb, hd), F32)
    wide = sb * hd
    own = (lax.broadcasted_iota(jnp.int32, (sb, wide), 0)
           == lax.shift_right_logical(lax.broadcasted_iota(jnp.int32, (sb, wide), 1), 7))
    outs = []
    for h in range(HG_HEADS):
        hs = slice(h * hd, (h + 1) * hd)
        pt = jnp.concatenate([p[:, hs] for p in parts] + [zpad], axis=0).T.astype(BF16)
        vt = jnp.where(own, jnp.concatenate([v[:, hs]] * sb, axis=1), 0.0)
        vall = jnp.concatenate([jnp.zeros((n_f, wide), F32), vt,
                                jnp.zeros((hd - n_f - sb, wide), F32)], axis=0).astype(BF16)
        prod = _dot(pt, jnp.concatenate([fsel_ref[...], vall, qsel_ref[...]], axis=1))
        o_rows = []
        for j in range(sb):
            js = slice(j * hd, (j + 1) * hd)
            s_new = prod[:, js] * s_ref[0, j, h] + prod[:, wide:2 * wide][:, js]
            so_ref[0, j, h] = s_new
            o_rows.append(jnp.sum(prod[:, 2 * wide:][:, js] * s_new, axis=0, keepdims=True))
        o = jnp.concatenate(o_rows, axis=0)
        outs.append(o * lax.rsqrt(jnp.mean(o * o, axis=-1, keepdims=True) + EPS) * gw)
    o_all = jnp.concatenate(outs, axis=1)
    o_ref[...] = (o_all * _silu(g_ref[...])).astype(BF16)


def _hgrn_dec_call(z, row0, n_rows, lbp, gnorm_w, s_all, j, y_prev, s_prev, sb=SUBLANES):
    d_b = HG_HEADS * HG_DIM
    rb0 = row0 // sb
    fix2 = lambda i: (0, 0)
    col = lambda c: pl.BlockSpec((sb, d_b), functools.partial(lambda i, c: (rb0 + i, c), c=c))
    vec = pl.BlockSpec((1, d_b), fix2)
    sblk = pl.BlockSpec((1, sb, HG_HEADS, HG_DIM, HG_DIM), lambda i: (j, i, 0, 0, 0))
    fsel = np.zeros((HG_DIM, sb * HG_DIM), np.float32)
    qsel = np.zeros((HG_DIM, sb * HG_DIM), np.float32)
    for jj in range(sb):
        fsel[jj:3 * sb:sb, jj * HG_DIM:(jj + 1) * HG_DIM] = 1.0
        qsel[4 * sb + jj, jj * HG_DIM:(jj + 1) * HG_DIM] = 1.0
    sel = pl.BlockSpec(fsel.shape, fix2)
    ins = [z, z, z, z, lbp["log_lb"], lbp["log1m_lb"], lbp["one_m_lb"], gnorm_w.reshape(1, -1),
           jnp.asarray(fsel, BF16), jnp.asarray(qsel, BF16), s_all, y_prev]
    specs = [col(1), col(2), col(3), col(4), vec, vec, vec, pl.BlockSpec((1, HG_DIM), fix2),
             sel, sel, sblk, pl.BlockSpec(memory_space=pl.ANY)]
    aliases = {11: 0}
    if s_prev is not None:
        ins.append(s_prev)
        specs.append(pl.BlockSpec(memory_space=pl.ANY))
        aliases[12] = 1
    else:
        ins.append(jnp.zeros((SUBLANES, LANES), F32))
        specs.append(pl.BlockSpec((SUBLANES, LANES), fix2))
    return pl.pallas_call(
        functools.partial(_hgrn_dec_kernel, aliased=True),
        out_shape=(jax.ShapeDtypeStruct(y_prev.shape, BF16), jax.ShapeDtypeStruct(s_all.shape, F32)),
        grid=(n_rows // sb,),
        in_specs=specs,
        out_specs=(pl.BlockSpec((sb, d_b), lambda i: (rb0 + i, 0)), sblk),
        input_output_aliases=aliases,
        compiler_params=_cp("parallel"),
        name="hgrn_dec",
    )(*ins)


def _lru_coeffs(xc, wg_ref, bga_ref, bgx_ref, lam_ref, a_ref, b_ref):
    sw = LRU_SUPER
    for j in range(xc.shape[1] // sw):
        cs = slice(j * sw, (j + 1) * sw)
        xj = xc[:, cs]
        gj = _dot(xj.astype(BF16), wg_ref[j])
        rg = jax.nn.sigmoid(gj[:, :sw] + bga_ref[:, cs])
        ig = jax.nn.sigmoid(gj[:, sw:] + bgx_ref[:, cs])
        log_a = LRU_C * rg * _log_sigmoid(lam_ref[:, cs])
        a = jnp.exp(log_a)
        a_ref[:, cs] = a
        b_ref[:, cs] = jnp.sqrt(-jnp.tanh(log_a) * (a * a + 1.0)) * ig * xj


def _lru_kernel(gb_ref, xr_ref, cw_ref, cb_ref, wg_ref, bga_ref, bgx_ref, lam_ref, c0_ref, h0_ref,
                o_ref, hl_ref, xs_ref, a_ref, b_ref, *, nb):
    r = xr_ref.shape[0]
    hdr = xs_ref.shape[0] - r
    kw = cw_ref.shape[0]

    @pl.when(pl.program_id(0) == 0)
    def _():
        xs_ref[0:hdr, :] = c0_ref[...]
        hl_ref[...] = h0_ref[...]

    xs_ref[hdr:hdr + r, :] = xr_ref[...]
    odd = xs_ref[hdr - (kw - 1) * nb:hdr - nb + r, :]
    taps = [odd[0:r], xs_ref[hdr - 2 * nb:hdr - 2 * nb + r, :], odd[2 * nb:2 * nb + r], xs_ref[hdr:hdr + r, :]]
    xc = cb_ref[...] + taps[0] * cw_ref[0:1, :]
    for j in range(1, kw):
        xc = xc + taps[j] * cw_ref[j:j + 1, :]
    xs_ref[0:hdr, :] = xs_ref[r:r + hdr, :]

    _lru_coeffs(xc, wg_ref, bga_ref, bgx_ref, lam_ref, a_ref, b_ref)

    lower = lax.broadcasted_iota(jnp.int32, (SUBLANES, xc.shape[1]), 0) < nb

    def pair(k, h):
        rows = pl.ds(pl.multiple_of(k * SUBLANES, SUBLANES), SUBLANES)
        a8 = a_ref[rows, :]
        b8 = b_ref[rows, :]
        h_a = a8 * h + b8
        h_b = a8 * pltpu.roll(h_a, nb, 0) + b8
        b_ref[rows, :] = jnp.where(lower, h_a, h_b)
        return pltpu.roll(h_b, nb, 0)

    hl_ref[...] = lax.fori_loop(0, r // SUBLANES, pair, hl_ref[...], unroll=2)
    o_ref[...] = (_gelu(gb_ref[...]) * b_ref[...]).astype(BF16)


def _lru_call(z, prm, conv0, h0, nb, n_chunks):
    m = z.shape[0]
    d_rnn = prm["lam"].shape[1]
    r = nb * LRU_CHUNK_T
    hdr = conv0.shape[0]
    fix2 = lambda i: (0, 0)
    fix3 = lambda i: (0, 0, 0)
    vec = pl.BlockSpec((1, d_rnn), fix2)
    return pl.pallas_call(
        functools.partial(_lru_kernel, nb=nb),
        out_shape=(jax.ShapeDtypeStruct((m, d_rnn), BF16),
                   jax.ShapeDtypeStruct((SUBLANES, d_rnn), F32)),
        grid=(n_chunks,),
        in_specs=[pl.BlockSpec((r, d_rnn), lambda i: (i, 0)),
                  pl.BlockSpec((r, d_rnn), lambda i: (i, 1)),
                  pl.BlockSpec(prm["cw"].shape, fix2), vec,
                  pl.BlockSpec(prm["wg"].shape, fix3), vec, vec, vec,
                  pl.BlockSpec((hdr, d_rnn), fix2), pl.BlockSpec((SUBLANES, d_rnn), fix2)],
        out_specs=(pl.BlockSpec((r, d_rnn), lambda i: (i, 0)),
                   pl.BlockSpec((SUBLANES, d_rnn), fix2)),
        scratch_shapes=[pltpu.VMEM((hdr + r, d_rnn), F32), pltpu.VMEM((r, d_rnn), F32),
                        pltpu.VMEM((r, d_rnn), F32)],
        compiler_params=_cp("arbitrary"),
        name="lru",
    )(z, z, prm["cw"], prm["cb"], prm["wg"], prm["bga"], prm["bgx"], prm["lam"], conv0, h0)


def _lru_dec_kernel(gb_ref, xr_ref, buf_ref, cw_ref, cb_ref, wg_ref, bga_ref, bgx_ref, lam_ref,
                    h0_ref, yp_ref, o_ref, h_ref, a_ref, b_ref):
    kw = cw_ref.shape[0]
    xc = cb_ref[...] + xr_ref[...] * cw_ref[kw - 1:kw, :]
    for j in range(kw - 1):
        xc = xc + buf_ref[j] * cw_ref[j:j + 1, :]
    _lru_coeffs(xc, wg_ref, bga_ref, bgx_ref, lam_ref, a_ref, b_ref)
    h = a_ref[...] * h0_ref[...] + b_ref[...]
    h_ref[...] = h
    o_ref[...] = (_gelu(gb_ref[...]) * h).astype(BF16)


def _lru_dec_call(z, row0, prm, buf, h0, y_prev):
    n = h0.shape[0]
    d_rnn = prm["lam"].shape[1]
    rb0 = row0 // n
    fix2 = lambda i: (0, 0)
    fix3 = lambda i: (0, 0, 0)
    vec = pl.BlockSpec((1, d_rnn), fix2)
    full = pl.BlockSpec((n, d_rnn), fix2)
    return pl.pallas_call(
        _lru_dec_kernel,
        out_shape=(jax.ShapeDtypeStruct(y_prev.shape, BF16), jax.ShapeDtypeStruct((n, d_rnn), F32)),
        grid=(1,),
        in_specs=[pl.BlockSpec((n, d_rnn), lambda i: (rb0, 0)),
                  pl.BlockSpec((n, d_rnn), lambda i: (rb0, 1)),
                  pl.BlockSpec(buf.shape, fix3),
                  pl.BlockSpec(prm["cw"].shape, fix2), vec,
                  pl.BlockSpec(prm["wg"].shape, fix3), vec, vec, vec, full,
                  pl.BlockSpec(memory_space=pl.ANY)],
        out_specs=(pl.BlockSpec((n, d_rnn), lambda i: (rb0, 0)), full),
        scratch_shapes=[pltpu.VMEM((n, d_rnn), F32), pltpu.VMEM((n, d_rnn), F32)],
        input_output_aliases={10: 0},
        compiler_params=_cp("arbitrary"),
        name="lru_dec",
    )(z, z, buf, prm["cw"], prm["cb"], prm["wg"], prm["bga"], prm["bgx"], prm["lam"], h0, y_prev)


def _lru_params(conv_w, conv_b, w_ga, b_ga, w_gx, b_gx, lam):
    nblk, bs = w_ga.shape[0], w_ga.shape[1]
    per = LRU_SUPER // bs
    ns = nblk // per

    def sup(w):
        return jnp.tile(w.reshape(ns, per * bs, bs), (1, 1, per)) * _block_diag_mask(per, bs, bs)

    return dict(cw=conv_w, cb=conv_b.reshape(1, -1),
                wg=jnp.concatenate([sup(w_ga), sup(w_gx)], axis=-1).astype(BF16),
                bga=b_ga.reshape(1, -1), bgx=b_gx.reshape(1, -1), lam=lam.reshape(1, -1))


def kernel(x_prompt, x_sample, state_s5_re, state_s5_im, state_hgrn, state_lru, state_conv,
           c_prompt, c_sample, norm_w, final_norm_w, w_ada, b_ada, w_ffn_gu, w_ffn_d,
           w_in_ab, s5_lam_re, s5_lam_im, s5_b_re, s5_b_im, s5_c_re, s5_c_im, s5_d, s5_log_step,
           s5_w_glu, s5_b_glu, hg_lb_logits, hg_norm_w, w_out_ab, w_in_c, conv_w, conv_b,
           w_gate_a, b_gate_a, w_gate_x, b_gate_x, lru_lambda, w_out_c):
    bsz, seq, d = x_prompt.shape
    nsm = x_sample.shape[0]
    depth = w_ada.shape[0]
    n_ab, n_c = w_in_ab.shape[0], w_in_c.shape[0]
    g_a, n_a = s5_lam_re.shape[1], s5_lam_re.shape[2]
    n_state = g_a * n_a
    d_rnn = lru_lambda.shape[1]
    kw = conv_w.shape[1]
    mp_rows = seq * bsz
    assert 2 * bsz == SUBLANES and mp_rows % nsm == 0 and (mp_rows + nsm) % ROW_TILE == 0

    s5p = [_s5_params(s5_lam_re[j], s5_lam_im[j], s5_b_re[j], s5_b_im[j], s5_c_re[j], s5_c_im[j],
                      s5_d[j], s5_log_step[j], s5_w_glu[j], s5_b_glu[j]) for j in range(n_ab)]
    lb_all = jnp.cumsum(jax.nn.softmax(hg_lb_logits.astype(F32), axis=0), axis=0)
    lb_all = lb_all - lb_all[0:1]
    lbp = [dict(log_lb=jnp.log(lb_all[j]).reshape(1, -1),
                log1m_lb=jnp.log1p(-lb_all[j]).reshape(1, -1),
                one_m_lb=(1.0 - lb_all[j]).reshape(1, -1)) for j in range(n_ab)]
    lrup = [_lru_params(conv_w[j], conv_b[j], w_gate_a[j], b_gate_a[j], w_gate_x[j], b_gate_x[j],
                        lru_lambda[j]) for j in range(n_c)]
    w_out_ab_bf = w_out_ab.astype(BF16)
    w_out_c_bf = w_out_c.astype(BF16)
    norm_w = norm_w.reshape(depth * 3, 1, d)

    c_all = jnp.concatenate([c_sample] + [c_prompt] * (SUBLANES // bsz), axis=0)
    mods = _ada_call(c_all, w_ada, b_ada)

    x = _pack_call(x_prompt, x_sample.reshape(nsm, d), FINAL_TILE_T)
    zeros_state = jnp.zeros((SUBLANES, n_state), F32)
    s5r_p, s5i_p, hg_p, lru_p, conv_p = [], [], [], [], []
    s5r_s, s5i_s, lru_s, conv_s = [], [], [], []
    hg_s = None
    for l in range(depth):
        j = l // 2
        x = _ffn_call(x, norm_w, mods, nsm, w_ffn_gu, w_ffn_d, l, 0, 0, ROW_TILE, 256)
        if l % 2 == 0:
            z = _inproj_call(x, norm_w, mods, nsm, w_in_ab, l, j, IN_ROW_TILE, IN_COL_TILE)
            ya, hr, hi = _s5_call(z, s5p[j], zeros_state, zeros_state, bsz, S5_CHUNK_T,
                                  0, seq // S5_CHUNK_T)
            ya, hrs, his = _s5_call(z, s5p[j], state_s5_re[j].reshape(nsm, n_state),
                                    state_s5_im[j].reshape(nsm, n_state), nsm, 1,
                                    mp_rows // nsm, 1, y_prev=ya)
            yb, hg = _hgrn_call(z, lbp[j], hg_norm_w[j],
                                jnp.zeros((HG_HEADS, bsz * HG_DIM, HG_DIM), F32), bsz, seq // HG_CHUNK_T)
            yb, hg_s = _hgrn_dec_call(z, mp_rows, nsm, lbp[j], hg_norm_w[j], state_hgrn, j, yb, hg_s)
            s5r_p.append(hr[:bsz].reshape(bsz, g_a, n_a))
            s5i_p.append(hi[:bsz].reshape(bsz, g_a, n_a))
            s5r_s.append(hrs.reshape(nsm, g_a, n_a))
            s5i_s.append(his.reshape(nsm, g_a, n_a))
            hg_p.append(jnp.transpose(hg.reshape(HG_HEADS, bsz, HG_DIM, HG_DIM), (1, 0, 3, 2)))
            x = _outproj_call(x, mods, nsm, l, [ya, yb], w_out_ab_bf, j, OUT_ROW_TILE)
        else:
            z = _inproj_call(x, norm_w, mods, nsm, w_in_c, l, j, IN_ROW_TILE, IN_COL_TILE)
            y, hl = _lru_call(z, lrup[j], jnp.zeros((4 * bsz, d_rnn), F32),
                              jnp.zeros((SUBLANES, d_rnn), F32), bsz, seq // LRU_CHUNK_T)
            y, hls = _lru_dec_call(z, mp_rows, lrup[j], jnp.transpose(state_conv[j], (1, 0, 2)),
                                   state_lru[j], y)
            lru_p.append(hl[:bsz])
            lru_s.append(hls)
            tail = z[mp_rows - (kw - 1) * bsz:mp_rows, d_rnn:]
            conv_p.append(jnp.transpose(tail.reshape(kw - 1, bsz, d_rnn), (1, 0, 2)))
            conv_s.append(jnp.concatenate([state_conv[j][:, 1:], z[mp_rows:, None, d_rnn:]], axis=1))
            x = _outproj_call(x, mods, nsm, l, [y], w_out_c_bf, j, OUT_ROW_TILE)
        x = _ffn_call(x, norm_w, mods, nsm, w_ffn_gu, w_ffn_d, l, 1, 2, ROW_TILE, 256)
    y_prompt, y_sample = _final_norm_call(x, final_norm_w.reshape(1, -1), bsz, seq, nsm, FINAL_TILE_T)
    return (y_prompt, y_sample.reshape(nsm, 1, d), jnp.stack(s5r_p), jnp.stack(s5i_p), jnp.stack(hg_p),
            jnp.stack(lru_p), jnp.stack(conv_p),
            jnp.stack(s5r_s), jnp.stack(s5i_s), hg_s, jnp.stack(lru_s), jnp.stack(conv_s))
```

```python
import functools

import numpy as np
import jax
import jax.numpy as jnp
from jax import lax
from jax.experimental import pallas as pl
from jax.experimental.pallas import tpu as pltpu

F32 = jnp.float32
BF16 = jnp.bfloat16
EPS = 1e-6
LRU_C = 8.0

VMEM_LIMIT_BYTES = 60 * 1024 * 1024
SUBLANES = 8
LANES = 128
BF16_ROWS = 16

S5_COLS = 8
S5_CHUNK_T = 128
HG_HEADS = 8
HG_DIM = 128
HG_CHUNK_T = 64
LRU_CHUNK_T = 128
LRU_SUPER = 640
N_MOD = 9
ROW_TILE = 1040
OUT_ROW_TILE = 640
IN_ROW_TILE = 1664
IN_COL_TILE = 512
FINAL_TILE_T = 256
MXU_COLS = 256


def _cp(*sem):
    return pltpu.CompilerParams(dimension_semantics=sem, vmem_limit_bytes=VMEM_LIMIT_BYTES)


def _dot(a, b):
    return jnp.dot(a, b, preferred_element_type=F32)


def _dot_nt(a, b):
    return lax.dot_general(a, b, (((1,), (1,)), ((), ())), preferred_element_type=F32)


def _dot_tn(a, b):
    return lax.dot_general(a, b, (((0,), (0,)), ((), ())), preferred_element_type=F32)


def _silu(x):
    return x * jax.nn.sigmoid(x)


def _gelu(x):
    return jax.nn.gelu(x, approximate=True)


def _log_sigmoid(x):
    return jnp.minimum(x, 0.0) - jnp.log1p(jnp.exp(-jnp.abs(x)))


def _block_diag_mask(nblk, rows, cols):
    r = np.arange(nblk * rows)[:, None] // rows
    c = np.arange(nblk * cols)[None, :] // cols
    return jnp.asarray(r == c, F32)


def _split3(x):
    hi = x.astype(BF16)
    r1 = x - hi.astype(F32)
    mid = r1.astype(BF16)
    lo = (r1 - mid.astype(F32)).astype(BF16)
    return hi, mid, lo


def _fma_rows(y, mul, add):
    rm = mul.shape[0]
    r, d = y.shape
    if rm == r:
        return y * mul + add
    y3 = y.reshape(r // rm, rm, d)
    return (y3 * mul[None] + add[None]).reshape(r, d)


def _mul_rows(y, mul):
    rm = mul.shape[0]
    r, d = y.shape
    if rm == r:
        return y * mul
    return (y.reshape(r // rm, rm, d) * mul[None]).reshape(r, d)


def _norm_mod_rows(x_ref, nw_ref, sh_ref, sc_ref, h_ref, row0, nrows):
    slab = BF16_ROWS
    per_row = sh_ref.shape[0] > SUBLANES
    nw = nw_ref[...]

    def body(s, carry):
        r0 = pl.multiple_of(row0 + s * slab, slab)
        x = x_ref[pl.ds(r0, slab), :]
        ms = jnp.mean(x * x, axis=-1, keepdims=True)
        y = x * lax.rsqrt(ms + EPS)
        if per_row:
            m0 = pl.multiple_of(s * slab, slab)
            h = y * nw * (1.0 + sc_ref[pl.ds(m0, slab), :]) + sh_ref[pl.ds(m0, slab), :]
        else:
            h = _fma_rows(y, nw * (1.0 + sc_ref[...]), sh_ref[...])
        h_ref[pl.ds(r0, slab), :] = h.astype(BF16)
        return carry

    lax.fori_loop(0, nrows // slab, body, 0, unroll=8)


def _residual_rows(o_ref, x_ref, acc, gt_ref, gts_ref, scale):
    tm = o_ref.shape[0]
    ns = gts_ref.shape[0]
    np_ = tm - ns
    is_last = pl.program_id(0) == pl.num_programs(0) - 1
    gp = scale * gt_ref[...]
    o_ref[0:np_, :] = x_ref[0:np_, :] + _mul_rows(acc(0, np_), gp)

    @pl.when(jnp.logical_not(is_last))
    def _():
        o_ref[np_:tm, :] = x_ref[np_:tm, :] + _mul_rows(acc(np_, tm), gp)

    @pl.when(is_last)
    def _():
        o_ref[np_:tm, :] = x_ref[np_:tm, :] + acc(np_, tm) * (scale * gts_ref[...])


def _ada_kernel(c_ref, w_ref, b_ref, o_ref):
    sc = _silu(c_ref[...]).astype(BF16)
    o_ref[0] = _dot(sc, w_ref[0].astype(BF16)) + b_ref[0]


def _ada_call(c_all, w_ada, b_ada, tn=1024):
    depth, d, n = w_ada.shape
    r = c_all.shape[0]
    return pl.pallas_call(
        _ada_kernel,
        out_shape=jax.ShapeDtypeStruct((depth, r, n), F32),
        grid=(depth, n // tn),
        in_specs=[pl.BlockSpec((r, d), lambda l, j: (0, 0)),
                  pl.BlockSpec((1, d, tn), lambda l, j: (l, 0, j)),
                  pl.BlockSpec((1, 1, tn), lambda l, j: (l, 0, j))],
        out_specs=pl.BlockSpec((1, r, tn), lambda l, j: (l, 0, j)),
        compiler_params=_cp("parallel", "parallel"),
        name="ada",
    )(c_all, w_ada, b_ada.reshape(depth, 1, n))


def _ffn_kernel(x_ref, nw_ref, sh_ref, sc_ref, gt_ref, shs_ref, scs_ref, gts_ref,
                wg_ref, wu_ref, wd_ref, o_ref, h_ref):
    i, f = pl.program_id(0), pl.program_id(1)
    nw_ref, sh_ref, sc_ref, gt_ref, shs_ref, scs_ref, gts_ref = (
        r.at[0] for r in (nw_ref, sh_ref, sc_ref, gt_ref, shs_ref, scs_ref, gts_ref))
    tm = x_ref.shape[0]
    ns = shs_ref.shape[0]
    is_last = i == pl.num_programs(0) - 1

    @pl.when(f == 0)
    def _():
        _norm_mod_rows(x_ref, nw_ref, sh_ref, sc_ref, h_ref, 0, tm)
        o_ref[...] = jnp.zeros_like(o_ref)

    @pl.when(jnp.logical_and(f == 0, is_last))
    def _():
        _norm_mod_rows(x_ref, nw_ref, shs_ref, scs_ref, h_ref, tm - ns, ns)

    h = h_ref[...]
    g = _dot(h, wg_ref[0, 0].astype(BF16))
    u = _dot(h, wu_ref[0, 0].astype(BF16))
    a = (_silu(g) * u).astype(BF16)
    o_ref[...] += _dot(a, wd_ref[0, 0].astype(BF16))

    @pl.when(f == pl.num_programs(1) - 1)
    def _():
        _residual_rows(o_ref, x_ref, lambda a0, a1: o_ref[a0:a1, :], gt_ref, gts_ref, 0.5)


def _mod_specs(mods, ns, d, l, ks):
    def pat(k):
        return pl.BlockSpec((1, SUBLANES, d), lambda *g: (l, ns // SUBLANES, k))

    def smp(k):
        return pl.BlockSpec((1, ns, d), lambda *g: (l, 0, k))

    return [pat(k) for k in ks] + [smp(k) for k in ks], [mods] * (2 * len(ks))


def _ffn_call(x, norm_w, mods, ns, w_gu, w_d, l, s, sub, tm, tf):
    m, d = x.shape
    ff = w_d.shape[2]
    nf = ff // tf
    row = lambda i, f: (i, 0)
    mspecs, margs = _mod_specs(mods, ns, d, l, (3 * sub, 3 * sub + 1, 3 * sub + 2))
    return pl.pallas_call(
        _ffn_kernel,
        out_shape=jax.ShapeDtypeStruct((m, d), F32),
        grid=(m // tm, nf),
        in_specs=[pl.BlockSpec((tm, d), row), pl.BlockSpec((1, 1, d), lambda i, f: (3 * l + sub, 0, 0))]
                 + mspecs +
                 [pl.BlockSpec((1, 1, d, tf), lambda i, f: (l, s, 0, f)),
                  pl.BlockSpec((1, 1, d, tf), lambda i, f: (l, s, 0, f + nf)),
                  pl.BlockSpec((1, 1, tf, d), lambda i, f: (l, s, f, 0))],
        out_specs=pl.BlockSpec((tm, d), row),
        scratch_shapes=[pltpu.VMEM((tm, d), BF16)],
        compiler_params=_cp("parallel", "arbitrary"),
        name="ffn",
    )(x, norm_w, *margs, w_gu, w_gu, w_d)


def _inproj_kernel(x_ref, nw_ref, sh_ref, sc_ref, shs_ref, scs_ref, w_ref, o_ref, h_ref):
    i, k = pl.program_id(0), pl.program_id(1)
    nw_ref, sh_ref, sc_ref, shs_ref, scs_ref = (
        r.at[0] for r in (nw_ref, sh_ref, sc_ref, shs_ref, scs_ref))
    tm = x_ref.shape[0]
    ns = shs_ref.shape[0]

    @pl.when(k == 0)
    def _():
        _norm_mod_rows(x_ref, nw_ref, sh_ref, sc_ref, h_ref, 0, tm)

    @pl.when(jnp.logical_and(k == 0, i == pl.num_programs(0) - 1))
    def _():
        _norm_mod_rows(x_ref, nw_ref, shs_ref, scs_ref, h_ref, tm - ns, ns)

    h = h_ref[...]
    for c in range(0, o_ref.shape[1], MXU_COLS):
        o_ref[:, c:c + MXU_COLS] = _dot(h, w_ref[0, :, c:c + MXU_COLS].astype(BF16))


def _inproj_call(x, norm_w, mods, ns, w, l, j, tm, tn):
    m, d = x.shape
    n = w.shape[2]
    mspecs, margs = _mod_specs(mods, ns, d, l, (3, 4))
    return pl.pallas_call(
        _inproj_kernel,
        out_shape=jax.ShapeDtypeStruct((m, n), F32),
        grid=(m // tm, n // tn),
        in_specs=[pl.BlockSpec((tm, d), lambda i, k: (i, 0)),
                  pl.BlockSpec((1, 1, d), lambda i, k: (3 * l + 1, 0, 0))]
                 + mspecs + [pl.BlockSpec((1, d, tn), lambda i, k: (j, 0, k))],
        out_specs=pl.BlockSpec((tm, tn), lambda i, k: (i, k)),
        scratch_shapes=[pltpu.VMEM((tm, d), BF16)],
        compiler_params=_cp("parallel", "arbitrary"),
        name="inproj",
    )(x, norm_w, *margs, w)


def _outproj_kernel(*refs, n_in):
    x_ref, gt_ref, gts_ref = refs[0], refs[1].at[0], refs[2].at[0]
    a_refs = refs[3:3 + n_in]
    w_refs = refs[3 + n_in:3 + 2 * n_in]
    o_ref = refs[3 + 2 * n_in]
    acc = _dot(a_refs[0][...], w_refs[0][0])
    for a_ref, w_ref in zip(a_refs[1:], w_refs[1:]):
        acc = acc + _dot(a_ref[...], w_ref[0])
    _residual_rows(o_ref, x_ref, lambda a0, a1: acc[a0:a1, :], gt_ref, gts_ref, 1.0)


def _outproj_call(x, mods, ns, l, acts, w_bf, j, tm):
    m, d = x.shape
    n_in = len(acts)
    ka = acts[0].shape[1]
    row = lambda i: (i, 0)
    mspecs, margs = _mod_specs(mods, ns, d, l, (5,))
    in_specs = [pl.BlockSpec((tm, d), row)] + mspecs
    in_specs += [pl.BlockSpec((tm, ka), row) for _ in acts]
    in_specs += [pl.BlockSpec((1, ka, d), functools.partial(lambda i, p: (j, p, 0), p=p)) for p in range(n_in)]
    return pl.pallas_call(
        functools.partial(_outproj_kernel, n_in=n_in),
        out_shape=jax.ShapeDtypeStruct((m, d), F32),
        grid=(m // tm,),
        in_specs=in_specs,
        out_specs=pl.BlockSpec((tm, d), row),
        compiler_params=_cp("parallel"),
        name="outproj",
    )(x, *margs, *acts, *([w_bf] * n_in))


def _pack_kernel(xp_ref, xs_ref, o_ref, slab_ref, *, n_tiles):
    i = pl.program_id(0)
    nb, tt, d = xp_ref.shape

    @pl.when(i < n_tiles)
    def _():
        for c in range(d // LANES):
            cs = slice(c * LANES, (c + 1) * LANES)
            for b in range(nb):
                slab_ref[c, pl.ds(b, tt, stride=nb), :] = xp_ref[b, :, cs]
            o_ref[:, cs] = slab_ref[c]

    @pl.when(i == n_tiles)
    def _():
        o_ref[0:xs_ref.shape[0], :] = xs_ref[...]


def _pack_call(x_prompt, xs, tt):
    nb, seq, d = x_prompt.shape
    ns = xs.shape[0]
    n_tiles = seq // tt
    return pl.pallas_call(
        functools.partial(_pack_kernel, n_tiles=n_tiles),
        out_shape=jax.ShapeDtypeStruct((seq * nb + ns, d), F32),
        grid=(n_tiles + 1,),
        in_specs=[pl.BlockSpec((nb, tt, d), lambda i: (0, jnp.minimum(i, n_tiles - 1), 0)),
                  pl.BlockSpec((ns, d), lambda i: (0, 0))],
        out_specs=pl.BlockSpec((tt * nb, d), lambda i: (i, 0)),
        scratch_shapes=[pltpu.VMEM((d // LANES, tt * nb, LANES), F32)],
        compiler_params=_cp("arbitrary"),
        name="pack",
    )(x_prompt, xs)


def _final_norm_kernel(x_ref, w_ref, yp_ref, ys_ref, slab_ref, *, n_tiles):
    i = pl.program_id(0)
    nb, tt, d = yp_ref.shape
    w = w_ref[...]

    def inv_rms(x):
        return lax.rsqrt(jnp.mean(x * x, axis=-1, keepdims=True) + EPS)

    @pl.when(i < n_tiles)
    def _():
        inv = inv_rms(x_ref[...])
        for c in range(d // LANES):
            cs = slice(c * LANES, (c + 1) * LANES)
            slab_ref[c] = x_ref[:, cs] * inv * w[:, cs]
            for b in range(nb):
                yp_ref[b, :, cs] = slab_ref[c, pl.ds(b, tt, stride=nb), :]

    @pl.when(i == n_tiles)
    def _():
        x = x_ref[0:ys_ref.shape[0], :]
        ys_ref[...] = x * inv_rms(x) * w


def _final_norm_call(x, w, nb, seq, ns, tt):
    d = x.shape[1]
    n_tiles = seq // tt
    return pl.pallas_call(
        functools.partial(_final_norm_kernel, n_tiles=n_tiles),
        out_shape=(jax.ShapeDtypeStruct((nb, seq, d), F32), jax.ShapeDtypeStruct((ns, d), F32)),
        grid=(n_tiles + 1,),
        in_specs=[pl.BlockSpec((tt * nb, d), lambda i: (i, 0)),
                  pl.BlockSpec((1, d), lambda i: (0, 0))],
        out_specs=(pl.BlockSpec((nb, tt, d), lambda i: (0, jnp.minimum(i, n_tiles - 1), 0)),
                   pl.BlockSpec((ns, d), lambda i: (0, 0))),
        scratch_shapes=[pltpu.VMEM((d // LANES, tt * nb, LANES), F32)],
        compiler_params=_cp("arbitrary"),
        name="final_norm",
    )(x, w)


def _s5_kernel(*refs, nb, tc, aliased):
    (u_ref, wb_ref, ar_ref, ai_ref, wcr_ref, wci_ref, d_ref, wglu_ref, bglu_ref,
     h0r_ref, h0i_ref) = refs[:11]
    o_ref, hr_ref, hi_ref, xr_ref, xi_ref = refs[12:] if aliased else refs[11:]

    @pl.when(pl.program_id(0) == 0)
    def _():
        hr_ref[...] = h0r_ref[...]
        hi_ref[...] = h0i_ref[...]

    u = u_ref[...]
    ub = u.astype(BF16)
    n_state = xr_ref.shape[1]
    cw = n_state // S5_COLS
    for c in range(S5_COLS):
        xc = _dot(ub[:, c * LANES:(c + 1) * LANES], wb_ref[c])
        xr_ref[:, c * cw:(c + 1) * cw] = xc[:, :cw]
        xi_ref[:, c * cw:(c + 1) * cw] = xc[:, cw:]

    lane_group = 1024
    for lo_ in range(0, n_state, lane_group):
        ls = pl.ds(lo_, lane_group)
        ar = ar_ref[:, ls]
        ai = ai_ref[:, ls]
        if tc == 1:
            hr, hi = hr_ref[:, ls], hi_ref[:, ls]
            nr = ar * hr - ai * hi + xr_ref[:, ls]
            ni = ar * hi + ai * hr + xi_ref[:, ls]
            xr_ref[:, ls] = nr
            xi_ref[:, ls] = ni
        else:
            lower = lax.broadcasted_iota(jnp.int32, (SUBLANES, lane_group), 0) < nb

            def pair(k, carry, ls=ls, ar=ar, ai=ai, lower=lower):
                hr, hi = carry
                rows = pl.ds(pl.multiple_of(k * SUBLANES, SUBLANES), SUBLANES)
                x_r, x_i = xr_ref[rows, ls], xi_ref[rows, ls]
                ar_ = ar * hr - ai * hi + x_r
                ai_ = ar * hi + ai * hr + x_i
                sr, si = pltpu.roll(ar_, nb, 0), pltpu.roll(ai_, nb, 0)
                br_ = ar * sr - ai * si + x_r
                bi_ = ar * si + ai * sr + x_i
                xr_ref[rows, ls] = jnp.where(lower, ar_, br_)
                xi_ref[rows, ls] = jnp.where(lower, ai_, bi_)
                return pltpu.roll(br_, nb, 0), pltpu.roll(bi_, nb, 0)

            nr, ni = lax.fori_loop(0, tc // 2, pair, (hr_ref[:, ls], hi_ref[:, ls]), unroll=2)
        hr_ref[:, ls] = nr
        hi_ref[:, ls] = ni

    ys = []
    for c in range(S5_COLS):
        cs = pl.ds(c * cw, cw)
        ys.append(_dot(xr_ref[:, cs].astype(BF16), wcr_ref[c])
                  + _dot(xi_ref[:, cs].astype(BF16), wci_ref[c]))
    y = jnp.concatenate(ys, axis=1) + d_ref[...] * u
    y = _gelu(y)
    z = _dot(y.astype(BF16), wglu_ref[...]) + bglu_ref[...]
    o_ref[...] = (y * jax.nn.sigmoid(z)).astype(BF16)


def _s5_call(z, prm, h0r, h0i, nb, tc, row_block0, n_chunks, y_prev=None):
    m = z.shape[0]
    d_a = prm["d"].shape[1]
    n_state = prm["ar"].shape[1]
    r = nb * tc
    sr = h0r.shape[0]
    fix2 = lambda i: (0, 0)
    fix3 = lambda i: (0, 0, 0)
    blk = lambda i: (row_block0 + i, 0)
    ins = [z, prm["wb"], prm["ar"], prm["ai"], prm["wcr"], prm["wci"], prm["d"], prm["wglu"],
           prm["bglu"], h0r, h0i]
    specs = [pl.BlockSpec((r, d_a), blk),
             pl.BlockSpec(prm["wb"].shape, fix3),
             pl.BlockSpec((1, n_state), fix2), pl.BlockSpec((1, n_state), fix2),
             pl.BlockSpec(prm["wcr"].shape, fix3), pl.BlockSpec(prm["wci"].shape, fix3),
             pl.BlockSpec((1, d_a), fix2), pl.BlockSpec((d_a, d_a), fix2),
             pl.BlockSpec((1, d_a), fix2),
             pl.BlockSpec((sr, n_state), fix2), pl.BlockSpec((sr, n_state), fix2)]
    aliases = {}
    if y_prev is not None:
        ins.append(y_prev)
        specs.append(pl.BlockSpec(memory_space=pl.ANY))
        aliases = {len(ins) - 1: 0}
    return pl.pallas_call(
        functools.partial(_s5_kernel, nb=nb, tc=tc, aliased=y_prev is not None),
        out_shape=(jax.ShapeDtypeStruct((m, d_a), BF16),
                   jax.ShapeDtypeStruct((sr, n_state), F32),
                   jax.ShapeDtypeStruct((sr, n_state), F32)),
        grid=(n_chunks,),
        in_specs=specs,
        out_specs=(pl.BlockSpec((r, d_a), blk),
                   pl.BlockSpec((sr, n_state), fix2), pl.BlockSpec((sr, n_state), fix2)),
        scratch_shapes=[pltpu.VMEM((r, n_state), F32), pltpu.VMEM((r, n_state), F32)],
        input_output_aliases=aliases,
        compiler_params=_cp("arbitrary"),
        name="s5",
    )(*ins)


def _s5_params(lam_re, lam_im, b_re, b_im, c_re, c_im, d_skip, log_step, w_glu, b_glu):
    g, n = lam_re.shape
    gs = b_re.shape[2]
    step = jnp.exp(log_step)[:, None]
    mag = jnp.exp(lam_re * step)
    abar_r, abar_i = mag * jnp.cos(lam_im * step), mag * jnp.sin(lam_im * step)
    den = lam_re * lam_re + lam_im * lam_im
    pr, pim = abar_r - 1.0, abar_i
    zr = (pr * lam_re + pim * lam_im) / den
    zi = (pim * lam_re - pr * lam_im) / den
    bbr = zr[..., None] * b_re - zi[..., None] * b_im
    bbi = zr[..., None] * b_im + zi[..., None] * b_re
    gl = LANES // gs
    nc = g // gl

    def in_mat(bb):
        a = jnp.transpose(bb.reshape(nc, gl, n, gs), (0, 1, 3, 2)).reshape(nc, gl * gs, n)
        return jnp.tile(a, (1, 1, gl)) * _block_diag_mask(gl, gs, n)

    def out_mat(cc):
        a = jnp.transpose(cc.reshape(nc, gl, gs, n), (0, 1, 3, 2)).reshape(nc, gl * n, gs)
        return jnp.tile(a, (1, 1, gl)) * _block_diag_mask(gl, n, gs)

    return dict(
        wb=jnp.concatenate([in_mat(bbr), in_mat(bbi)], axis=-1).astype(BF16),
        wcr=out_mat(c_re).astype(BF16),
        wci=out_mat(-c_im).astype(BF16),
        ar=abar_r.reshape(1, g * n), ai=abar_i.reshape(1, g * n),
        d=d_skip.reshape(1, -1), wglu=w_glu.astype(BF16), bglu=b_glu.reshape(1, -1))


def _log_f(fz, log_lb, log1m_lb):
    b = log1m_lb + _log_sigmoid(fz)
    a = jnp.broadcast_to(log_lb, b.shape)
    return jnp.maximum(a, b) + jnp.log1p(jnp.exp(-jnp.abs(a - b)))


def _hgrn_consts(tc, nb):
    r = tc * nb
    levels = int(np.log2(tc))
    t = np.arange(r) // nb
    b = np.arange(r) % nb
    same_b = b[:, None] == b[None, :]
    tt, ts = t[:, None], t[None, :]
    mask = np.zeros((levels + 1, r, r), np.float32)
    for l in range(levels):
        blk, half = 2 << l, 1 << l
        upper = (t % blk) >= half
        mask[l] = same_b & ((tt // blk) == (ts // blk)) & upper[:, None] & ~upper[None, :]
    expo = (same_b & (ts <= tt)).astype(np.float32)
    mask[levels] = np.eye(r, dtype=np.float32)
    bmask = np.zeros((nb, r, LANES), np.float32)
    for k in range(nb):
        bmask[k, b == k, :] = 1.0
    return expo, mask, bmask


def _hgrn_kernel(q_ref, fz_ref, v_ref, g_ref, llb_ref, l1m_ref, om_ref, gw_ref, expo_ref, mask_ref,
                 bm_ref, s0_ref, o_ref, st_ref, sc_ref, gs_ref, lf_ref, *, nb):
    levels = mask_ref.shape[0] - 1
    hd = HG_DIM
    r = q_ref.shape[0]

    @pl.when(pl.program_id(0) == 0)
    def _():
        st_ref[...] = s0_ref[...]
        sc_ref[...] = jnp.zeros_like(sc_ref)

    lf = _log_f(fz_ref[...], llb_ref[...], l1m_ref[...])
    lf_ref[...] = lf
    lf_hi = lf.astype(BF16)
    lf_lo = (lf - lf_hi.astype(F32)).astype(BF16)
    gs_ref[...] = _dot(expo_ref[...], lf_hi) + _dot(expo_ref[...], lf_lo)

    tiles = r // SUBLANES
    lower = lax.broadcasted_iota(jnp.int32, (1, SUBLANES, 1), 1) < nb

    def both_halves(t):
        return jnp.where(lower, pltpu.roll(t, nb, 1), t)

    gw = gw_ref[...]
    for h in range(HG_HEADS):
        hs = slice(h * hd, (h + 1) * hd)
        q = q_ref[:, hs]
        kk = om_ref[:, hs] * jax.nn.sigmoid(-fz_ref[:, hs])
        gcum = gs_ref[:, hs]
        g3 = gcum.reshape(tiles, SUBLANES, hd)

        def level_expo(l, g3=g3, hs=hs):
            if l == 0:
                return jnp.where(lower, 0.0, lf_ref[:, hs].reshape(tiles, SUBLANES, hd)).reshape(r, hd)
            span = 1 << l
            g4 = g3.reshape(tiles // span, span, SUBLANES, hd)
            split = both_halves(g4[:, span // 2 - 1])
            return (-jnp.abs(g4 - split[:, None])).reshape(r, hd)

        halves = (slice(0, r // 2), slice(r // 2, r))
        qb, kb = q.astype(BF16), kk.astype(BF16)
        acc = [mask_ref[levels, hv, hv] * _dot_nt(qb[hv], kb[hv]) for hv in halves]
        for l in range(levels - 1):
            e = jnp.exp(level_expo(l))
            qt, kt = (q * e).astype(BF16), (kk * e).astype(BF16)
            acc = [a + mask_ref[l, hv, hv] * _dot_nt(qt[hv], kt[hv]) for a, hv in zip(acc, halves)]
        e = jnp.exp(level_expo(levels - 1))
        lo_, hi_ = halves
        sc_ref[lo_, lo_] = acc[0]
        sc_ref[hi_, hi_] = acc[1]
        sc_ref[hi_, lo_] = mask_ref[levels - 1, hi_, lo_] * _dot_nt((q[hi_] * e[hi_]).astype(BF16),
                                                                   (kk[lo_] * e[lo_]).astype(BF16))

        qg = (q * jnp.exp(gcum)).astype(BF16)
        to_end = (both_halves(g3[tiles - 1:tiles]) - g3).reshape(r, hd)
        kend = (kk * jnp.exp(to_end)).astype(BF16)
        v = v_ref[:, hs]
        st = st_ref[h]
        o = _dot(sc_ref[...].astype(BF16), v.astype(BF16))
        oi = _dot_nt(qg, st.astype(BF16))
        for k in range(nb):
            o = o + bm_ref[k] * oi[:, k * hd:(k + 1) * hd]
        vcat = jnp.concatenate([(v * bm_ref[k]).astype(BF16) for k in range(nb)], axis=1)
        upd = _dot_tn(vcat, kend)
        dec = jnp.concatenate(
            [jnp.broadcast_to(jnp.exp(gs_ref[r - nb + k:r - nb + k + 1, hs]), (hd, hd)) for k in range(nb)],
            axis=0)
        st_ref[h] = dec * st + upd
        o = o * lax.rsqrt(jnp.mean(o * o, axis=-1, keepdims=True) + EPS) * gw
        o_ref[:, hs] = (o * _silu(g_ref[:, hs])).astype(BF16)


def _hgrn_call(z, lbp, gnorm_w, s0t, nb, n_chunks):
    m = z.shape[0]
    d_b = HG_HEADS * HG_DIM
    r = HG_CHUNK_T * nb
    expo, mask, bmask = _hgrn_consts(HG_CHUNK_T, nb)
    fix2 = lambda i: (0, 0)
    fix3 = lambda i: (0, 0, 0)
    col = lambda c: pl.BlockSpec((r, d_b), functools.partial(lambda i, c: (i, c), c=c))
    vec = pl.BlockSpec((1, d_b), fix2)
    return pl.pallas_call(
        functools.partial(_hgrn_kernel, nb=nb),
        out_shape=(jax.ShapeDtypeStruct((m, d_b), BF16),
                   jax.ShapeDtypeStruct(s0t.shape, F32)),
        grid=(n_chunks,),
        in_specs=[col(1), col(2), col(3), col(4), vec, vec, vec,
                  pl.BlockSpec((1, HG_DIM), fix2),
                  pl.BlockSpec(expo.shape, fix2), pl.BlockSpec(mask.shape, fix3),
                  pl.BlockSpec(bmask.shape, fix3), pl.BlockSpec(s0t.shape, fix3)],
        out_specs=(pl.BlockSpec((r, d_b), lambda i: (i, 0)), pl.BlockSpec(s0t.shape, fix3)),
        scratch_shapes=[pltpu.VMEM((r, r), F32), pltpu.VMEM((r, d_b), F32), pltpu.VMEM((r, d_b), F32)],
        compiler_params=_cp("arbitrary"),
        name="hgrn",
    )(z, z, z, z, lbp["log_lb"], lbp["log1m_lb"], lbp["one_m_lb"], gnorm_w.reshape(1, -1),
      jnp.asarray(expo, BF16), jnp.asarray(mask, F32), jnp.asarray(bmask, F32), s0t)


def _hgrn_dec_kernel(*refs, aliased):
    q_ref, fz_ref, v_ref, g_ref, llb_ref, l1m_ref, om_ref, gw_ref, fsel_ref, qsel_ref, s_ref = refs[:11]
    o_ref, so_ref = refs[13:] if aliased else refs[11:]
    hd = HG_DIM
    sb = q_ref.shape[0]
    fz = fz_ref[...]
    f = jnp.exp(_log_f(fz, llb_ref[...], l1m_ref[...]))
    kk = om_ref[...] * jax.nn.sigmoid(-fz)
    v = v_ref[...]
    gw = gw_ref[...]
    parts = [p.astype(F32) for p in _split3(f)]
    parts += [kk.astype(BF16).astype(F32), q_ref[...].astype(BF16).astype(F32)]
    n_f = 3 * sb
    zpad = jnp.zeros((hd - len(parts) * sb, hd), F32)
    wide = sb * hd
    own = (lax.broadcasted_iota(jnp.int32, (sb, wide), 0)
           == lax.shift_right_logical(lax.broadcasted_iota(jnp.int32, (sb, wide), 1), 7))
    outs = []
    for h in range(HG_HEADS):
        hs = slice(h * hd, (h + 1) * hd)
        pt = jnp.concatenate([p[:, hs] for p in parts] + [zpad], axis=0).T.astype(BF16)
        vt = jnp.where(own, jnp.concatenate([v[:, hs]] * sb, axis=1), 0.0)
        vall = jnp.concatenate([jnp.zeros((n_f, wide), F32), vt,
                                jnp.zeros((hd - n_f - sb, wide), F32)], axis=0).astype(BF16)
        prod = _dot(pt, jnp.concatenate([fsel_ref[...], vall, qsel_ref[...]], axis=1))
        o_rows = []
        for j in range(sb):
            js = slice(j * hd, (j + 1) * hd)
            s_new = prod[:, js] * s_ref[0, j, h] + prod[:, wide:2 * wide][:, js]
            so_ref[0, j, h] = s_new
            o_rows.append(jnp.sum(prod[:, 2 * wide:][:, js] * s_new, axis=0, keepdims=True))
        o = jnp.concatenate(o_rows, axis=0)
        outs.append(o * lax.rsqrt(jnp.mean(o * o, axis=-1, keepdims=True) + EPS) * gw)
    o_all = jnp.concatenate(outs, axis=1)
    o_ref[...] = (o_all * _silu(g_ref[...])).astype(BF16)


def _hgrn_dec_call(z, row0, n_rows, lbp, gnorm_w, s_all, j, y_prev, s_prev, sb=SUBLANES):
    d_b = HG_HEADS * HG_DIM
    rb0 = row0 // sb
    fix2 = lambda i: (0, 0)
    col = lambda c: pl.BlockSpec((sb, d_b), functools.partial(lambda i, c: (rb0 + i, c), c=c))
    vec = pl.BlockSpec((1, d_b), fix2)
    sblk = pl.BlockSpec((1, sb, HG_HEADS, HG_DIM, HG_DIM), lambda i: (j, i, 0, 0, 0))
    fsel = np.zeros((HG_DIM, sb * HG_DIM), np.float32)
    qsel = np.zeros((HG_DIM, sb * HG_DIM), np.float32)
    for jj in range(sb):
        fsel[jj:3 * sb:sb, jj * HG_DIM:(jj + 1) * HG_DIM] = 1.0
        qsel[4 * sb + jj, jj * HG_DIM:(jj + 1) * HG_DIM] = 1.0
    sel = pl.BlockSpec(fsel.shape, fix2)
    ins = [z, z, z, z, lbp["log_lb"], lbp["log1m_lb"], lbp["one_m_lb"], gnorm_w.reshape(1, -1),
           jnp.asarray(fsel, BF16), jnp.asarray(qsel, BF16), s_all, y_prev]
    specs = [col(1), col(2), col(3), col(4), vec, vec, vec, pl.BlockSpec((1, HG_DIM), fix2),
             sel, sel, sblk, pl.BlockSpec(memory_space=pl.ANY)]
    aliases = {11: 0}
    if s_prev is not None:
        ins.append(s_prev)
        specs.append(pl.BlockSpec(memory_space=pl.ANY))
        aliases[12] = 1
    else:
        ins.append(jnp.zeros((SUBLANES, LANES), F32))
        specs.append(pl.BlockSpec((SUBLANES, LANES), fix2))
    return pl.pallas_call(
        functools.partial(_hgrn_dec_kernel, aliased=True),
        out_shape=(jax.ShapeDtypeStruct(y_prev.shape, BF16), jax.ShapeDtypeStruct(s_all.shape, F32)),
        grid=(n_rows // sb,),
        in_specs=specs,
        out_specs=(pl.BlockSpec((sb, d_b), lambda i: (rb0 + i, 0)), sblk),
        input_output_aliases=aliases,
        compiler_params=_cp("parallel"),
        name="hgrn_dec",
    )(*ins)


def _lru_coeffs(conv_block, wg_ref, bga_ref, bgx_ref, lam_ref, a_ref, b_ref):
    sw = LRU_SUPER
    for j in range(a_ref.shape[1] // sw):
        cs = slice(j * sw, (j + 1) * sw)
        xj = conv_block(cs)
        gj = _dot(xj.astype(BF16), wg_ref[j])
        rg = jax.nn.sigmoid(gj[:, :sw] + bga_ref[:, cs])
        ig = jax.nn.sigmoid(gj[:, sw:] + bgx_ref[:, cs])
        log_a = LRU_C * rg * _log_sigmoid(lam_ref[:, cs])
        a = jnp.exp(log_a)
        a_ref[:, cs] = a
        b_ref[:, cs] = jnp.sqrt(-jnp.tanh(log_a) * (a * a + 1.0)) * ig * xj


def _lru_kernel(gb_ref, xr_ref, cw_ref, cb_ref, wg_ref, bga_ref, bgx_ref, lam_ref, c0_ref, h0_ref,
                o_ref, hl_ref, xs_ref, a_ref, b_ref, *, nb):
    r = xr_ref.shape[0]
    hdr = xs_ref.shape[0] - r
    kw = cw_ref.shape[0]

    @pl.when(pl.program_id(0) == 0)
    def _():
        xs_ref[0:hdr, :] = c0_ref[...]
        hl_ref[...] = h0_ref[...]

    xs_ref[hdr:hdr + r, :] = xr_ref[...]
    n_win = hdr + r

    def conv_block(cs):
        win = xs_ref[:, cs]
        sh = pltpu.roll(win, n_win - nb, 0)
        taps = [sh[hdr - kw * nb:hdr - kw * nb + r], win[hdr - 2 * nb:hdr - 2 * nb + r],
                sh[hdr - 2 * nb:hdr - 2 * nb + r], win[hdr:hdr + r]]
        xc = cb_ref[:, cs] + taps[0] * cw_ref[0:1, cs]
        for j in range(1, kw):
            xc = xc + taps[j] * cw_ref[j:j + 1, cs]
        return xc

    _lru_coeffs(conv_block, wg_ref, bga_ref, bgx_ref, lam_ref, a_ref, b_ref)
    xs_ref[0:hdr, :] = xs_ref[r:r + hdr, :]

    lower = lax.broadcasted_iota(jnp.int32, (SUBLANES, a_ref.shape[1]), 0) < nb

    def pair(k, h):
        rows = pl.ds(pl.multiple_of(k * SUBLANES, SUBLANES), SUBLANES)
        a8 = a_ref[rows, :]
        b8 = b_ref[rows, :]
        h_a = a8 * h + b8
        h_b = a8 * pltpu.roll(h_a, nb, 0) + b8
        b_ref[rows, :] = jnp.where(lower, h_a, h_b)
        return pltpu.roll(h_b, nb, 0)

    hl_ref[...] = lax.fori_loop(0, r // SUBLANES, pair, hl_ref[...], unroll=2)
    o_ref[...] = (_gelu(gb_ref[...]) * b_ref[...]).astype(BF16)


def _lru_call(z, prm, conv0, h0, nb, n_chunks):
    m = z.shape[0]
    d_rnn = prm["lam"].shape[1]
    r = nb * LRU_CHUNK_T
    hdr = conv0.shape[0]
    fix2 = lambda i: (0, 0)
    fix3 = lambda i: (0, 0, 0)
    vec = pl.BlockSpec((1, d_rnn), fix2)
    return pl.pallas_call(
        functools.partial(_lru_kernel, nb=nb),
        out_shape=(jax.ShapeDtypeStruct((m, d_rnn), BF16),
                   jax.ShapeDtypeStruct((SUBLANES, d_rnn), F32)),
        grid=(n_chunks,),
        in_specs=[pl.BlockSpec((r, d_rnn), lambda i: (i, 0)),
                  pl.BlockSpec((r, d_rnn), lambda i: (i, 1)),
                  pl.BlockSpec(prm["cw"].shape, fix2), vec,
                  pl.BlockSpec(prm["wg"].shape, fix3), vec, vec, vec,
                  pl.BlockSpec((hdr, d_rnn), fix2), pl.BlockSpec((SUBLANES, d_rnn), fix2)],
        out_specs=(pl.BlockSpec((r, d_rnn), lambda i: (i, 0)),
                   pl.BlockSpec((SUBLANES, d_rnn), fix2)),
        scratch_shapes=[pltpu.VMEM((hdr + r, d_rnn), F32), pltpu.VMEM((r, d_rnn), F32),
                        pltpu.VMEM((r, d_rnn), F32)],
        compiler_params=_cp("arbitrary"),
        name="lru",
    )(z, z, prm["cw"], prm["cb"], prm["wg"], prm["bga"], prm["bgx"], prm["lam"], conv0, h0)


def _lru_dec_kernel(gb_ref, xr_ref, buf_ref, cw_ref, cb_ref, wg_ref, bga_ref, bgx_ref, lam_ref,
                    h0_ref, yp_ref, o_ref, h_ref, a_ref, b_ref):
    kw = cw_ref.shape[0]

    def conv_block(cs):
        xc = cb_ref[:, cs] + xr_ref[:, cs] * cw_ref[kw - 1:kw, cs]
        for j in range(kw - 1):
            xc = xc + buf_ref[j, :, cs] * cw_ref[j:j + 1, cs]
        return xc

    _lru_coeffs(conv_block, wg_ref, bga_ref, bgx_ref, lam_ref, a_ref, b_ref)
    h = a_ref[...] * h0_ref[...] + b_ref[...]
    h_ref[...] = h
    o_ref[...] = (_gelu(gb_ref[...]) * h).astype(BF16)


def _lru_dec_call(z, row0, prm, buf, h0, y_prev):
    n = h0.shape[0]
    d_rnn = prm["lam"].shape[1]
    rb0 = row0 // n
    fix2 = lambda i: (0, 0)
    fix3 = lambda i: (0, 0, 0)
    vec = pl.BlockSpec((1, d_rnn), fix2)
    full = pl.BlockSpec((n, d_rnn), fix2)
    return pl.pallas_call(
        _lru_dec_kernel,
        out_shape=(jax.ShapeDtypeStruct(y_prev.shape, BF16), jax.ShapeDtypeStruct((n, d_rnn), F32)),
        grid=(1,),
        in_specs=[pl.BlockSpec((n, d_rnn), lambda i: (rb0, 0)),
                  pl.BlockSpec((n, d_rnn), lambda i: (rb0, 1)),
                  pl.BlockSpec(buf.shape, fix3),
                  pl.BlockSpec(prm["cw"].shape, fix2), vec,
                  pl.BlockSpec(prm["wg"].shape, fix3), vec, vec, vec, full,
                  pl.BlockSpec(memory_space=pl.ANY)],
        out_specs=(pl.BlockSpec((n, d_rnn), lambda i: (rb0, 0)), full),
        scratch_shapes=[pltpu.VMEM((n, d_rnn), F32), pltpu.VMEM((n, d_rnn), F32)],
        input_output_aliases={10: 0},
        compiler_params=_cp("arbitrary"),
        name="lru_dec",
    )(z, z, buf, prm["cw"], prm["cb"], prm["wg"], prm["bga"], prm["bgx"], prm["lam"], h0, y_prev)


def _lru_params(conv_w, conv_b, w_ga, b_ga, w_gx, b_gx, lam):
    nblk, bs = w_ga.shape[0], w_ga.shape[1]
    per = LRU_SUPER // bs
    ns = nblk // per

    def sup(w):
        return jnp.tile(w.reshape(ns, per * bs, bs), (1, 1, per)) * _block_diag_mask(per, bs, bs)

    return dict(cw=conv_w, cb=conv_b.reshape(1, -1),
                wg=jnp.concatenate([sup(w_ga), sup(w_gx)], axis=-1).astype(BF16),
                bga=b_ga.reshape(1, -1), bgx=b_gx.reshape(1, -1), lam=lam.reshape(1, -1))


def kernel(x_prompt, x_sample, state_s5_re, state_s5_im, state_hgrn, state_lru, state_conv,
           c_prompt, c_sample, norm_w, final_norm_w, w_ada, b_ada, w_ffn_gu, w_ffn_d,
           w_in_ab, s5_lam_re, s5_lam_im, s5_b_re, s5_b_im, s5_c_re, s5_c_im, s5_d, s5_log_step,
           s5_w_glu, s5_b_glu, hg_lb_logits, hg_norm_w, w_out_ab, w_in_c, conv_w, conv_b,
           w_gate_a, b_gate_a, w_gate_x, b_gate_x, lru_lambda, w_out_c):
    bsz, seq, d = x_prompt.shape
    nsm = x_sample.shape[0]
    depth = w_ada.shape[0]
    n_ab, n_c = w_in_ab.shape[0], w_in_c.shape[0]
    g_a, n_a = s5_lam_re.shape[1], s5_lam_re.shape[2]
    n_state = g_a * n_a
    d_rnn = lru_lambda.shape[1]
    kw = conv_w.shape[1]
    mp_rows = seq * bsz
    assert 2 * bsz == SUBLANES and mp_rows % nsm == 0 and (mp_rows + nsm) % ROW_TILE == 0

    s5p = [_s5_params(s5_lam_re[j], s5_lam_im[j], s5_b_re[j], s5_b_im[j], s5_c_re[j], s5_c_im[j],
                      s5_d[j], s5_log_step[j], s5_w_glu[j], s5_b_glu[j]) for j in range(n_ab)]
    lb_all = jnp.cumsum(jax.nn.softmax(hg_lb_logits.astype(F32), axis=0), axis=0)
    lb_all = lb_all - lb_all[0:1]
    lbp = [dict(log_lb=jnp.log(lb_all[j]).reshape(1, -1),
                log1m_lb=jnp.log1p(-lb_all[j]).reshape(1, -1),
                one_m_lb=(1.0 - lb_all[j]).reshape(1, -1)) for j in range(n_ab)]
    lrup = [_lru_params(conv_w[j], conv_b[j], w_gate_a[j], b_gate_a[j], w_gate_x[j], b_gate_x[j],
                        lru_lambda[j]) for j in range(n_c)]
    w_out_ab_bf = w_out_ab.astype(BF16)
    w_out_c_bf = w_out_c.astype(BF16)
    norm_w = norm_w.reshape(depth * 3, 1, d)

    c_all = jnp.concatenate([c_sample] + [c_prompt] * (SUBLANES // bsz), axis=0)
    mods = _ada_call(c_all, w_ada, b_ada)

    x = _pack_call(x_prompt, x_sample.reshape(nsm, d), FINAL_TILE_T)
    zeros_state = jnp.zeros((SUBLANES, n_state), F32)
    s5r_p, s5i_p, hg_p, lru_p, conv_p = [], [], [], [], []
    s5r_s, s5i_s, lru_s, conv_s = [], [], [], []
    hg_s = None
    for l in range(depth):
        j = l // 2
        x = _ffn_call(x, norm_w, mods, nsm, w_ffn_gu, w_ffn_d, l, 0, 0, ROW_TILE, 256)
        if l % 2 == 0:
            z = _inproj_call(x, norm_w, mods, nsm, w_in_ab, l, j, IN_ROW_TILE, IN_COL_TILE)
            ya, hr, hi = _s5_call(z, s5p[j], zeros_state, zeros_state, bsz, S5_CHUNK_T,
                                  0, seq // S5_CHUNK_T)
            ya, hrs, his = _s5_call(z, s5p[j], state_s5_re[j].reshape(nsm, n_state),
                                    state_s5_im[j].reshape(nsm, n_state), nsm, 1,
                                    mp_rows // nsm, 1, y_prev=ya)
            yb, hg = _hgrn_call(z, lbp[j], hg_norm_w[j],
                                jnp.zeros((HG_HEADS, bsz * HG_DIM, HG_DIM), F32), bsz, seq // HG_CHUNK_T)
            yb, hg_s = _hgrn_dec_call(z, mp_rows, nsm, lbp[j], hg_norm_w[j], state_hgrn, j, yb, hg_s)
            s5r_p.append(hr[:bsz].reshape(bsz, g_a, n_a))
            s5i_p.append(hi[:bsz].reshape(bsz, g_a, n_a))
            s5r_s.append(hrs.reshape(nsm, g_a, n_a))
            s5i_s.append(his.reshape(nsm, g_a, n_a))
            hg_p.append(jnp.transpose(hg.reshape(HG_HEADS, bsz, HG_DIM, HG_DIM), (1, 0, 3, 2)))
            x = _outproj_call(x, mods, nsm, l, [ya, yb], w_out_ab_bf, j, OUT_ROW_TILE)
        else:
            z = _inproj_call(x, norm_w, mods, nsm, w_in_c, l, j, IN_ROW_TILE, IN_COL_TILE)
            y, hl = _lru_call(z, lrup[j], jnp.zeros((4 * bsz, d_rnn), F32),
                              jnp.zeros((SUBLANES, d_rnn), F32), bsz, seq // LRU_CHUNK_T)
            y, hls = _lru_dec_call(z, mp_rows, lrup[j], jnp.transpose(state_conv[j], (1, 0, 2)),
                                   state_lru[j], y)
            lru_p.append(hl[:bsz])
            lru_s.append(hls)
            tail = z[mp_rows - (kw - 1) * bsz:mp_rows, d_rnn:]
            conv_p.append(jnp.transpose(tail.reshape(kw - 1, bsz, d_rnn), (1, 0, 2)))
            conv_s.append(jnp.concatenate([state_conv[j][:, 1:], z[mp_rows:, None, d_rnn:]], axis=1))
            x = _outproj_call(x, mods, nsm, l, [y], w_out_c_bf, j, OUT_ROW_TILE)
        x = _ffn_call(x, norm_w, mods, nsm, w_ffn_gu, w_ffn_d, l, 1, 2, ROW_TILE, 256)
    y_prompt, y_sample = _final_norm_call(x, final_norm_w.reshape(1, -1), bsz, seq, nsm, FINAL_TILE_T)
    return (y_prompt, y_sample.reshape(nsm, 1, d), jnp.stack(s5r_p), jnp.stack(s5i_p), jnp.stack(hg_p),
            jnp.stack(lru_p), jnp.stack(conv_p),
            jnp.stack(s5r_s), jnp.stack(s5i_s), hg_s, jnp.stack(lru_s), jnp.stack(conv_s))
```

```python
import functools

import numpy as np
import jax
import jax.numpy as jnp
from jax import lax
from jax.experimental import pallas as pl
from jax.experimental.pallas import tpu as pltpu

F32 = jnp.float32
BF16 = jnp.bfloat16
EPS = 1e-6
LRU_C = 8.0

VMEM_LIMIT_BYTES = 60 * 1024 * 1024
SUBLANES = 8
LANES = 128
BF16_ROWS = 16

MXU_COLS = 256

S5_COLS = 8
S5_CHUNK_T = 128
S5_SCAN_LANES = 1024
HG_HEADS = 8
HG_DIM = 128
HG_DIM_LOG2 = 7
HG_CHUNK_T = 64
LRU_CHUNK_T = 128
LRU_SUPER = 640
ROW_TILE = 1040
FFN_COL_TILE = MXU_COLS
OUT_ROW_TILE = 640
IN_ROW_TILE = 1664
IN_COL_TILE = 2 * MXU_COLS
ADA_COL_TILE = 2048
FINAL_TILE_T = 256


def _cp(*sem):
    return pltpu.CompilerParams(dimension_semantics=sem, vmem_limit_bytes=VMEM_LIMIT_BYTES)


def _dot(a, b):
    return jnp.dot(a, b, preferred_element_type=F32)


def _dot_nt(a, b):
    return lax.dot_general(a, b, (((1,), (1,)), ((), ())), preferred_element_type=F32)


def _dot_tn(a, b):
    return lax.dot_general(a, b, (((0,), (0,)), ((), ())), preferred_element_type=F32)


def _silu(x):
    return x * jax.nn.sigmoid(x)


def _gelu(x):
    return jax.nn.gelu(x, approximate=True)


def _log_sigmoid(x):
    return jnp.minimum(x, 0.0) - jnp.log1p(jnp.exp(-jnp.abs(x)))


def _block_diag_mask(nblk, rows, cols):
    r = np.arange(nblk * rows)[:, None] // rows
    c = np.arange(nblk * cols)[None, :] // cols
    return jnp.asarray(r == c, F32)


def _split3(x):
    hi = x.astype(BF16)
    r1 = x - hi.astype(F32)
    mid = r1.astype(BF16)
    lo = (r1 - mid.astype(F32)).astype(BF16)
    return hi, mid, lo


def _fma_rows(y, mul, add):
    rm = mul.shape[0]
    r, d = y.shape
    if rm == r:
        return y * mul + add
    y3 = y.reshape(r // rm, rm, d)
    return (y3 * mul[None] + add[None]).reshape(r, d)


def _mul_rows(y, mul):
    rm = mul.shape[0]
    r, d = y.shape
    if rm == r:
        return y * mul
    return (y.reshape(r // rm, rm, d) * mul[None]).reshape(r, d)


def _norm_mod_rows(x_ref, nw_ref, sh_ref, sc_ref, h_ref, row0, nrows):
    slab = BF16_ROWS
    per_row = sh_ref.shape[0] > SUBLANES
    nw = nw_ref[...]

    def body(s, carry):
        r0 = pl.multiple_of(row0 + s * slab, slab)
        x = x_ref[pl.ds(r0, slab), :]
        ms = jnp.mean(x * x, axis=-1, keepdims=True)
        y = x * lax.rsqrt(ms + EPS)
        if per_row:
            m0 = pl.multiple_of(s * slab, slab)
            h = y * nw * (1.0 + sc_ref[pl.ds(m0, slab), :]) + sh_ref[pl.ds(m0, slab), :]
        else:
            h = _fma_rows(y, nw * (1.0 + sc_ref[...]), sh_ref[...])
        h_ref[pl.ds(r0, slab), :] = h.astype(BF16)
        return carry

    lax.fori_loop(0, nrows // slab, body, 0, unroll=8)


def _residual_rows(o_ref, x_ref, acc, gt_ref, gts_ref, scale):
    tm = o_ref.shape[0]
    ns = gts_ref.shape[0]
    np_ = tm - ns
    is_last = pl.program_id(0) == pl.num_programs(0) - 1
    gp = scale * gt_ref[...]
    o_ref[0:np_, :] = x_ref[0:np_, :] + _mul_rows(acc(0, np_), gp)

    @pl.when(jnp.logical_not(is_last))
    def _():
        o_ref[np_:tm, :] = x_ref[np_:tm, :] + _mul_rows(acc(np_, tm), gp)

    @pl.when(is_last)
    def _():
        o_ref[np_:tm, :] = x_ref[np_:tm, :] + acc(np_, tm) * (scale * gts_ref[...])


def _ada_kernel(c_ref, w_ref, b_ref, o_ref):
    sc = _silu(c_ref[...]).astype(BF16)
    o_ref[0] = _dot(sc, w_ref[0].astype(BF16)) + b_ref[0]


def _ada_call(c_all, w_ada, b_ada, tn=ADA_COL_TILE):
    depth, d, n = w_ada.shape
    r = c_all.shape[0]
    return pl.pallas_call(
        _ada_kernel,
        out_shape=jax.ShapeDtypeStruct((depth, r, n), F32),
        grid=(depth, n // tn),
        in_specs=[pl.BlockSpec((r, d), lambda l, j: (0, 0)),
                  pl.BlockSpec((1, d, tn), lambda l, j: (l, 0, j)),
                  pl.BlockSpec((1, 1, tn), lambda l, j: (l, 0, j))],
        out_specs=pl.BlockSpec((1, r, tn), lambda l, j: (l, 0, j)),
        compiler_params=_cp("parallel", "parallel"),
        name="ada",
    )(c_all, w_ada, b_ada.reshape(depth, 1, n))


def _ffn_kernel(x_ref, nw_ref, sh_ref, sc_ref, gt_ref, shs_ref, scs_ref, gts_ref,
                wg_ref, wu_ref, wd_ref, o_ref, h_ref):
    i, f = pl.program_id(0), pl.program_id(1)
    nw_ref, sh_ref, sc_ref, gt_ref, shs_ref, scs_ref, gts_ref = (
        r.at[0] for r in (nw_ref, sh_ref, sc_ref, gt_ref, shs_ref, scs_ref, gts_ref))
    tm = x_ref.shape[0]
    ns = shs_ref.shape[0]
    is_last = i == pl.num_programs(0) - 1

    @pl.when(f == 0)
    def _():
        _norm_mod_rows(x_ref, nw_ref, sh_ref, sc_ref, h_ref, 0, tm)
        o_ref[...] = jnp.zeros_like(o_ref)

    @pl.when(jnp.logical_and(f == 0, is_last))
    def _():
        _norm_mod_rows(x_ref, nw_ref, shs_ref, scs_ref, h_ref, tm - ns, ns)

    h = h_ref[...]
    g = _dot(h, wg_ref[0, 0].astype(BF16))
    u = _dot(h, wu_ref[0, 0].astype(BF16))
    a = (_silu(g) * u).astype(BF16)
    o_ref[...] += _dot(a, wd_ref[0, 0].astype(BF16))

    @pl.when(f == pl.num_programs(1) - 1)
    def _():
        _residual_rows(o_ref, x_ref, lambda a0, a1: o_ref[a0:a1, :], gt_ref, gts_ref, 0.5)


def _mod_specs(mods, ns, d, l, ks):
    def pat(k):
        return pl.BlockSpec((1, SUBLANES, d), lambda *g: (l, ns // SUBLANES, k))

    def smp(k):
        return pl.BlockSpec((1, ns, d), lambda *g: (l, 0, k))

    return [pat(k) for k in ks] + [smp(k) for k in ks], [mods] * (2 * len(ks))


def _ffn_call(x, norm_w, mods, ns, w_gu, w_d, l, s, sub, tm, tf):
    m, d = x.shape
    ff = w_d.shape[2]
    nf = ff // tf
    row = lambda i, f: (i, 0)
    mspecs, margs = _mod_specs(mods, ns, d, l, (3 * sub, 3 * sub + 1, 3 * sub + 2))
    return pl.pallas_call(
        _ffn_kernel,
        out_shape=jax.ShapeDtypeStruct((m, d), F32),
        grid=(m // tm, nf),
        in_specs=[pl.BlockSpec((tm, d), row), pl.BlockSpec((1, 1, d), lambda i, f: (3 * l + sub, 0, 0))]
                 + mspecs +
                 [pl.BlockSpec((1, 1, d, tf), lambda i, f: (l, s, 0, f)),
                  pl.BlockSpec((1, 1, d, tf), lambda i, f: (l, s, 0, f + nf)),
                  pl.BlockSpec((1, 1, tf, d), lambda i, f: (l, s, f, 0))],
        out_specs=pl.BlockSpec((tm, d), row),
        scratch_shapes=[pltpu.VMEM((tm, d), BF16)],
        compiler_params=_cp("parallel", "arbitrary"),
        name="ffn",
    )(x, norm_w, *margs, w_gu, w_gu, w_d)


def _inproj_kernel(x_ref, nw_ref, sh_ref, sc_ref, shs_ref, scs_ref, w_ref, o_ref, h_ref):
    i, k = pl.program_id(0), pl.program_id(1)
    nw_ref, sh_ref, sc_ref, shs_ref, scs_ref = (
        r.at[0] for r in (nw_ref, sh_ref, sc_ref, shs_ref, scs_ref))
    tm = x_ref.shape[0]
    ns = shs_ref.shape[0]

    @pl.when(k == 0)
    def _():
        _norm_mod_rows(x_ref, nw_ref, sh_ref, sc_ref, h_ref, 0, tm)

    @pl.when(jnp.logical_and(k == 0, i == pl.num_programs(0) - 1))
    def _():
        _norm_mod_rows(x_ref, nw_ref, shs_ref, scs_ref, h_ref, tm - ns, ns)

    h = h_ref[...]
    for c in range(0, o_ref.shape[1], MXU_COLS):
        o_ref[:, c:c + MXU_COLS] = _dot(h, w_ref[0, :, c:c + MXU_COLS].astype(BF16))


def _inproj_call(x, norm_w, mods, ns, w, l, j, tm, tn):
    m, d = x.shape
    n = w.shape[2]
    mspecs, margs = _mod_specs(mods, ns, d, l, (3, 4))
    return pl.pallas_call(
        _inproj_kernel,
        out_shape=jax.ShapeDtypeStruct((m, n), F32),
        grid=(m // tm, n // tn),
        in_specs=[pl.BlockSpec((tm, d), lambda i, k: (i, 0)),
                  pl.BlockSpec((1, 1, d), lambda i, k: (3 * l + 1, 0, 0))]
                 + mspecs + [pl.BlockSpec((1, d, tn), lambda i, k: (j, 0, k))],
        out_specs=pl.BlockSpec((tm, tn), lambda i, k: (i, k)),
        scratch_shapes=[pltpu.VMEM((tm, d), BF16)],
        compiler_params=_cp("parallel", "arbitrary"),
        name="inproj",
    )(x, norm_w, *margs, w)


def _outproj_kernel(*refs, n_in):
    x_ref, gt_ref, gts_ref = refs[0], refs[1].at[0], refs[2].at[0]
    a_refs = refs[3:3 + n_in]
    w_refs = refs[3 + n_in:3 + 2 * n_in]
    o_ref = refs[3 + 2 * n_in]
    acc = _dot(a_refs[0][...], w_refs[0][0])
    for a_ref, w_ref in zip(a_refs[1:], w_refs[1:]):
        acc = acc + _dot(a_ref[...], w_ref[0])
    _residual_rows(o_ref, x_ref, lambda a0, a1: acc[a0:a1, :], gt_ref, gts_ref, 1.0)


def _outproj_call(x, mods, ns, l, acts, w_bf, j, tm):
    m, d = x.shape
    n_in = len(acts)
    ka = acts[0].shape[1]
    row = lambda i: (i, 0)
    mspecs, margs = _mod_specs(mods, ns, d, l, (5,))
    in_specs = [pl.BlockSpec((tm, d), row)] + mspecs
    in_specs += [pl.BlockSpec((tm, ka), row) for _ in acts]
    in_specs += [pl.BlockSpec((1, ka, d), functools.partial(lambda i, p: (j, p, 0), p=p)) for p in range(n_in)]
    return pl.pallas_call(
        functools.partial(_outproj_kernel, n_in=n_in),
        out_shape=jax.ShapeDtypeStruct((m, d), F32),
        grid=(m // tm,),
        in_specs=in_specs,
        out_specs=pl.BlockSpec((tm, d), row),
        compiler_params=_cp("parallel"),
        name="outproj",
    )(x, *margs, *acts, *([w_bf] * n_in))


def _pack_kernel(xp_ref, xs_ref, o_ref, slab_ref, *, n_tiles):
    i = pl.program_id(0)
    nb, tt, d = xp_ref.shape

    @pl.when(i < n_tiles)
    def _():
        for c in range(d // LANES):
            cs = slice(c * LANES, (c + 1) * LANES)
            for b in range(nb):
                slab_ref[c, pl.ds(b, tt, stride=nb), :] = xp_ref[b, :, cs]
            o_ref[:, cs] = slab_ref[c]

    @pl.when(i == n_tiles)
    def _():
        o_ref[0:xs_ref.shape[0], :] = xs_ref[...]


def _pack_call(x_prompt, xs, tt):
    nb, seq, d = x_prompt.shape
    ns = xs.shape[0]
    n_tiles = seq // tt
    return pl.pallas_call(
        functools.partial(_pack_kernel, n_tiles=n_tiles),
        out_shape=jax.ShapeDtypeStruct((seq * nb + ns, d), F32),
        grid=(n_tiles + 1,),
        in_specs=[pl.BlockSpec((nb, tt, d), lambda i: (0, jnp.minimum(i, n_tiles - 1), 0)),
                  pl.BlockSpec((ns, d), lambda i: (0, 0))],
        out_specs=pl.BlockSpec((tt * nb, d), lambda i: (i, 0)),
        scratch_shapes=[pltpu.VMEM((d // LANES, tt * nb, LANES), F32)],
        compiler_params=_cp("arbitrary"),
        name="pack",
    )(x_prompt, xs)


def _final_norm_kernel(x_ref, w_ref, yp_ref, ys_ref, slab_ref, *, n_tiles):
    i = pl.program_id(0)
    nb, tt, d = yp_ref.shape
    w = w_ref[...]

    def inv_rms(x):
        return lax.rsqrt(jnp.mean(x * x, axis=-1, keepdims=True) + EPS)

    @pl.when(i < n_tiles)
    def _():
        inv = inv_rms(x_ref[...])
        for c in range(d // LANES):
            cs = slice(c * LANES, (c + 1) * LANES)
            slab_ref[c] = x_ref[:, cs] * inv * w[:, cs]
            for b in range(nb):
                yp_ref[b, :, cs] = slab_ref[c, pl.ds(b, tt, stride=nb), :]

    @pl.when(i == n_tiles)
    def _():
        x = x_ref[0:ys_ref.shape[0], :]
        ys_ref[...] = x * inv_rms(x) * w


def _final_norm_call(x, w, nb, seq, ns, tt):
    d = x.shape[1]
    n_tiles = seq // tt
    return pl.pallas_call(
        functools.partial(_final_norm_kernel, n_tiles=n_tiles),
        out_shape=(jax.ShapeDtypeStruct((nb, seq, d), F32), jax.ShapeDtypeStruct((ns, d), F32)),
        grid=(n_tiles + 1,),
        in_specs=[pl.BlockSpec((tt * nb, d), lambda i: (i, 0)),
                  pl.BlockSpec((1, d), lambda i: (0, 0))],
        out_specs=(pl.BlockSpec((nb, tt, d), lambda i: (0, jnp.minimum(i, n_tiles - 1), 0)),
                   pl.BlockSpec((ns, d), lambda i: (0, 0))),
        scratch_shapes=[pltpu.VMEM((d // LANES, tt * nb, LANES), F32)],
        compiler_params=_cp("arbitrary"),
        name="final_norm",
    )(x, w)


def _s5_kernel(*refs, nb, tc, aliased):
    (u_ref, wb_ref, ar_ref, ai_ref, wcr_ref, wci_ref, d_ref, wglu_ref, bglu_ref,
     h0r_ref, h0i_ref) = refs[:11]
    o_ref, hr_ref, hi_ref, xr_ref, xi_ref = refs[12:] if aliased else refs[11:]

    @pl.when(pl.program_id(0) == 0)
    def _():
        hr_ref[...] = h0r_ref[...]
        hi_ref[...] = h0i_ref[...]

    u = u_ref[...]
    ub = u.astype(BF16)
    n_state = xr_ref.shape[1]
    cw = n_state // S5_COLS
    for c in range(S5_COLS):
        xc = _dot(ub[:, c * LANES:(c + 1) * LANES], wb_ref[c])
        xr_ref[:, c * cw:(c + 1) * cw] = xc[:, :cw]
        xi_ref[:, c * cw:(c + 1) * cw] = xc[:, cw:]

    lane_group = S5_SCAN_LANES
    for lo_ in range(0, n_state, lane_group):
        ls = pl.ds(lo_, lane_group)
        ar = ar_ref[:, ls]
        ai = ai_ref[:, ls]
        if tc == 1:
            hr, hi = hr_ref[:, ls], hi_ref[:, ls]
            nr = ar * hr - ai * hi + xr_ref[:, ls]
            ni = ar * hi + ai * hr + xi_ref[:, ls]
            xr_ref[:, ls] = nr
            xi_ref[:, ls] = ni
        else:
            lower = lax.broadcasted_iota(jnp.int32, (SUBLANES, lane_group), 0) < nb

            def pair(k, carry, ls=ls, ar=ar, ai=ai, lower=lower):
                hr, hi = carry
                rows = pl.ds(pl.multiple_of(k * SUBLANES, SUBLANES), SUBLANES)
                x_r, x_i = xr_ref[rows, ls], xi_ref[rows, ls]
                ar_ = ar * hr - ai * hi + x_r
                ai_ = ar * hi + ai * hr + x_i
                sr, si = pltpu.roll(ar_, nb, 0), pltpu.roll(ai_, nb, 0)
                br_ = ar * sr - ai * si + x_r
                bi_ = ar * si + ai * sr + x_i
                xr_ref[rows, ls] = jnp.where(lower, ar_, br_)
                xi_ref[rows, ls] = jnp.where(lower, ai_, bi_)
                return pltpu.roll(br_, nb, 0), pltpu.roll(bi_, nb, 0)

            nr, ni = lax.fori_loop(0, tc // 2, pair, (hr_ref[:, ls], hi_ref[:, ls]), unroll=2)
        hr_ref[:, ls] = nr
        hi_ref[:, ls] = ni

    ys = []
    for c in range(S5_COLS):
        cs = pl.ds(c * cw, cw)
        ys.append(_dot(xr_ref[:, cs].astype(BF16), wcr_ref[c])
                  + _dot(xi_ref[:, cs].astype(BF16), wci_ref[c]))
    y = jnp.concatenate(ys, axis=1) + d_ref[...] * u
    y = _gelu(y)
    z = _dot(y.astype(BF16), wglu_ref[...]) + bglu_ref[...]
    o_ref[...] = (y * jax.nn.sigmoid(z)).astype(BF16)


def _s5_call(z, prm, h0r, h0i, nb, tc, row_block0, n_chunks, y_prev=None):
    m = z.shape[0]
    d_a = prm["d"].shape[1]
    n_state = prm["ar"].shape[1]
    r = nb * tc
    sr = h0r.shape[0]
    fix2 = lambda i: (0, 0)
    fix3 = lambda i: (0, 0, 0)
    blk = lambda i: (row_block0 + i, 0)
    ins = [z, prm["wb"], prm["ar"], prm["ai"], prm["wcr"], prm["wci"], prm["d"], prm["wglu"],
           prm["bglu"], h0r, h0i]
    specs = [pl.BlockSpec((r, d_a), blk),
             pl.BlockSpec(prm["wb"].shape, fix3),
             pl.BlockSpec((1, n_state), fix2), pl.BlockSpec((1, n_state), fix2),
             pl.BlockSpec(prm["wcr"].shape, fix3), pl.BlockSpec(prm["wci"].shape, fix3),
             pl.BlockSpec((1, d_a), fix2), pl.BlockSpec((d_a, d_a), fix2),
             pl.BlockSpec((1, d_a), fix2),
             pl.BlockSpec((sr, n_state), fix2), pl.BlockSpec((sr, n_state), fix2)]
    aliases = {}
    if y_prev is not None:
        ins.append(y_prev)
        specs.append(pl.BlockSpec(memory_space=pl.ANY))
        aliases = {len(ins) - 1: 0}
    return pl.pallas_call(
        functools.partial(_s5_kernel, nb=nb, tc=tc, aliased=y_prev is not None),
        out_shape=(jax.ShapeDtypeStruct((m, d_a), BF16),
                   jax.ShapeDtypeStruct((sr, n_state), F32),
                   jax.ShapeDtypeStruct((sr, n_state), F32)),
        grid=(n_chunks,),
        in_specs=specs,
        out_specs=(pl.BlockSpec((r, d_a), blk),
                   pl.BlockSpec((sr, n_state), fix2), pl.BlockSpec((sr, n_state), fix2)),
        scratch_shapes=[pltpu.VMEM((r, n_state), F32), pltpu.VMEM((r, n_state), F32)],
        input_output_aliases=aliases,
        compiler_params=_cp("arbitrary"),
        name="s5",
    )(*ins)


def _s5_params(lam_re, lam_im, b_re, b_im, c_re, c_im, d_skip, log_step, w_glu, b_glu):
    g, n = lam_re.shape
    gs = b_re.shape[2]
    step = jnp.exp(log_step)[:, None]
    mag = jnp.exp(lam_re * step)
    abar_r, abar_i = mag * jnp.cos(lam_im * step), mag * jnp.sin(lam_im * step)
    den = lam_re * lam_re + lam_im * lam_im
    pr, pim = abar_r - 1.0, abar_i
    zr = (pr * lam_re + pim * lam_im) / den
    zi = (pim * lam_re - pr * lam_im) / den
    bbr = zr[..., None] * b_re - zi[..., None] * b_im
    bbi = zr[..., None] * b_im + zi[..., None] * b_re
    gl = LANES // gs
    nc = g // gl

    def in_mat(bb):
        a = jnp.transpose(bb.reshape(nc, gl, n, gs), (0, 1, 3, 2)).reshape(nc, gl * gs, n)
        return jnp.tile(a, (1, 1, gl)) * _block_diag_mask(gl, gs, n)

    def out_mat(cc):
        a = jnp.transpose(cc.reshape(nc, gl, gs, n), (0, 1, 3, 2)).reshape(nc, gl * n, gs)
        return jnp.tile(a, (1, 1, gl)) * _block_diag_mask(gl, n, gs)

    return dict(
        wb=jnp.concatenate([in_mat(bbr), in_mat(bbi)], axis=-1).astype(BF16),
        wcr=out_mat(c_re).astype(BF16),
        wci=out_mat(-c_im).astype(BF16),
        ar=abar_r.reshape(1, g * n), ai=abar_i.reshape(1, g * n),
        d=d_skip.reshape(1, -1), wglu=w_glu.astype(BF16), bglu=b_glu.reshape(1, -1))


def _log_f(fz, log_lb, log1m_lb):
    b = log1m_lb + _log_sigmoid(fz)
    a = jnp.broadcast_to(log_lb, b.shape)
    return jnp.maximum(a, b) + jnp.log1p(jnp.exp(-jnp.abs(a - b)))


def _hgrn_consts(tc, nb):
    r = tc * nb
    levels = int(np.log2(tc))
    t = np.arange(r) // nb
    b = np.arange(r) % nb
    same_b = b[:, None] == b[None, :]
    tt, ts = t[:, None], t[None, :]
    mask = np.zeros((levels + 1, r, r), np.float32)
    for l in range(levels):
        blk, half = 2 << l, 1 << l
        upper = (t % blk) >= half
        mask[l] = same_b & ((tt // blk) == (ts // blk)) & upper[:, None] & ~upper[None, :]
    expo = (same_b & (ts <= tt)).astype(np.float32)
    mask[levels] = np.eye(r, dtype=np.float32)
    bmask = np.zeros((nb, r, LANES), np.float32)
    for k in range(nb):
        bmask[k, b == k, :] = 1.0
    return expo, mask, bmask


def _hgrn_kernel(q_ref, fz_ref, v_ref, g_ref, llb_ref, l1m_ref, om_ref, gw_ref, expo_ref, mask_ref,
                 bm_ref, s0_ref, o_ref, st_ref, sc_ref, gs_ref, lf_ref, *, nb):
    levels = mask_ref.shape[0] - 1
    hd = HG_DIM
    r = q_ref.shape[0]

    @pl.when(pl.program_id(0) == 0)
    def _():
        st_ref[...] = s0_ref[...]
        sc_ref[...] = jnp.zeros_like(sc_ref)

    lf = _log_f(fz_ref[...], llb_ref[...], l1m_ref[...])
    lf_ref[...] = lf
    lf_hi = lf.astype(BF16)
    lf_lo = (lf - lf_hi.astype(F32)).astype(BF16)
    gs_ref[...] = _dot(expo_ref[...], lf_hi) + _dot(expo_ref[...], lf_lo)

    tiles = r // SUBLANES
    lower = lax.broadcasted_iota(jnp.int32, (1, SUBLANES, 1), 1) < nb

    def both_halves(t):
        return jnp.where(lower, pltpu.roll(t, nb, 1), t)

    gw = gw_ref[...]
    for h in range(HG_HEADS):
        hs = slice(h * hd, (h + 1) * hd)
        q = q_ref[:, hs]
        kk = om_ref[:, hs] * jax.nn.sigmoid(-fz_ref[:, hs])
        gcum = gs_ref[:, hs]
        g3 = gcum.reshape(tiles, SUBLANES, hd)

        def level_expo(l, g3=g3, hs=hs):
            if l == 0:
                return jnp.where(lower, 0.0, lf_ref[:, hs].reshape(tiles, SUBLANES, hd)).reshape(r, hd)
            span = 1 << l
            g4 = g3.reshape(tiles // span, span, SUBLANES, hd)
            split = both_halves(g4[:, span // 2 - 1])
            return (-jnp.abs(g4 - split[:, None])).reshape(r, hd)

        halves = (slice(0, r // 2), slice(r // 2, r))
        qb, kb = q.astype(BF16), kk.astype(BF16)
        acc = [mask_ref[levels, hv, hv] * _dot_nt(qb[hv], kb[hv]) for hv in halves]
        for l in range(levels - 1):
            e = jnp.exp(level_expo(l))
            qt, kt = (q * e).astype(BF16), (kk * e).astype(BF16)
            acc = [a + mask_ref[l, hv, hv] * _dot_nt(qt[hv], kt[hv]) for a, hv in zip(acc, halves)]
        e = jnp.exp(level_expo(levels - 1))
        lo_, hi_ = halves
        sc_ref[lo_, lo_] = acc[0]
        sc_ref[hi_, hi_] = acc[1]
        sc_ref[hi_, lo_] = mask_ref[levels - 1, hi_, lo_] * _dot_nt((q[hi_] * e[hi_]).astype(BF16),
                                                                   (kk[lo_] * e[lo_]).astype(BF16))

        qg = (q * jnp.exp(gcum)).astype(BF16)
        to_end = (both_halves(g3[tiles - 1:tiles]) - g3).reshape(r, hd)
        kend = (kk * jnp.exp(to_end)).astype(BF16)
        v = v_ref[:, hs]
        st = st_ref[h]
        o = _dot(sc_ref[...].astype(BF16), v.astype(BF16))
        oi = _dot_nt(qg, st.astype(BF16))
        for k in range(nb):
            o = o + bm_ref[k] * oi[:, k * hd:(k + 1) * hd]
        vcat = jnp.concatenate([(v * bm_ref[k]).astype(BF16) for k in range(nb)], axis=1)
        upd = _dot_tn(vcat, kend)
        dec = jnp.concatenate(
            [jnp.broadcast_to(jnp.exp(gs_ref[r - nb + k:r - nb + k + 1, hs]), (hd, hd)) for k in range(nb)],
            axis=0)
        st_ref[h] = dec * st + upd
        o = o * lax.rsqrt(jnp.mean(o * o, axis=-1, keepdims=True) + EPS) * gw
        o_ref[:, hs] = (o * _silu(g_ref[:, hs])).astype(BF16)


def _hgrn_call(z, lbp, gnorm_w, s0t, nb, n_chunks):
    m = z.shape[0]
    d_b = HG_HEADS * HG_DIM
    r = HG_CHUNK_T * nb
    expo, mask, bmask = _hgrn_consts(HG_CHUNK_T, nb)
    fix2 = lambda i: (0, 0)
    fix3 = lambda i: (0, 0, 0)
    col = lambda c: pl.BlockSpec((r, d_b), functools.partial(lambda i, c: (i, c), c=c))
    vec = pl.BlockSpec((1, d_b), fix2)
    return pl.pallas_call(
        functools.partial(_hgrn_kernel, nb=nb),
        out_shape=(jax.ShapeDtypeStruct((m, d_b), BF16),
                   jax.ShapeDtypeStruct(s0t.shape, F32)),
        grid=(n_chunks,),
        in_specs=[col(1), col(2), col(3), col(4), vec, vec, vec,
                  pl.BlockSpec((1, HG_DIM), fix2),
                  pl.BlockSpec(expo.shape, fix2), pl.BlockSpec(mask.shape, fix3),
                  pl.BlockSpec(bmask.shape, fix3), pl.BlockSpec(s0t.shape, fix3)],
        out_specs=(pl.BlockSpec((r, d_b), lambda i: (i, 0)), pl.BlockSpec(s0t.shape, fix3)),
        scratch_shapes=[pltpu.VMEM((r, r), F32), pltpu.VMEM((r, d_b), F32), pltpu.VMEM((r, d_b), F32)],
        compiler_params=_cp("arbitrary"),
        name="hgrn",
    )(z, z, z, z, lbp["log_lb"], lbp["log1m_lb"], lbp["one_m_lb"], gnorm_w.reshape(1, -1),
      jnp.asarray(expo, BF16), jnp.asarray(mask, F32), jnp.asarray(bmask, F32), s0t)


def _hgrn_dec_kernel(*refs, aliased):
    q_ref, fz_ref, v_ref, g_ref, llb_ref, l1m_ref, om_ref, gw_ref, fsel_ref, qsel_ref, s_ref = refs[:11]
    o_ref, so_ref = refs[13:] if aliased else refs[11:]
    hd = HG_DIM
    sb = q_ref.shape[0]
    fz = fz_ref[...]
    f = jnp.exp(_log_f(fz, llb_ref[...], l1m_ref[...]))
    kk = om_ref[...] * jax.nn.sigmoid(-fz)
    v = v_ref[...]
    gw = gw_ref[...]
    parts = [p.astype(F32) for p in _split3(f)]
    parts += [kk.astype(BF16).astype(F32), q_ref[...].astype(BF16).astype(F32)]
    n_f = 3 * sb
    zpad = jnp.zeros((hd - len(parts) * sb, hd), F32)
    wide = sb * hd
    own = (lax.broadcasted_iota(jnp.int32, (sb, wide), 0)
           == lax.shift_right_logical(lax.broadcasted_iota(jnp.int32, (sb, wide), 1), HG_DIM_LOG2))
    outs = []
    for h in range(HG_HEADS):
        hs = slice(h * hd, (h + 1) * hd)
        pt = jnp.concatenate([p[:, hs] for p in parts] + [zpad], axis=0).T.astype(BF16)
        vt = jnp.where(own, jnp.concatenate([v[:, hs]] * sb, axis=1), 0.0)
        vall = jnp.concatenate([jnp.zeros((n_f, wide), F32), vt,
                                jnp.zeros((hd - n_f - sb, wide), F32)], axis=0).astype(BF16)
        prod = _dot(pt, jnp.concatenate([fsel_ref[...], vall, qsel_ref[...]], axis=1))
        o_rows = []
        for j in range(sb):
            js = slice(j * hd, (j + 1) * hd)
            s_new = prod[:, js] * s_ref[0, j, h] + prod[:, wide:2 * wide][:, js]
            so_ref[0, j, h] = s_new
            o_rows.append(jnp.sum(prod[:, 2 * wide:][:, js] * s_new, axis=0, keepdims=True))
        o = jnp.concatenate(o_rows, axis=0)
        outs.append(o * lax.rsqrt(jnp.mean(o * o, axis=-1, keepdims=True) + EPS) * gw)
    o_all = jnp.concatenate(outs, axis=1)
    o_ref[...] = (o_all * _silu(g_ref[...])).astype(BF16)


def _hgrn_dec_call(z, row0, n_rows, lbp, gnorm_w, s_all, j, y_prev, s_prev, sb=SUBLANES):
    d_b = HG_HEADS * HG_DIM
    rb0 = row0 // sb
    fix2 = lambda i: (0, 0)
    col = lambda c: pl.BlockSpec((sb, d_b), functools.partial(lambda i, c: (rb0 + i, c), c=c))
    vec = pl.BlockSpec((1, d_b), fix2)
    sblk = pl.BlockSpec((1, sb, HG_HEADS, HG_DIM, HG_DIM), lambda i: (j, i, 0, 0, 0))
    fsel = np.zeros((HG_DIM, sb * HG_DIM), np.float32)
    qsel = np.zeros((HG_DIM, sb * HG_DIM), np.float32)
    for jj in range(sb):
        fsel[jj:3 * sb:sb, jj * HG_DIM:(jj + 1) * HG_DIM] = 1.0
        qsel[4 * sb + jj, jj * HG_DIM:(jj + 1) * HG_DIM] = 1.0
    sel = pl.BlockSpec(fsel.shape, fix2)
    ins = [z, z, z, z, lbp["log_lb"], lbp["log1m_lb"], lbp["one_m_lb"], gnorm_w.reshape(1, -1),
           jnp.asarray(fsel, BF16), jnp.asarray(qsel, BF16), s_all, y_prev]
    specs = [col(1), col(2), col(3), col(4), vec, vec, vec, pl.BlockSpec((1, HG_DIM), fix2),
             sel, sel, sblk, pl.BlockSpec(memory_space=pl.ANY)]
    aliases = {11: 0}
    if s_prev is not None:
        ins.append(s_prev)
        specs.append(pl.BlockSpec(memory_space=pl.ANY))
        aliases[12] = 1
    else:
        ins.append(jnp.zeros((SUBLANES, LANES), F32))
        specs.append(pl.BlockSpec((SUBLANES, LANES), fix2))
    return pl.pallas_call(
        functools.partial(_hgrn_dec_kernel, aliased=True),
        out_shape=(jax.ShapeDtypeStruct(y_prev.shape, BF16), jax.ShapeDtypeStruct(s_all.shape, F32)),
        grid=(n_rows // sb,),
        in_specs=specs,
        out_specs=(pl.BlockSpec((sb, d_b), lambda i: (rb0 + i, 0)), sblk),
        input_output_aliases=aliases,
        compiler_params=_cp("parallel"),
        name="hgrn_dec",
    )(*ins)


def _lru_coeffs(conv_block, wg_ref, bga_ref, bgx_ref, lam_ref, a_ref, b_ref):
    sw = LRU_SUPER
    for j in range(a_ref.shape[1] // sw):
        cs = slice(j * sw, (j + 1) * sw)
        xj = conv_block(cs)
        gj = _dot(xj.astype(BF16), wg_ref[j])
        rg = jax.nn.sigmoid(gj[:, :sw] + bga_ref[:, cs])
        ig = jax.nn.sigmoid(gj[:, sw:] + bgx_ref[:, cs])
        log_a = LRU_C * rg * _log_sigmoid(lam_ref[:, cs])
        a = jnp.exp(log_a)
        a_ref[:, cs] = a
        b_ref[:, cs] = jnp.sqrt(-jnp.tanh(log_a) * (a * a + 1.0)) * ig * xj


def _lru_kernel(gb_ref, xr_ref, cw_ref, cb_ref, wg_ref, bga_ref, bgx_ref, lam_ref, c0_ref, h0_ref,
                o_ref, hl_ref, xs_ref, a_ref, b_ref, *, nb):
    r = xr_ref.shape[0]
    hdr = xs_ref.shape[0] - r
    kw = cw_ref.shape[0]

    @pl.when(pl.program_id(0) == 0)
    def _():
        xs_ref[0:hdr, :] = c0_ref[...]
        hl_ref[...] = h0_ref[...]

    xs_ref[hdr:hdr + r, :] = xr_ref[...]
    n_win = hdr + r

    def conv_block(cs):
        win = xs_ref[:, cs]
        sh = pltpu.roll(win, n_win - nb, 0)
        taps = [sh[hdr - kw * nb:hdr - kw * nb + r], win[hdr - 2 * nb:hdr - 2 * nb + r],
                sh[hdr - 2 * nb:hdr - 2 * nb + r], win[hdr:hdr + r]]
        xc = cb_ref[:, cs] + taps[0] * cw_ref[0:1, cs]
        for j in range(1, kw):
            xc = xc + taps[j] * cw_ref[j:j + 1, cs]
        return xc

    _lru_coeffs(conv_block, wg_ref, bga_ref, bgx_ref, lam_ref, a_ref, b_ref)
    xs_ref[0:hdr, :] = xs_ref[r:r + hdr, :]

    lower = lax.broadcasted_iota(jnp.int32, (SUBLANES, a_ref.shape[1]), 0) < nb

    def pair(k, h):
        rows = pl.ds(pl.multiple_of(k * SUBLANES, SUBLANES), SUBLANES)
        a8 = a_ref[rows, :]
        b8 = b_ref[rows, :]
        h_a = a8 * h + b8
        h_b = a8 * pltpu.roll(h_a, nb, 0) + b8
        b_ref[rows, :] = jnp.where(lower, h_a, h_b)
        return pltpu.roll(h_b, nb, 0)

    hl_ref[...] = lax.fori_loop(0, r // SUBLANES, pair, hl_ref[...], unroll=2)
    o_ref[...] = (_gelu(gb_ref[...]) * b_ref[...]).astype(BF16)


def _lru_call(z, prm, conv0, h0, nb, n_chunks):
    m = z.shape[0]
    d_rnn = prm["lam"].shape[1]
    r = nb * LRU_CHUNK_T
    hdr = conv0.shape[0]
    fix2 = lambda i: (0, 0)
    fix3 = lambda i: (0, 0, 0)
    vec = pl.BlockSpec((1, d_rnn), fix2)
    return pl.pallas_call(
        functools.partial(_lru_kernel, nb=nb),
        out_shape=(jax.ShapeDtypeStruct((m, d_rnn), BF16),
                   jax.ShapeDtypeStruct((SUBLANES, d_rnn), F32)),
        grid=(n_chunks,),
        in_specs=[pl.BlockSpec((r, d_rnn), lambda i: (i, 0)),
                  pl.BlockSpec((r, d_rnn), lambda i: (i, 1)),
                  pl.BlockSpec(prm["cw"].shape, fix2), vec,
                  pl.BlockSpec(prm["wg"].shape, fix3), vec, vec, vec,
                  pl.BlockSpec((hdr, d_rnn), fix2), pl.BlockSpec((SUBLANES, d_rnn), fix2)],
        out_specs=(pl.BlockSpec((r, d_rnn), lambda i: (i, 0)),
                   pl.BlockSpec((SUBLANES, d_rnn), fix2)),
        scratch_shapes=[pltpu.VMEM((hdr + r, d_rnn), F32), pltpu.VMEM((r, d_rnn), F32),
                        pltpu.VMEM((r, d_rnn), F32)],
        compiler_params=_cp("arbitrary"),
        name="lru",
    )(z, z, prm["cw"], prm["cb"], prm["wg"], prm["bga"], prm["bgx"], prm["lam"], conv0, h0)


def _lru_dec_kernel(gb_ref, xr_ref, buf_ref, cw_ref, cb_ref, wg_ref, bga_ref, bgx_ref, lam_ref,
                    h0_ref, yp_ref, o_ref, h_ref, a_ref, b_ref):
    kw = cw_ref.shape[0]

    def conv_block(cs):
        xc = cb_ref[:, cs] + xr_ref[:, cs] * cw_ref[kw - 1:kw, cs]
        for j in range(kw - 1):
            xc = xc + buf_ref[j, :, cs] * cw_ref[j:j + 1, cs]
        return xc

    _lru_coeffs(conv_block, wg_ref, bga_ref, bgx_ref, lam_ref, a_ref, b_ref)
    h = a_ref[...] * h0_ref[...] + b_ref[...]
    h_ref[...] = h
    o_ref[...] = (_gelu(gb_ref[...]) * h).astype(BF16)


def _lru_dec_call(z, row0, prm, buf, h0, y_prev):
    n = h0.shape[0]
    d_rnn = prm["lam"].shape[1]
    rb0 = row0 // n
    fix2 = lambda i: (0, 0)
    fix3 = lambda i: (0, 0, 0)
    vec = pl.BlockSpec((1, d_rnn), fix2)
    full = pl.BlockSpec((n, d_rnn), fix2)
    return pl.pallas_call(
        _lru_dec_kernel,
        out_shape=(jax.ShapeDtypeStruct(y_prev.shape, BF16), jax.ShapeDtypeStruct((n, d_rnn), F32)),
        grid=(1,),
        in_specs=[pl.BlockSpec((n, d_rnn), lambda i: (rb0, 0)),
                  pl.BlockSpec((n, d_rnn), lambda i: (rb0, 1)),
                  pl.BlockSpec(buf.shape, fix3),
                  pl.BlockSpec(prm["cw"].shape, fix2), vec,
                  pl.BlockSpec(prm["wg"].shape, fix3), vec, vec, vec, full,
                  pl.BlockSpec(memory_space=pl.ANY)],
        out_specs=(pl.BlockSpec((n, d_rnn), lambda i: (rb0, 0)), full),
        scratch_shapes=[pltpu.VMEM((n, d_rnn), F32), pltpu.VMEM((n, d_rnn), F32)],
        input_output_aliases={10: 0},
        compiler_params=_cp("arbitrary"),
        name="lru_dec",
    )(z, z, buf, prm["cw"], prm["cb"], prm["wg"], prm["bga"], prm["bgx"], prm["lam"], h0, y_prev)


def _lru_params(conv_w, conv_b, w_ga, b_ga, w_gx, b_gx, lam):
    nblk, bs = w_ga.shape[0], w_ga.shape[1]
    per = LRU_SUPER // bs
    ns = nblk // per

    def sup(w):
        return jnp.tile(w.reshape(ns, per * bs, bs), (1, 1, per)) * _block_diag_mask(per, bs, bs)

    return dict(cw=conv_w, cb=conv_b.reshape(1, -1),
                wg=jnp.concatenate([sup(w_ga), sup(w_gx)], axis=-1).astype(BF16),
                bga=b_ga.reshape(1, -1), bgx=b_gx.reshape(1, -1), lam=lam.reshape(1, -1))


def kernel(x_prompt, x_sample, state_s5_re, state_s5_im, state_hgrn, state_lru, state_conv,
           c_prompt, c_sample, norm_w, final_norm_w, w_ada, b_ada, w_ffn_gu, w_ffn_d,
           w_in_ab, s5_lam_re, s5_lam_im, s5_b_re, s5_b_im, s5_c_re, s5_c_im, s5_d, s5_log_step,
           s5_w_glu, s5_b_glu, hg_lb_logits, hg_norm_w, w_out_ab, w_in_c, conv_w, conv_b,
           w_gate_a, b_gate_a, w_gate_x, b_gate_x, lru_lambda, w_out_c):
    bsz, seq, d = x_prompt.shape
    nsm = x_sample.shape[0]
    depth = w_ada.shape[0]
    n_ab, n_c = w_in_ab.shape[0], w_in_c.shape[0]
    g_a, n_a = s5_lam_re.shape[1], s5_lam_re.shape[2]
    n_state = g_a * n_a
    d_rnn = lru_lambda.shape[1]
    kw = conv_w.shape[1]
    mp_rows = seq * bsz
    assert 2 * bsz == SUBLANES and mp_rows % nsm == 0
    assert all((mp_rows + nsm) % t == 0 and t > nsm for t in (ROW_TILE, IN_ROW_TILE, OUT_ROW_TILE))

    s5p = [_s5_params(s5_lam_re[j], s5_lam_im[j], s5_b_re[j], s5_b_im[j], s5_c_re[j], s5_c_im[j],
                      s5_d[j], s5_log_step[j], s5_w_glu[j], s5_b_glu[j]) for j in range(n_ab)]
    lb_all = jnp.cumsum(jax.nn.softmax(hg_lb_logits.astype(F32), axis=0), axis=0)
    lb_all = lb_all - lb_all[0:1]
    lbp = [dict(log_lb=jnp.log(lb_all[j]).reshape(1, -1),
                log1m_lb=jnp.log1p(-lb_all[j]).reshape(1, -1),
                one_m_lb=(1.0 - lb_all[j]).reshape(1, -1)) for j in range(n_ab)]
    lrup = [_lru_params(conv_w[j], conv_b[j], w_gate_a[j], b_gate_a[j], w_gate_x[j], b_gate_x[j],
                        lru_lambda[j]) for j in range(n_c)]
    w_out_ab_bf = w_out_ab.astype(BF16)
    w_out_c_bf = w_out_c.astype(BF16)
    norm_w = norm_w.reshape(depth * 3, 1, d)

    c_all = jnp.concatenate([c_sample] + [c_prompt] * (SUBLANES // bsz), axis=0)
    mods = _ada_call(c_all, w_ada, b_ada)

    x = _pack_call(x_prompt, x_sample.reshape(nsm, d), FINAL_TILE_T)
    zeros_state = jnp.zeros((SUBLANES, n_state), F32)
    s5r_p, s5i_p, hg_p, lru_p, conv_p = [], [], [], [], []
    s5r_s, s5i_s, lru_s, conv_s = [], [], [], []
    hg_s = None
    for l in range(depth):
        j = l // 2
        x = _ffn_call(x, norm_w, mods, nsm, w_ffn_gu, w_ffn_d, l, 0, 0, ROW_TILE, FFN_COL_TILE)
        if l % 2 == 0:
            z = _inproj_call(x, norm_w, mods, nsm, w_in_ab, l, j, IN_ROW_TILE, IN_COL_TILE)
            ya, hr, hi = _s5_call(z, s5p[j], zeros_state, zeros_state, bsz, S5_CHUNK_T,
                                  0, seq // S5_CHUNK_T)
            ya, hrs, his = _s5_call(z, s5p[j], state_s5_re[j].reshape(nsm, n_state),
                                    state_s5_im[j].reshape(nsm, n_state), nsm, 1,
                                    mp_rows // nsm, 1, y_prev=ya)
            yb, hg = _hgrn_call(z, lbp[j], hg_norm_w[j],
                                jnp.zeros((HG_HEADS, bsz * HG_DIM, HG_DIM), F32), bsz, seq // HG_CHUNK_T)
            yb, hg_s = _hgrn_dec_call(z, mp_rows, nsm, lbp[j], hg_norm_w[j], state_hgrn, j, yb, hg_s)
            s5r_p.append(hr[:bsz].reshape(bsz, g_a, n_a))
            s5i_p.append(hi[:bsz].reshape(bsz, g_a, n_a))
            s5r_s.append(hrs.reshape(nsm, g_a, n_a))
            s5i_s.append(his.reshape(nsm, g_a, n_a))
            hg_p.append(jnp.transpose(hg.reshape(HG_HEADS, bsz, HG_DIM, HG_DIM), (1, 0, 3, 2)))
            x = _outproj_call(x, mods, nsm, l, [ya, yb], w_out_ab_bf, j, OUT_ROW_TILE)
        else:
            z = _inproj_call(x, norm_w, mods, nsm, w_in_c, l, j, IN_ROW_TILE, IN_COL_TILE)
            y, hl = _lru_call(z, lrup[j], jnp.zeros((4 * bsz, d_rnn), F32),
                              jnp.zeros((SUBLANES, d_rnn), F32), bsz, seq // LRU_CHUNK_T)
            y, hls = _lru_dec_call(z, mp_rows, lrup[j], jnp.transpose(state_conv[j], (1, 0, 2)),
                                   state_lru[j], y)
            lru_p.append(hl[:bsz])
            lru_s.append(hls)
            tail = z[mp_rows - (kw - 1) * bsz:mp_rows, d_rnn:]
            conv_p.append(jnp.transpose(tail.reshape(kw - 1, bsz, d_rnn), (1, 0, 2)))
            conv_s.append(jnp.concatenate([state_conv[j][:, 1:], z[mp_rows:, None, d_rnn:]], axis=1))
            x = _outproj_call(x, mods, nsm, l, [y], w_out_c_bf, j, OUT_ROW_TILE)
        x = _ffn_call(x, norm_w, mods, nsm, w_ffn_gu, w_ffn_d, l, 1, 2, ROW_TILE, FFN_COL_TILE)
    y_prompt, y_sample = _final_norm_call(x, final_norm_w.reshape(1, -1), bsz, seq, nsm, FINAL_TILE_T)
    return (y_prompt, y_sample.reshape(nsm, 1, d), jnp.stack(s5r_p), jnp.stack(s5i_p), jnp.stack(hg_p),
            jnp.stack(lru_p), jnp.stack(conv_p),
            jnp.stack(s5r_s), jnp.stack(s5i_s), hg_s, jnp.stack(lru_s), jnp.stack(conv_s))
```

```python
import functools

import numpy as np
import jax
import jax.numpy as jnp
from jax import lax
from jax.experimental import pallas as pl
from jax.experimental.pallas import tpu as pltpu

F32 = jnp.float32
BF16 = jnp.bfloat16
EPS = 1e-6
LRU_C = 8.0

VMEM_LIMIT_BYTES = 60 * 1024 * 1024
SUBLANES = 8
LANES = 128
BF16_ROWS = 16

MXU_COLS = 256

S5_COLS = 8
S5_CHUNK_T = 128
S5_SCAN_LANES = 1024
HG_HEADS = 8
HG_DIM = 128
HG_DIM_LOG2 = 7
HG_CHUNK_T = 64
LRU_CHUNK_T = 64
LRU_SUPER = 640
ROW_TILE = 1040
FFN_COL_TILE = MXU_COLS
OUT_ROW_TILE = 640
IN_ROW_TILE = 1664
IN_COL_TILE = 2 * MXU_COLS
ADA_COL_TILE = 2048
FINAL_TILE_T = 256


def _cp(*sem):
    return pltpu.CompilerParams(dimension_semantics=sem, vmem_limit_bytes=VMEM_LIMIT_BYTES)


def _dot(a, b):
    return jnp.dot(a, b, preferred_element_type=F32)


def _dot_nt(a, b):
    return lax.dot_general(a, b, (((1,), (1,)), ((), ())), preferred_element_type=F32)


def _dot_tn(a, b):
    return lax.dot_general(a, b, (((0,), (0,)), ((), ())), preferred_element_type=F32)


def _silu(x):
    return x * jax.nn.sigmoid(x)


def _gelu(x):
    return jax.nn.gelu(x, approximate=True)


def _log_sigmoid(x):
    return jnp.minimum(x, 0.0) - jnp.log1p(jnp.exp(-jnp.abs(x)))


def _block_diag_mask(nblk, rows, cols):
    r = np.arange(nblk * rows)[:, None] // rows
    c = np.arange(nblk * cols)[None, :] // cols
    return jnp.asarray(r == c, F32)


def _split3(x):
    hi = x.astype(BF16)
    r1 = x - hi.astype(F32)
    mid = r1.astype(BF16)
    lo = (r1 - mid.astype(F32)).astype(BF16)
    return hi, mid, lo


def _fma_rows(y, mul, add):
    rm = mul.shape[0]
    r, d = y.shape
    if rm == r:
        return y * mul + add
    y3 = y.reshape(r // rm, rm, d)
    return (y3 * mul[None] + add[None]).reshape(r, d)


def _mul_rows(y, mul):
    rm = mul.shape[0]
    r, d = y.shape
    if rm == r:
        return y * mul
    return (y.reshape(r // rm, rm, d) * mul[None]).reshape(r, d)


def _norm_mod_rows(x_ref, nw_ref, sh_ref, sc_ref, h_ref, row0, nrows):
    slab = BF16_ROWS
    per_row = sh_ref.shape[0] > SUBLANES
    nw = nw_ref[...]

    def body(s, carry):
        r0 = pl.multiple_of(row0 + s * slab, slab)
        x = x_ref[pl.ds(r0, slab), :]
        ms = jnp.mean(x * x, axis=-1, keepdims=True)
        y = x * lax.rsqrt(ms + EPS)
        if per_row:
            m0 = pl.multiple_of(s * slab, slab)
            h = y * nw * (1.0 + sc_ref[pl.ds(m0, slab), :]) + sh_ref[pl.ds(m0, slab), :]
        else:
            h = _fma_rows(y, nw * (1.0 + sc_ref[...]), sh_ref[...])
        h_ref[pl.ds(r0, slab), :] = h.astype(BF16)
        return carry

    lax.fori_loop(0, nrows // slab, body, 0, unroll=8)


def _residual_rows(o_ref, x_ref, acc, gt_ref, gts_ref, scale):
    tm = o_ref.shape[0]
    ns = gts_ref.shape[0]
    np_ = tm - ns
    is_last = pl.program_id(0) == pl.num_programs(0) - 1
    gp = scale * gt_ref[...]
    o_ref[0:np_, :] = x_ref[0:np_, :] + _mul_rows(acc(0, np_), gp)

    @pl.when(jnp.logical_not(is_last))
    def _():
        o_ref[np_:tm, :] = x_ref[np_:tm, :] + _mul_rows(acc(np_, tm), gp)

    @pl.when(is_last)
    def _():
        o_ref[np_:tm, :] = x_ref[np_:tm, :] + acc(np_, tm) * (scale * gts_ref[...])


def _ada_kernel(c_ref, w_ref, b_ref, o_ref):
    sc = _silu(c_ref[...]).astype(BF16)
    o_ref[0] = _dot(sc, w_ref[0].astype(BF16)) + b_ref[0]


def _ada_call(c_all, w_ada, b_ada, tn=ADA_COL_TILE):
    depth, d, n = w_ada.shape
    r = c_all.shape[0]
    return pl.pallas_call(
        _ada_kernel,
        out_shape=jax.ShapeDtypeStruct((depth, r, n), F32),
        grid=(depth, n // tn),
        in_specs=[pl.BlockSpec((r, d), lambda l, j: (0, 0)),
                  pl.BlockSpec((1, d, tn), lambda l, j: (l, 0, j)),
                  pl.BlockSpec((1, 1, tn), lambda l, j: (l, 0, j))],
        out_specs=pl.BlockSpec((1, r, tn), lambda l, j: (l, 0, j)),
        compiler_params=_cp("parallel", "parallel"),
        name="ada",
    )(c_all, w_ada, b_ada.reshape(depth, 1, n))


def _ffn_kernel(x_ref, nw_ref, sh_ref, sc_ref, gt_ref, shs_ref, scs_ref, gts_ref,
                wg_ref, wu_ref, wd_ref, o_ref, h_ref):
    i, f = pl.program_id(0), pl.program_id(1)
    nw_ref, sh_ref, sc_ref, gt_ref, shs_ref, scs_ref, gts_ref = (
        r.at[0] for r in (nw_ref, sh_ref, sc_ref, gt_ref, shs_ref, scs_ref, gts_ref))
    tm = x_ref.shape[0]
    ns = shs_ref.shape[0]
    is_last = i == pl.num_programs(0) - 1

    @pl.when(f == 0)
    def _():
        _norm_mod_rows(x_ref, nw_ref, sh_ref, sc_ref, h_ref, 0, tm)
        o_ref[...] = jnp.zeros_like(o_ref)

    @pl.when(jnp.logical_and(f == 0, is_last))
    def _():
        _norm_mod_rows(x_ref, nw_ref, shs_ref, scs_ref, h_ref, tm - ns, ns)

    h = h_ref[...]
    g = _dot(h, wg_ref[0, 0].astype(BF16))
    u = _dot(h, wu_ref[0, 0].astype(BF16))
    a = (_silu(g) * u).astype(BF16)
    o_ref[...] += _dot(a, wd_ref[0, 0].astype(BF16))

    @pl.when(f == pl.num_programs(1) - 1)
    def _():
        _residual_rows(o_ref, x_ref, lambda a0, a1: o_ref[a0:a1, :], gt_ref, gts_ref, 0.5)


def _mod_specs(mods, ns, d, l, ks):
    def pat(k):
        return pl.BlockSpec((1, SUBLANES, d), lambda *g: (l, ns // SUBLANES, k))

    def smp(k):
        return pl.BlockSpec((1, ns, d), lambda *g: (l, 0, k))

    return [pat(k) for k in ks] + [smp(k) for k in ks], [mods] * (2 * len(ks))


def _ffn_call(x, norm_w, mods, ns, w_gu, w_d, l, s, sub, tm, tf):
    m, d = x.shape
    ff = w_d.shape[2]
    nf = ff // tf
    row = lambda i, f: (i, 0)
    mspecs, margs = _mod_specs(mods, ns, d, l, (3 * sub, 3 * sub + 1, 3 * sub + 2))
    return pl.pallas_call(
        _ffn_kernel,
        out_shape=jax.ShapeDtypeStruct((m, d), F32),
        grid=(m // tm, nf),
        in_specs=[pl.BlockSpec((tm, d), row), pl.BlockSpec((1, 1, d), lambda i, f: (3 * l + sub, 0, 0))]
                 + mspecs +
                 [pl.BlockSpec((1, 1, d, tf), lambda i, f: (l, s, 0, f)),
                  pl.BlockSpec((1, 1, d, tf), lambda i, f: (l, s, 0, f + nf)),
                  pl.BlockSpec((1, 1, tf, d), lambda i, f: (l, s, f, 0))],
        out_specs=pl.BlockSpec((tm, d), row),
        scratch_shapes=[pltpu.VMEM((tm, d), BF16)],
        compiler_params=_cp("parallel", "arbitrary"),
        name="ffn",
    )(x, norm_w, *margs, w_gu, w_gu, w_d)


def _inproj_kernel(x_ref, nw_ref, sh_ref, sc_ref, shs_ref, scs_ref, w_ref, o_ref, h_ref):
    i, k = pl.program_id(0), pl.program_id(1)
    nw_ref, sh_ref, sc_ref, shs_ref, scs_ref = (
        r.at[0] for r in (nw_ref, sh_ref, sc_ref, shs_ref, scs_ref))
    tm = x_ref.shape[0]
    ns = shs_ref.shape[0]

    @pl.when(k == 0)
    def _():
        _norm_mod_rows(x_ref, nw_ref, sh_ref, sc_ref, h_ref, 0, tm)

    @pl.when(jnp.logical_and(k == 0, i == pl.num_programs(0) - 1))
    def _():
        _norm_mod_rows(x_ref, nw_ref, shs_ref, scs_ref, h_ref, tm - ns, ns)

    h = h_ref[...]
    for c in range(0, o_ref.shape[1], MXU_COLS):
        o_ref[:, c:c + MXU_COLS] = _dot(h, w_ref[0, :, c:c + MXU_COLS].astype(BF16))


def _inproj_call(x, norm_w, mods, ns, w, l, j, tm, tn):
    m, d = x.shape
    n = w.shape[2]
    mspecs, margs = _mod_specs(mods, ns, d, l, (3, 4))
    return pl.pallas_call(
        _inproj_kernel,
        out_shape=jax.ShapeDtypeStruct((m, n), F32),
        grid=(m // tm, n // tn),
        in_specs=[pl.BlockSpec((tm, d), lambda i, k: (i, 0)),
                  pl.BlockSpec((1, 1, d), lambda i, k: (3 * l + 1, 0, 0))]
                 + mspecs + [pl.BlockSpec((1, d, tn), lambda i, k: (j, 0, k))],
        out_specs=pl.BlockSpec((tm, tn), lambda i, k: (i, k)),
        scratch_shapes=[pltpu.VMEM((tm, d), BF16)],
        compiler_params=_cp("parallel", "arbitrary"),
        name="inproj",
    )(x, norm_w, *margs, w)


def _outproj_kernel(*refs, n_in):
    x_ref, gt_ref, gts_ref = refs[0], refs[1].at[0], refs[2].at[0]
    a_refs = refs[3:3 + n_in]
    w_refs = refs[3 + n_in:3 + 2 * n_in]
    o_ref = refs[3 + 2 * n_in]
    acc = _dot(a_refs[0][...], w_refs[0][0])
    for a_ref, w_ref in zip(a_refs[1:], w_refs[1:]):
        acc = acc + _dot(a_ref[...], w_ref[0])
    _residual_rows(o_ref, x_ref, lambda a0, a1: acc[a0:a1, :], gt_ref, gts_ref, 1.0)


def _outproj_call(x, mods, ns, l, acts, w_bf, j, tm):
    m, d = x.shape
    n_in = len(acts)
    ka = acts[0].shape[1]
    row = lambda i: (i, 0)
    mspecs, margs = _mod_specs(mods, ns, d, l, (5,))
    in_specs = [pl.BlockSpec((tm, d), row)] + mspecs
    in_specs += [pl.BlockSpec((tm, ka), row) for _ in acts]
    in_specs += [pl.BlockSpec((1, ka, d), functools.partial(lambda i, p: (j, p, 0), p=p)) for p in range(n_in)]
    return pl.pallas_call(
        functools.partial(_outproj_kernel, n_in=n_in),
        out_shape=jax.ShapeDtypeStruct((m, d), F32),
        grid=(m // tm,),
        in_specs=in_specs,
        out_specs=pl.BlockSpec((tm, d), row),
        compiler_params=_cp("parallel"),
        name="outproj",
    )(x, *margs, *acts, *([w_bf] * n_in))


def _pack_kernel(xp_ref, xs_ref, o_ref, slab_ref, *, n_tiles):
    i = pl.program_id(0)
    nb, tt, d = xp_ref.shape

    @pl.when(i < n_tiles)
    def _():
        for c in range(d // LANES):
            cs = slice(c * LANES, (c + 1) * LANES)
            for b in range(nb):
                slab_ref[c, pl.ds(b, tt, stride=nb), :] = xp_ref[b, :, cs]
            o_ref[:, cs] = slab_ref[c]

    @pl.when(i == n_tiles)
    def _():
        o_ref[0:xs_ref.shape[0], :] = xs_ref[...]


def _pack_call(x_prompt, xs, tt):
    nb, seq, d = x_prompt.shape
    ns = xs.shape[0]
    n_tiles = seq // tt
    return pl.pallas_call(
        functools.partial(_pack_kernel, n_tiles=n_tiles),
        out_shape=jax.ShapeDtypeStruct((seq * nb + ns, d), F32),
        grid=(n_tiles + 1,),
        in_specs=[pl.BlockSpec((nb, tt, d), lambda i: (0, jnp.minimum(i, n_tiles - 1), 0)),
                  pl.BlockSpec((ns, d), lambda i: (0, 0))],
        out_specs=pl.BlockSpec((tt * nb, d), lambda i: (i, 0)),
        scratch_shapes=[pltpu.VMEM((d // LANES, tt * nb, LANES), F32)],
        compiler_params=_cp("arbitrary"),
        name="pack",
    )(x_prompt, xs)


def _final_norm_kernel(x_ref, w_ref, yp_ref, ys_ref, slab_ref, *, n_tiles):
    i = pl.program_id(0)
    nb, tt, d = yp_ref.shape
    w = w_ref[...]

    def inv_rms(x):
        return lax.rsqrt(jnp.mean(x * x, axis=-1, keepdims=True) + EPS)

    @pl.when(i < n_tiles)
    def _():
        inv = inv_rms(x_ref[...])
        for c in range(d // LANES):
            cs = slice(c * LANES, (c + 1) * LANES)
            slab_ref[c] = x_ref[:, cs] * inv * w[:, cs]
            for b in range(nb):
                yp_ref[b, :, cs] = slab_ref[c, pl.ds(b, tt, stride=nb), :]

    @pl.when(i == n_tiles)
    def _():
        x = x_ref[0:ys_ref.shape[0], :]
        ys_ref[...] = x * inv_rms(x) * w


def _final_norm_call(x, w, nb, seq, ns, tt):
    d = x.shape[1]
    n_tiles = seq // tt
    return pl.pallas_call(
        functools.partial(_final_norm_kernel, n_tiles=n_tiles),
        out_shape=(jax.ShapeDtypeStruct((nb, seq, d), F32), jax.ShapeDtypeStruct((ns, d), F32)),
        grid=(n_tiles + 1,),
        in_specs=[pl.BlockSpec((tt * nb, d), lambda i: (i, 0)),
                  pl.BlockSpec((1, d), lambda i: (0, 0))],
        out_specs=(pl.BlockSpec((nb, tt, d), lambda i: (0, jnp.minimum(i, n_tiles - 1), 0)),
                   pl.BlockSpec((ns, d), lambda i: (0, 0))),
        scratch_shapes=[pltpu.VMEM((d // LANES, tt * nb, LANES), F32)],
        compiler_params=_cp("arbitrary"),
        name="final_norm",
    )(x, w)


def _s5_kernel(*refs, nb, tc, aliased):
    (u_ref, wb_ref, ar_ref, ai_ref, wcr_ref, wci_ref, d_ref, wglu_ref, bglu_ref,
     h0r_ref, h0i_ref) = refs[:11]
    o_ref, hr_ref, hi_ref, xr_ref, xi_ref = refs[12:] if aliased else refs[11:]

    @pl.when(pl.program_id(0) == 0)
    def _():
        hr_ref[...] = h0r_ref[...]
        hi_ref[...] = h0i_ref[...]

    u = u_ref[...]
    ub = u.astype(BF16)
    n_state = xr_ref.shape[1]
    cw = n_state // S5_COLS
    for c in range(S5_COLS):
        xc = _dot(ub[:, c * LANES:(c + 1) * LANES], wb_ref[c])
        xr_ref[:, c * cw:(c + 1) * cw] = xc[:, :cw]
        xi_ref[:, c * cw:(c + 1) * cw] = xc[:, cw:]

    lane_group = S5_SCAN_LANES
    for lo_ in range(0, n_state, lane_group):
        ls = pl.ds(lo_, lane_group)
        ar = ar_ref[:, ls]
        ai = ai_ref[:, ls]
        if tc == 1:
            hr, hi = hr_ref[:, ls], hi_ref[:, ls]
            nr = ar * hr - ai * hi + xr_ref[:, ls]
            ni = ar * hi + ai * hr + xi_ref[:, ls]
            xr_ref[:, ls] = nr
            xi_ref[:, ls] = ni
        else:
            lower = lax.broadcasted_iota(jnp.int32, (SUBLANES, lane_group), 0) < nb

            def pair(k, carry, ls=ls, ar=ar, ai=ai, lower=lower):
                hr, hi = carry
                rows = pl.ds(pl.multiple_of(k * SUBLANES, SUBLANES), SUBLANES)
                x_r, x_i = xr_ref[rows, ls], xi_ref[rows, ls]
                ar_ = ar * hr - ai * hi + x_r
                ai_ = ar * hi + ai * hr + x_i
                sr, si = pltpu.roll(ar_, nb, 0), pltpu.roll(ai_, nb, 0)
                br_ = ar * sr - ai * si + x_r
                bi_ = ar * si + ai * sr + x_i
                xr_ref[rows, ls] = jnp.where(lower, ar_, br_)
                xi_ref[rows, ls] = jnp.where(lower, ai_, bi_)
                return pltpu.roll(br_, nb, 0), pltpu.roll(bi_, nb, 0)

            nr, ni = lax.fori_loop(0, tc // 2, pair, (hr_ref[:, ls], hi_ref[:, ls]), unroll=2)
        hr_ref[:, ls] = nr
        hi_ref[:, ls] = ni

    ys = []
    for c in range(S5_COLS):
        cs = pl.ds(c * cw, cw)
        ys.append(_dot(xr_ref[:, cs].astype(BF16), wcr_ref[c])
                  + _dot(xi_ref[:, cs].astype(BF16), wci_ref[c]))
    y = jnp.concatenate(ys, axis=1) + d_ref[...] * u
    y = _gelu(y)
    z = _dot(y.astype(BF16), wglu_ref[...]) + bglu_ref[...]
    o_ref[...] = (y * jax.nn.sigmoid(z)).astype(BF16)


def _s5_call(z, prm, h0r, h0i, nb, tc, row_block0, n_chunks, y_prev=None):
    m = z.shape[0]
    d_a = prm["d"].shape[1]
    n_state = prm["ar"].shape[1]
    r = nb * tc
    sr = h0r.shape[0]
    fix2 = lambda i: (0, 0)
    fix3 = lambda i: (0, 0, 0)
    blk = lambda i: (row_block0 + i, 0)
    ins = [z, prm["wb"], prm["ar"], prm["ai"], prm["wcr"], prm["wci"], prm["d"], prm["wglu"],
           prm["bglu"], h0r, h0i]
    specs = [pl.BlockSpec((r, d_a), blk),
             pl.BlockSpec(prm["wb"].shape, fix3),
             pl.BlockSpec((1, n_state), fix2), pl.BlockSpec((1, n_state), fix2),
             pl.BlockSpec(prm["wcr"].shape, fix3), pl.BlockSpec(prm["wci"].shape, fix3),
             pl.BlockSpec((1, d_a), fix2), pl.BlockSpec((d_a, d_a), fix2),
             pl.BlockSpec((1, d_a), fix2),
             pl.BlockSpec((sr, n_state), fix2), pl.BlockSpec((sr, n_state), fix2)]
    aliases = {}
    if y_prev is not None:
        ins.append(y_prev)
        specs.append(pl.BlockSpec(memory_space=pl.ANY))
        aliases = {len(ins) - 1: 0}
    return pl.pallas_call(
        functools.partial(_s5_kernel, nb=nb, tc=tc, aliased=y_prev is not None),
        out_shape=(jax.ShapeDtypeStruct((m, d_a), BF16),
                   jax.ShapeDtypeStruct((sr, n_state), F32),
                   jax.ShapeDtypeStruct((sr, n_state), F32)),
        grid=(n_chunks,),
        in_specs=specs,
        out_specs=(pl.BlockSpec((r, d_a), blk),
                   pl.BlockSpec((sr, n_state), fix2), pl.BlockSpec((sr, n_state), fix2)),
        scratch_shapes=[pltpu.VMEM((r, n_state), F32), pltpu.VMEM((r, n_state), F32)],
        input_output_aliases=aliases,
        compiler_params=_cp("arbitrary"),
        name="s5",
    )(*ins)


def _s5_params(lam_re, lam_im, b_re, b_im, c_re, c_im, d_skip, log_step, w_glu, b_glu):
    g, n = lam_re.shape
    gs = b_re.shape[2]
    step = jnp.exp(log_step)[:, None]
    mag = jnp.exp(lam_re * step)
    abar_r, abar_i = mag * jnp.cos(lam_im * step), mag * jnp.sin(lam_im * step)
    den = lam_re * lam_re + lam_im * lam_im
    pr, pim = abar_r - 1.0, abar_i
    zr = (pr * lam_re + pim * lam_im) / den
    zi = (pim * lam_re - pr * lam_im) / den
    bbr = zr[..., None] * b_re - zi[..., None] * b_im
    bbi = zr[..., None] * b_im + zi[..., None] * b_re
    gl = LANES // gs
    nc = g // gl

    def in_mat(bb):
        a = jnp.transpose(bb.reshape(nc, gl, n, gs), (0, 1, 3, 2)).reshape(nc, gl * gs, n)
        return jnp.tile(a, (1, 1, gl)) * _block_diag_mask(gl, gs, n)

    def out_mat(cc):
        a = jnp.transpose(cc.reshape(nc, gl, gs, n), (0, 1, 3, 2)).reshape(nc, gl * n, gs)
        return jnp.tile(a, (1, 1, gl)) * _block_diag_mask(gl, n, gs)

    return dict(
        wb=jnp.concatenate([in_mat(bbr), in_mat(bbi)], axis=-1).astype(BF16),
        wcr=out_mat(c_re).astype(BF16),
        wci=out_mat(-c_im).astype(BF16),
        ar=abar_r.reshape(1, g * n), ai=abar_i.reshape(1, g * n),
        d=d_skip.reshape(1, -1), wglu=w_glu.astype(BF16), bglu=b_glu.reshape(1, -1))


def _log_f(fz, log_lb, log1m_lb):
    b = log1m_lb + _log_sigmoid(fz)
    a = jnp.broadcast_to(log_lb, b.shape)
    return jnp.maximum(a, b) + jnp.log1p(jnp.exp(-jnp.abs(a - b)))


def _hgrn_consts(tc, nb):
    r = tc * nb
    levels = int(np.log2(tc))
    t = np.arange(r) // nb
    b = np.arange(r) % nb
    same_b = b[:, None] == b[None, :]
    tt, ts = t[:, None], t[None, :]
    mask = np.zeros((levels + 1, r, r), np.float32)
    for l in range(levels):
        blk, half = 2 << l, 1 << l
        upper = (t % blk) >= half
        mask[l] = same_b & ((tt // blk) == (ts // blk)) & upper[:, None] & ~upper[None, :]
    expo = (same_b & (ts <= tt)).astype(np.float32)
    mask[levels] = np.eye(r, dtype=np.float32)
    bmask = np.zeros((nb, r, LANES), np.float32)
    for k in range(nb):
        bmask[k, b == k, :] = 1.0
    return expo, mask, bmask


def _hgrn_kernel(q_ref, fz_ref, v_ref, g_ref, llb_ref, l1m_ref, om_ref, gw_ref, expo_ref, mask_ref,
                 bm_ref, s0_ref, o_ref, st_ref, sc_ref, gs_ref, lf_ref, *, nb):
    levels = mask_ref.shape[0] - 1
    hd = HG_DIM
    r = q_ref.shape[0]

    @pl.when(pl.program_id(0) == 0)
    def _():
        st_ref[...] = s0_ref[...]
        sc_ref[...] = jnp.zeros_like(sc_ref)

    lf = _log_f(fz_ref[...], llb_ref[...], l1m_ref[...])
    lf_ref[...] = lf
    lf_hi = lf.astype(BF16)
    lf_lo = (lf - lf_hi.astype(F32)).astype(BF16)
    gs_ref[...] = _dot(expo_ref[...], lf_hi) + _dot(expo_ref[...], lf_lo)

    tiles = r // SUBLANES
    lower = lax.broadcasted_iota(jnp.int32, (1, SUBLANES, 1), 1) < nb

    def both_halves(t):
        return jnp.where(lower, pltpu.roll(t, nb, 1), t)

    gw = gw_ref[...]
    for h in range(HG_HEADS):
        hs = slice(h * hd, (h + 1) * hd)
        q = q_ref[:, hs]
        kk = om_ref[:, hs] * jax.nn.sigmoid(-fz_ref[:, hs])
        gcum = gs_ref[:, hs]
        g3 = gcum.reshape(tiles, SUBLANES, hd)

        def level_expo(l, g3=g3, hs=hs):
            if l == 0:
                return jnp.where(lower, 0.0, lf_ref[:, hs].reshape(tiles, SUBLANES, hd)).reshape(r, hd)
            span = 1 << l
            g4 = g3.reshape(tiles // span, span, SUBLANES, hd)
            split = both_halves(g4[:, span // 2 - 1])
            return (-jnp.abs(g4 - split[:, None])).reshape(r, hd)

        halves = (slice(0, r // 2), slice(r // 2, r))
        qb, kb = q.astype(BF16), kk.astype(BF16)
        acc = [mask_ref[levels, hv, hv] * _dot_nt(qb[hv], kb[hv]) for hv in halves]
        for l in range(levels - 1):
            e = jnp.exp(level_expo(l))
            qt, kt = (q * e).astype(BF16), (kk * e).astype(BF16)
            acc = [a + mask_ref[l, hv, hv] * _dot_nt(qt[hv], kt[hv]) for a, hv in zip(acc, halves)]
        e = jnp.exp(level_expo(levels - 1))
        lo_, hi_ = halves
        sc_ref[lo_, lo_] = acc[0]
        sc_ref[hi_, hi_] = acc[1]
        sc_ref[hi_, lo_] = mask_ref[levels - 1, hi_, lo_] * _dot_nt((q[hi_] * e[hi_]).astype(BF16),
                                                                   (kk[lo_] * e[lo_]).astype(BF16))

        qg = (q * jnp.exp(gcum)).astype(BF16)
        to_end = (both_halves(g3[tiles - 1:tiles]) - g3).reshape(r, hd)
        kend = (kk * jnp.exp(to_end)).astype(BF16)
        v = v_ref[:, hs]
        st = st_ref[h]
        o = _dot(sc_ref[...].astype(BF16), v.astype(BF16))
        oi = _dot_nt(qg, st.astype(BF16))
        for k in range(nb):
            o = o + bm_ref[k] * oi[:, k * hd:(k + 1) * hd]
        vcat = jnp.concatenate([(v * bm_ref[k]).astype(BF16) for k in range(nb)], axis=1)
        upd = _dot_tn(vcat, kend)
        dec = jnp.concatenate(
            [jnp.broadcast_to(jnp.exp(gs_ref[r - nb + k:r - nb + k + 1, hs]), (hd, hd)) for k in range(nb)],
            axis=0)
        st_ref[h] = dec * st + upd
        o = o * lax.rsqrt(jnp.mean(o * o, axis=-1, keepdims=True) + EPS) * gw
        o_ref[:, hs] = (o * _silu(g_ref[:, hs])).astype(BF16)


def _hgrn_call(z, lbp, gnorm_w, s0t, nb, n_chunks):
    m = z.shape[0]
    d_b = HG_HEADS * HG_DIM
    r = HG_CHUNK_T * nb
    expo, mask, bmask = _hgrn_consts(HG_CHUNK_T, nb)
    fix2 = lambda i: (0, 0)
    fix3 = lambda i: (0, 0, 0)
    col = lambda c: pl.BlockSpec((r, d_b), functools.partial(lambda i, c: (i, c), c=c))
    vec = pl.BlockSpec((1, d_b), fix2)
    return pl.pallas_call(
        functools.partial(_hgrn_kernel, nb=nb),
        out_shape=(jax.ShapeDtypeStruct((m, d_b), BF16),
                   jax.ShapeDtypeStruct(s0t.shape, F32)),
        grid=(n_chunks,),
        in_specs=[col(1), col(2), col(3), col(4), vec, vec, vec,
                  pl.BlockSpec((1, HG_DIM), fix2),
                  pl.BlockSpec(expo.shape, fix2), pl.BlockSpec(mask.shape, fix3),
                  pl.BlockSpec(bmask.shape, fix3), pl.BlockSpec(s0t.shape, fix3)],
        out_specs=(pl.BlockSpec((r, d_b), lambda i: (i, 0)), pl.BlockSpec(s0t.shape, fix3)),
        scratch_shapes=[pltpu.VMEM((r, r), F32), pltpu.VMEM((r, d_b), F32), pltpu.VMEM((r, d_b), F32)],
        compiler_params=_cp("arbitrary"),
        name="hgrn",
    )(z, z, z, z, lbp["log_lb"], lbp["log1m_lb"], lbp["one_m_lb"], gnorm_w.reshape(1, -1),
      jnp.asarray(expo, BF16), jnp.asarray(mask, F32), jnp.asarray(bmask, F32), s0t)


def _hgrn_dec_kernel(*refs, aliased):
    q_ref, fz_ref, v_ref, g_ref, llb_ref, l1m_ref, om_ref, gw_ref, fsel_ref, qsel_ref, s_ref = refs[:11]
    o_ref, so_ref = refs[13:] if aliased else refs[11:]
    hd = HG_DIM
    sb = q_ref.shape[0]
    fz = fz_ref[...]
    f = jnp.exp(_log_f(fz, llb_ref[...], l1m_ref[...]))
    kk = om_ref[...] * jax.nn.sigmoid(-fz)
    v = v_ref[...]
    gw = gw_ref[...]
    parts = [p.astype(F32) for p in _split3(f)]
    parts += [kk.astype(BF16).astype(F32), q_ref[...].astype(BF16).astype(F32)]
    n_f = 3 * sb
    zpad = jnp.zeros((hd - len(parts) * sb, hd), F32)
    wide = sb * hd
    own = (lax.broadcasted_iota(jnp.int32, (sb, wide), 0)
           == lax.shift_right_logical(lax.broadcasted_iota(jnp.int32, (sb, wide), 1), HG_DIM_LOG2))
    outs = []
    for h in range(HG_HEADS):
        hs = slice(h * hd, (h + 1) * hd)
        pt = jnp.concatenate([p[:, hs] for p in parts] + [zpad], axis=0).T.astype(BF16)
        vt = jnp.where(own, jnp.concatenate([v[:, hs]] * sb, axis=1), 0.0)
        vall = jnp.concatenate([jnp.zeros((n_f, wide), F32), vt,
                                jnp.zeros((hd - n_f - sb, wide), F32)], axis=0).astype(BF16)
        prod = _dot(pt, jnp.concatenate([fsel_ref[...], vall, qsel_ref[...]], axis=1))
        o_rows = []
        for j in range(sb):
            js = slice(j * hd, (j + 1) * hd)
            s_new = prod[:, js] * s_ref[0, j, h] + prod[:, wide:2 * wide][:, js]
            so_ref[0, j, h] = s_new
            o_rows.append(jnp.sum(prod[:, 2 * wide:][:, js] * s_new, axis=0, keepdims=True))
        o = jnp.concatenate(o_rows, axis=0)
        outs.append(o * lax.rsqrt(jnp.mean(o * o, axis=-1, keepdims=True) + EPS) * gw)
    o_all = jnp.concatenate(outs, axis=1)
    o_ref[...] = (o_all * _silu(g_ref[...])).astype(BF16)


def _hgrn_dec_call(z, row0, n_rows, lbp, gnorm_w, s_all, j, y_prev, s_prev, sb=SUBLANES):
    d_b = HG_HEADS * HG_DIM
    rb0 = row0 // sb
    fix2 = lambda i: (0, 0)
    col = lambda c: pl.BlockSpec((sb, d_b), functools.partial(lambda i, c: (rb0 + i, c), c=c))
    vec = pl.BlockSpec((1, d_b), fix2)
    sblk = pl.BlockSpec((1, sb, HG_HEADS, HG_DIM, HG_DIM), lambda i: (j, i, 0, 0, 0))
    fsel = np.zeros((HG_DIM, sb * HG_DIM), np.float32)
    qsel = np.zeros((HG_DIM, sb * HG_DIM), np.float32)
    for jj in range(sb):
        fsel[jj:3 * sb:sb, jj * HG_DIM:(jj + 1) * HG_DIM] = 1.0
        qsel[4 * sb + jj, jj * HG_DIM:(jj + 1) * HG_DIM] = 1.0
    sel = pl.BlockSpec(fsel.shape, fix2)
    ins = [z, z, z, z, lbp["log_lb"], lbp["log1m_lb"], lbp["one_m_lb"], gnorm_w.reshape(1, -1),
           jnp.asarray(fsel, BF16), jnp.asarray(qsel, BF16), s_all, y_prev]
    specs = [col(1), col(2), col(3), col(4), vec, vec, vec, pl.BlockSpec((1, HG_DIM), fix2),
             sel, sel, sblk, pl.BlockSpec(memory_space=pl.ANY)]
    aliases = {11: 0}
    if s_prev is not None:
        ins.append(s_prev)
        specs.append(pl.BlockSpec(memory_space=pl.ANY))
        aliases[12] = 1
    else:
        ins.append(jnp.zeros((SUBLANES, LANES), F32))
        specs.append(pl.BlockSpec((SUBLANES, LANES), fix2))
    return pl.pallas_call(
        functools.partial(_hgrn_dec_kernel, aliased=True),
        out_shape=(jax.ShapeDtypeStruct(y_prev.shape, BF16), jax.ShapeDtypeStruct(s_all.shape, F32)),
        grid=(n_rows // sb,),
        in_specs=specs,
        out_specs=(pl.BlockSpec((sb, d_b), lambda i: (rb0 + i, 0)), sblk),
        input_output_aliases=aliases,
        compiler_params=_cp("parallel"),
        name="hgrn_dec",
    )(*ins)


def _lru_coeffs(conv_block, wg_ref, bga_ref, bgx_ref, lam_ref, a_ref, b_ref):
    sw = LRU_SUPER
    for j in range(a_ref.shape[1] // sw):
        cs = slice(j * sw, (j + 1) * sw)
        xj = conv_block(cs)
        gj = _dot(xj.astype(BF16), wg_ref[j])
        rg = jax.nn.sigmoid(gj[:, :sw] + bga_ref[:, cs])
        ig = jax.nn.sigmoid(gj[:, sw:] + bgx_ref[:, cs])
        log_a = LRU_C * rg * _log_sigmoid(lam_ref[:, cs])
        a = jnp.exp(log_a)
        a_ref[:, cs] = a
        b_ref[:, cs] = jnp.sqrt(-jnp.tanh(log_a) * (a * a + 1.0)) * ig * xj


def _lru_kernel(gb_ref, xr_ref, cw_ref, cb_ref, wg_ref, bga_ref, bgx_ref, lam_ref, c0_ref, h0_ref,
                x_ref, gt_ref, wo_ref, o_ref, hl_ref, xs_ref, a_ref, b_ref, *, nb):
    r = xr_ref.shape[0]
    hdr = xs_ref.shape[0] - r
    kw = cw_ref.shape[0]

    @pl.when(pl.program_id(0) == 0)
    def _():
        xs_ref[0:hdr, :] = c0_ref[...]
        hl_ref[...] = h0_ref[...]

    xs_ref[hdr:hdr + r, :] = xr_ref[...]
    n_win = hdr + r

    def conv_block(cs):
        win = xs_ref[:, cs]
        sh = pltpu.roll(win, n_win - nb, 0)
        taps = [sh[hdr - kw * nb:hdr - kw * nb + r], win[hdr - 2 * nb:hdr - 2 * nb + r],
                sh[hdr - 2 * nb:hdr - 2 * nb + r], win[hdr:hdr + r]]
        xc = cb_ref[:, cs] + taps[0] * cw_ref[0:1, cs]
        for j in range(1, kw):
            xc = xc + taps[j] * cw_ref[j:j + 1, cs]
        return xc

    _lru_coeffs(conv_block, wg_ref, bga_ref, bgx_ref, lam_ref, a_ref, b_ref)
    xs_ref[0:hdr, :] = xs_ref[r:r + hdr, :]

    lower = lax.broadcasted_iota(jnp.int32, (SUBLANES, a_ref.shape[1]), 0) < nb

    def pair(k, h):
        rows = pl.ds(pl.multiple_of(k * SUBLANES, SUBLANES), SUBLANES)
        a8 = a_ref[rows, :]
        b8 = b_ref[rows, :]
        h_a = a8 * h + b8
        h_b = a8 * pltpu.roll(h_a, nb, 0) + b8
        b_ref[rows, :] = jnp.where(lower, h_a, h_b)
        return pltpu.roll(h_b, nb, 0)

    hl_ref[...] = lax.fori_loop(0, r // SUBLANES, pair, hl_ref[...], unroll=2)
    y = (_gelu(gb_ref[...]) * b_ref[...]).astype(BF16)
    o_ref[...] = x_ref[...] + _mul_rows(_dot(y, wo_ref[0]), gt_ref[0])


def _lru_call(z, prm, conv0, h0, nb, n_chunks, x, mods, ns, l, w_bf, j):
    m, d = x.shape
    d_rnn = prm["lam"].shape[1]
    r = nb * LRU_CHUNK_T
    hdr = conv0.shape[0]
    fix2 = lambda i: (0, 0)
    fix3 = lambda i: (0, 0, 0)
    vec = pl.BlockSpec((1, d_rnn), fix2)
    return pl.pallas_call(
        functools.partial(_lru_kernel, nb=nb),
        out_shape=(jax.ShapeDtypeStruct((m, d), F32),
                   jax.ShapeDtypeStruct((SUBLANES, d_rnn), F32)),
        grid=(n_chunks,),
        in_specs=[pl.BlockSpec((r, d_rnn), lambda i: (i, 0)),
                  pl.BlockSpec((r, d_rnn), lambda i: (i, 1)),
                  pl.BlockSpec(prm["cw"].shape, fix2), vec,
                  pl.BlockSpec(prm["wg"].shape, fix3), vec, vec, vec,
                  pl.BlockSpec((hdr, d_rnn), fix2), pl.BlockSpec((SUBLANES, d_rnn), fix2),
                  pl.BlockSpec((r, d), lambda i: (i, 0)),
                  pl.BlockSpec((1, SUBLANES, d), lambda i: (l, ns // SUBLANES, 5)),
                  pl.BlockSpec((1, d_rnn, d), lambda i: (j, 0, 0), pipeline_mode=pl.Buffered(1))],
        out_specs=(pl.BlockSpec((r, d), lambda i: (i, 0)),
                   pl.BlockSpec((SUBLANES, d_rnn), fix2)),
        scratch_shapes=[pltpu.VMEM((hdr + r, d_rnn), F32), pltpu.VMEM((r, d_rnn), F32),
                        pltpu.VMEM((r, d_rnn), F32)],
        compiler_params=_cp("arbitrary"),
        name="lru",
    )(z, z, prm["cw"], prm["cb"], prm["wg"], prm["bga"], prm["bgx"], prm["lam"], conv0, h0, x, mods, w_bf)


def _lru_dec_kernel(gb_ref, xr_ref, buf_ref, cw_ref, cb_ref, wg_ref, bga_ref, bgx_ref, lam_ref,
                    h0_ref, x_ref, gt_ref, wo_ref, xp_ref, o_ref, h_ref, a_ref, b_ref):
    kw = cw_ref.shape[0]

    def conv_block(cs):
        xc = cb_ref[:, cs] + xr_ref[:, cs] * cw_ref[kw - 1:kw, cs]
        for j in range(kw - 1):
            xc = xc + buf_ref[j, :, cs] * cw_ref[j:j + 1, cs]
        return xc

    _lru_coeffs(conv_block, wg_ref, bga_ref, bgx_ref, lam_ref, a_ref, b_ref)
    h = a_ref[...] * h0_ref[...] + b_ref[...]
    h_ref[...] = h
    y = (_gelu(gb_ref[...]) * h).astype(BF16)
    o_ref[...] = x_ref[...] + _dot(y, wo_ref[0]) * gt_ref[0]


def _lru_dec_call(z, row0, prm, buf, h0, x, mods, l, w_bf, j, x_prev):
    n = h0.shape[0]
    d = x.shape[1]
    d_rnn = prm["lam"].shape[1]
    rb0 = row0 // n
    fix2 = lambda i: (0, 0)
    fix3 = lambda i: (0, 0, 0)
    vec = pl.BlockSpec((1, d_rnn), fix2)
    full = pl.BlockSpec((n, d_rnn), fix2)
    return pl.pallas_call(
        _lru_dec_kernel,
        out_shape=(jax.ShapeDtypeStruct(x_prev.shape, F32), jax.ShapeDtypeStruct((n, d_rnn), F32)),
        grid=(1,),
        in_specs=[pl.BlockSpec((n, d_rnn), lambda i: (rb0, 0)),
                  pl.BlockSpec((n, d_rnn), lambda i: (rb0, 1)),
                  pl.BlockSpec(buf.shape, fix3),
                  pl.BlockSpec(prm["cw"].shape, fix2), vec,
                  pl.BlockSpec(prm["wg"].shape, fix3), vec, vec, vec, full,
                  pl.BlockSpec((n, d), lambda i: (rb0, 0)),
                  pl.BlockSpec((1, n, d), lambda i: (l, 0, 5)),
                  pl.BlockSpec((1, d_rnn, d), lambda i: (j, 0, 0)),
                  pl.BlockSpec(memory_space=pl.ANY)],
        out_specs=(pl.BlockSpec((n, d), lambda i: (rb0, 0)), full),
        scratch_shapes=[pltpu.VMEM((n, d_rnn), F32), pltpu.VMEM((n, d_rnn), F32)],
        input_output_aliases={13: 0},
        compiler_params=_cp("arbitrary"),
        name="lru_dec",
    )(z, z, buf, prm["cw"], prm["cb"], prm["wg"], prm["bga"], prm["bgx"], prm["lam"], h0, x, mods, w_bf, x_prev)


def _lru_params(conv_w, conv_b, w_ga, b_ga, w_gx, b_gx, lam):
    nblk, bs = w_ga.shape[0], w_ga.shape[1]
    per = LRU_SUPER // bs
    ns = nblk // per

    def sup(w):
        return jnp.tile(w.reshape(ns, per * bs, bs), (1, 1, per)) * _block_diag_mask(per, bs, bs)

    return dict(cw=conv_w, cb=conv_b.reshape(1, -1),
                wg=jnp.concatenate([sup(w_ga), sup(w_gx)], axis=-1).astype(BF16),
                bga=b_ga.reshape(1, -1), bgx=b_gx.reshape(1, -1), lam=lam.reshape(1, -1))


def kernel(x_prompt, x_sample, state_s5_re, state_s5_im, state_hgrn, state_lru, state_conv,
           c_prompt, c_sample, norm_w, final_norm_w, w_ada, b_ada, w_ffn_gu, w_ffn_d,
           w_in_ab, s5_lam_re, s5_lam_im, s5_b_re, s5_b_im, s5_c_re, s5_c_im, s5_d, s5_log_step,
           s5_w_glu, s5_b_glu, hg_lb_logits, hg_norm_w, w_out_ab, w_in_c, conv_w, conv_b,
           w_gate_a, b_gate_a, w_gate_x, b_gate_x, lru_lambda, w_out_c):
    bsz, seq, d = x_prompt.shape
    nsm = x_sample.shape[0]
    depth = w_ada.shape[0]
    n_ab, n_c = w_in_ab.shape[0], w_in_c.shape[0]
    g_a, n_a = s5_lam_re.shape[1], s5_lam_re.shape[2]
    n_state = g_a * n_a
    d_rnn = lru_lambda.shape[1]
    kw = conv_w.shape[1]
    mp_rows = seq * bsz
    assert 2 * bsz == SUBLANES and mp_rows % nsm == 0
    assert all((mp_rows + nsm) % t == 0 and t > nsm for t in (ROW_TILE, IN_ROW_TILE, OUT_ROW_TILE))

    s5p = [_s5_params(s5_lam_re[j], s5_lam_im[j], s5_b_re[j], s5_b_im[j], s5_c_re[j], s5_c_im[j],
                      s5_d[j], s5_log_step[j], s5_w_glu[j], s5_b_glu[j]) for j in range(n_ab)]
    lb_all = jnp.cumsum(jax.nn.softmax(hg_lb_logits.astype(F32), axis=0), axis=0)
    lb_all = lb_all - lb_all[0:1]
    lbp = [dict(log_lb=jnp.log(lb_all[j]).reshape(1, -1),
                log1m_lb=jnp.log1p(-lb_all[j]).reshape(1, -1),
                one_m_lb=(1.0 - lb_all[j]).reshape(1, -1)) for j in range(n_ab)]
    lrup = [_lru_params(conv_w[j], conv_b[j], w_gate_a[j], b_gate_a[j], w_gate_x[j], b_gate_x[j],
                        lru_lambda[j]) for j in range(n_c)]
    w_out_ab_bf = w_out_ab.astype(BF16)
    w_out_c_bf = w_out_c.astype(BF16)
    norm_w = norm_w.reshape(depth * 3, 1, d)

    c_all = jnp.concatenate([c_sample] + [c_prompt] * (SUBLANES // bsz), axis=0)
    mods = _ada_call(c_all, w_ada, b_ada)

    x = _pack_call(x_prompt, x_sample.reshape(nsm, d), FINAL_TILE_T)
    zeros_state = jnp.zeros((SUBLANES, n_state), F32)
    s5r_p, s5i_p, hg_p, lru_p, conv_p = [], [], [], [], []
    s5r_s, s5i_s, lru_s, conv_s = [], [], [], []
    hg_s = None
    for l in range(depth):
        j = l // 2
        x = _ffn_call(x, norm_w, mods, nsm, w_ffn_gu, w_ffn_d, l, 0, 0, ROW_TILE, FFN_COL_TILE)
        if l % 2 == 0:
            z = _inproj_call(x, norm_w, mods, nsm, w_in_ab, l, j, IN_ROW_TILE, IN_COL_TILE)
            ya, hr, hi = _s5_call(z, s5p[j], zeros_state, zeros_state, bsz, S5_CHUNK_T,
                                  0, seq // S5_CHUNK_T)
            ya, hrs, his = _s5_call(z, s5p[j], state_s5_re[j].reshape(nsm, n_state),
                                    state_s5_im[j].reshape(nsm, n_state), nsm, 1,
                                    mp_rows // nsm, 1, y_prev=ya)
            yb, hg = _hgrn_call(z, lbp[j], hg_norm_w[j],
                                jnp.zeros((HG_HEADS, bsz * HG_DIM, HG_DIM), F32), bsz, seq // HG_CHUNK_T)
            yb, hg_s = _hgrn_dec_call(z, mp_rows, nsm, lbp[j], hg_norm_w[j], state_hgrn, j, yb, hg_s)
            s5r_p.append(hr[:bsz].reshape(bsz, g_a, n_a))
            s5i_p.append(hi[:bsz].reshape(bsz, g_a, n_a))
            s5r_s.append(hrs.reshape(nsm, g_a, n_a))
            s5i_s.append(his.reshape(nsm, g_a, n_a))
            hg_p.append(jnp.transpose(hg.reshape(HG_HEADS, bsz, HG_DIM, HG_DIM), (1, 0, 3, 2)))
            x = _outproj_call(x, mods, nsm, l, [ya, yb], w_out_ab_bf, j, OUT_ROW_TILE)
        else:
            z = _inproj_call(x, norm_w, mods, nsm, w_in_c, l, j, IN_ROW_TILE, IN_COL_TILE)
            xn, hl = _lru_call(z, lrup[j], jnp.zeros((4 * bsz, d_rnn), F32),
                               jnp.zeros((SUBLANES, d_rnn), F32), bsz, seq // LRU_CHUNK_T,
                               x, mods, nsm, l, w_out_c_bf, j)
            x, hls = _lru_dec_call(z, mp_rows, lrup[j], jnp.transpose(state_conv[j], (1, 0, 2)),
                                   state_lru[j], x, mods, l, w_out_c_bf, j, xn)
            lru_p.append(hl[:bsz])
            lru_s.append(hls)
            tail = z[mp_rows - (kw - 1) * bsz:mp_rows, d_rnn:]
            conv_p.append(jnp.transpose(tail.reshape(kw - 1, bsz, d_rnn), (1, 0, 2)))
            conv_s.append(jnp.concatenate([state_conv[j][:, 1:], z[mp_rows:, None, d_rnn:]], axis=1))
        x = _ffn_call(x, norm_w, mods, nsm, w_ffn_gu, w_ffn_d, l, 1, 2, ROW_TILE, FFN_COL_TILE)
    y_prompt, y_sample = _final_norm_call(x, final_norm_w.reshape(1, -1), bsz, seq, nsm, FINAL_TILE_T)
    return (y_prompt, y_sample.reshape(nsm, 1, d), jnp.stack(s5r_p), jnp.stack(s5i_p), jnp.stack(hg_p),
            jnp.stack(lru_p), jnp.stack(conv_p),
            jnp.stack(s5r_s), jnp.stack(s5i_s), hg_s, jnp.stack(lru_s), jnp.stack(conv_s))
```

```python
import functools

import numpy as np
import jax
import jax.numpy as jnp
from jax import lax
from jax.experimental import pallas as pl
from jax.experimental.pallas import tpu as pltpu

F32 = jnp.float32
BF16 = jnp.bfloat16
EPS = 1e-6
LRU_C = 8.0

VMEM_LIMIT_BYTES = 60 * 1024 * 1024
SUBLANES = 8
LANES = 128
BF16_ROWS = 16

MXU_COLS = 256

S5_COLS = 8
S5_CHUNK_T = 128
S5_SCAN_LANES = 1024
HG_HEADS = 8
HG_DIM = 128
HG_DIM_LOG2 = 7
HG_CHUNK_T = 64
LRU_CHUNK_T = 64
LRU_SUPER = 640
ROW_TILE = 1040
FFN_COL_TILE = MXU_COLS
OUT_ROW_TILE = 640
IN_ROW_TILE = 1664
IN_COL_TILE = 2 * MXU_COLS
ADA_COL_TILE = 2048
FINAL_TILE_T = 256


def _cp(*sem):
    return pltpu.CompilerParams(dimension_semantics=sem, vmem_limit_bytes=VMEM_LIMIT_BYTES)


def _dot(a, b):
    return jnp.dot(a, b, preferred_element_type=F32)


def _dot_nt(a, b):
    return lax.dot_general(a, b, (((1,), (1,)), ((), ())), preferred_element_type=F32)


def _dot_tn(a, b):
    return lax.dot_general(a, b, (((0,), (0,)), ((), ())), preferred_element_type=F32)


def _silu(x):
    return x * jax.nn.sigmoid(x)


def _gelu(x):
    return jax.nn.gelu(x, approximate=True)


def _log_sigmoid(x):
    return jnp.minimum(x, 0.0) - jnp.log1p(jnp.exp(-jnp.abs(x)))


def _block_diag_mask(nblk, rows, cols):
    r = np.arange(nblk * rows)[:, None] // rows
    c = np.arange(nblk * cols)[None, :] // cols
    return jnp.asarray(r == c, F32)


def _split3(x):
    hi = x.astype(BF16)
    r1 = x - hi.astype(F32)
    mid = r1.astype(BF16)
    lo = (r1 - mid.astype(F32)).astype(BF16)
    return hi, mid, lo


def _fma_rows(y, mul, add):
    rm = mul.shape[0]
    r, d = y.shape
    if rm == r:
        return y * mul + add
    y3 = y.reshape(r // rm, rm, d)
    return (y3 * mul[None] + add[None]).reshape(r, d)


def _mul_rows(y, mul):
    rm = mul.shape[0]
    r, d = y.shape
    if rm == r:
        return y * mul
    return (y.reshape(r // rm, rm, d) * mul[None]).reshape(r, d)


def _norm_mod_rows(x_ref, nw_ref, sh_ref, sc_ref, h_ref, row0, nrows):
    slab = BF16_ROWS
    per_row = sh_ref.shape[0] > SUBLANES
    nw = nw_ref[...]

    def body(s, carry):
        r0 = pl.multiple_of(row0 + s * slab, slab)
        x = x_ref[pl.ds(r0, slab), :]
        ms = jnp.mean(x * x, axis=-1, keepdims=True)
        y = x * lax.rsqrt(ms + EPS)
        if per_row:
            m0 = pl.multiple_of(s * slab, slab)
            h = y * nw * (1.0 + sc_ref[pl.ds(m0, slab), :]) + sh_ref[pl.ds(m0, slab), :]
        else:
            h = _fma_rows(y, nw * (1.0 + sc_ref[...]), sh_ref[...])
        h_ref[pl.ds(r0, slab), :] = h.astype(BF16)
        return carry

    lax.fori_loop(0, nrows // slab, body, 0, unroll=8)


def _residual_rows(o_ref, x_ref, acc, gt_ref, gts_ref, scale):
    tm = o_ref.shape[0]
    ns = gts_ref.shape[0]
    np_ = tm - ns
    is_last = pl.program_id(0) == pl.num_programs(0) - 1
    gp = scale * gt_ref[...]
    o_ref[0:np_, :] = x_ref[0:np_, :] + _mul_rows(acc(0, np_), gp)

    @pl.when(jnp.logical_not(is_last))
    def _():
        o_ref[np_:tm, :] = x_ref[np_:tm, :] + _mul_rows(acc(np_, tm), gp)

    @pl.when(is_last)
    def _():
        o_ref[np_:tm, :] = x_ref[np_:tm, :] + acc(np_, tm) * (scale * gts_ref[...])


def _ada_kernel(c_ref, w_ref, b_ref, o_ref):
    sc = _silu(c_ref[...]).astype(BF16)
    o_ref[0] = _dot(sc, w_ref[0].astype(BF16)) + b_ref[0]


def _ada_call(c_all, w_ada, b_ada, tn=ADA_COL_TILE):
    depth, d, n = w_ada.shape
    r = c_all.shape[0]
    return pl.pallas_call(
        _ada_kernel,
        out_shape=jax.ShapeDtypeStruct((depth, r, n), F32),
        grid=(depth, n // tn),
        in_specs=[pl.BlockSpec((r, d), lambda l, j: (0, 0)),
                  pl.BlockSpec((1, d, tn), lambda l, j: (l, 0, j)),
                  pl.BlockSpec((1, 1, tn), lambda l, j: (l, 0, j))],
        out_specs=pl.BlockSpec((1, r, tn), lambda l, j: (l, 0, j)),
        compiler_params=_cp("parallel", "parallel"),
        name="ada",
    )(c_all, w_ada, b_ada.reshape(depth, 1, n))


def _ffn_kernel(x_ref, nw_ref, sh_ref, sc_ref, gt_ref, shs_ref, scs_ref, gts_ref,
                wg_ref, wu_ref, wd_ref, o_ref, h_ref):
    i, f = pl.program_id(0), pl.program_id(1)
    nw_ref, sh_ref, sc_ref, gt_ref, shs_ref, scs_ref, gts_ref = (
        r.at[0] for r in (nw_ref, sh_ref, sc_ref, gt_ref, shs_ref, scs_ref, gts_ref))
    tm = x_ref.shape[0]
    ns = shs_ref.shape[0]
    is_last = i == pl.num_programs(0) - 1

    @pl.when(f == 0)
    def _():
        _norm_mod_rows(x_ref, nw_ref, sh_ref, sc_ref, h_ref, 0, tm)
        o_ref[...] = jnp.zeros_like(o_ref)

    @pl.when(jnp.logical_and(f == 0, is_last))
    def _():
        _norm_mod_rows(x_ref, nw_ref, shs_ref, scs_ref, h_ref, tm - ns, ns)

    h = h_ref[...]
    g = _dot(h, wg_ref[0, 0].astype(BF16))
    u = _dot(h, wu_ref[0, 0].astype(BF16))
    a = (_silu(g) * u).astype(BF16)
    o_ref[...] += _dot(a, wd_ref[0, 0].astype(BF16))

    @pl.when(f == pl.num_programs(1) - 1)
    def _():
        _residual_rows(o_ref, x_ref, lambda a0, a1: o_ref[a0:a1, :], gt_ref, gts_ref, 0.5)


def _mod_specs(mods, ns, d, l, ks):
    def pat(k):
        return pl.BlockSpec((1, SUBLANES, d), lambda *g: (l, ns // SUBLANES, k))

    def smp(k):
        return pl.BlockSpec((1, ns, d), lambda *g: (l, 0, k))

    return [pat(k) for k in ks] + [smp(k) for k in ks], [mods] * (2 * len(ks))


def _ffn_call(x, norm_w, mods, ns, w_gu, w_d, l, s, sub, tm, tf):
    m, d = x.shape
    ff = w_d.shape[2]
    nf = ff // tf
    row = lambda i, f: (i, 0)
    mspecs, margs = _mod_specs(mods, ns, d, l, (3 * sub, 3 * sub + 1, 3 * sub + 2))
    return pl.pallas_call(
        _ffn_kernel,
        out_shape=jax.ShapeDtypeStruct((m, d), F32),
        grid=(m // tm, nf),
        in_specs=[pl.BlockSpec((tm, d), row), pl.BlockSpec((1, 1, d), lambda i, f: (3 * l + sub, 0, 0))]
                 + mspecs +
                 [pl.BlockSpec((1, 1, d, tf), lambda i, f: (l, s, 0, f)),
                  pl.BlockSpec((1, 1, d, tf), lambda i, f: (l, s, 0, f + nf)),
                  pl.BlockSpec((1, 1, tf, d), lambda i, f: (l, s, f, 0))],
        out_specs=pl.BlockSpec((tm, d), row),
        scratch_shapes=[pltpu.VMEM((tm, d), BF16)],
        compiler_params=_cp("parallel", "arbitrary"),
        name="ffn",
    )(x, norm_w, *margs, w_gu, w_gu, w_d)


def _inproj_kernel(x_ref, nw_ref, sh_ref, sc_ref, shs_ref, scs_ref, w_ref, o_ref, h_ref):
    i, k = pl.program_id(0), pl.program_id(1)
    nw_ref, sh_ref, sc_ref, shs_ref, scs_ref = (
        r.at[0] for r in (nw_ref, sh_ref, sc_ref, shs_ref, scs_ref))
    tm = x_ref.shape[0]
    ns = shs_ref.shape[0]

    @pl.when(k == 0)
    def _():
        _norm_mod_rows(x_ref, nw_ref, sh_ref, sc_ref, h_ref, 0, tm)

    @pl.when(jnp.logical_and(k == 0, i == pl.num_programs(0) - 1))
    def _():
        _norm_mod_rows(x_ref, nw_ref, shs_ref, scs_ref, h_ref, tm - ns, ns)

    h = h_ref[...]
    for c in range(0, o_ref.shape[1], MXU_COLS):
        o_ref[:, c:c + MXU_COLS] = _dot(h, w_ref[0, :, c:c + MXU_COLS].astype(BF16))


def _inproj_call(x, norm_w, mods, ns, w, l, j, tm, tn):
    m, d = x.shape
    n = w.shape[2]
    mspecs, margs = _mod_specs(mods, ns, d, l, (3, 4))
    return pl.pallas_call(
        _inproj_kernel,
        out_shape=jax.ShapeDtypeStruct((m, n), F32),
        grid=(m // tm, n // tn),
        in_specs=[pl.BlockSpec((tm, d), lambda i, k: (i, 0)),
                  pl.BlockSpec((1, 1, d), lambda i, k: (3 * l + 1, 0, 0))]
                 + mspecs + [pl.BlockSpec((1, d, tn), lambda i, k: (j, 0, k))],
        out_specs=pl.BlockSpec((tm, tn), lambda i, k: (i, k)),
        scratch_shapes=[pltpu.VMEM((tm, d), BF16)],
        compiler_params=_cp("parallel", "arbitrary"),
        name="inproj",
    )(x, norm_w, *margs, w)


def _outproj_kernel(*refs, n_in):
    x_ref, gt_ref, gts_ref = refs[0], refs[1].at[0], refs[2].at[0]
    a_refs = refs[3:3 + n_in]
    w_refs = refs[3 + n_in:3 + 2 * n_in]
    o_ref = refs[3 + 2 * n_in]
    acc = _dot(a_refs[0][...], w_refs[0][0])
    for a_ref, w_ref in zip(a_refs[1:], w_refs[1:]):
        acc = acc + _dot(a_ref[...], w_ref[0])
    _residual_rows(o_ref, x_ref, lambda a0, a1: acc[a0:a1, :], gt_ref, gts_ref, 1.0)


def _outproj_call(x, mods, ns, l, acts, w_bf, j, tm):
    m, d = x.shape
    n_in = len(acts)
    ka = acts[0].shape[1]
    row = lambda i: (i, 0)
    mspecs, margs = _mod_specs(mods, ns, d, l, (5,))
    in_specs = [pl.BlockSpec((tm, d), row)] + mspecs
    in_specs += [pl.BlockSpec((tm, ka), row) for _ in acts]
    in_specs += [pl.BlockSpec((1, ka, d), functools.partial(lambda i, p: (j, p, 0), p=p)) for p in range(n_in)]
    return pl.pallas_call(
        functools.partial(_outproj_kernel, n_in=n_in),
        out_shape=jax.ShapeDtypeStruct((m, d), F32),
        grid=(m // tm,),
        in_specs=in_specs,
        out_specs=pl.BlockSpec((tm, d), row),
        compiler_params=_cp("parallel"),
        name="outproj",
    )(x, *margs, *acts, *([w_bf] * n_in))


def _pack_kernel(xp_ref, xs_ref, o_ref, slab_ref, *, n_tiles):
    i = pl.program_id(0)
    nb, tt, d = xp_ref.shape

    @pl.when(i < n_tiles)
    def _():
        for c in range(d // LANES):
            cs = slice(c * LANES, (c + 1) * LANES)
            for b in range(nb):
                slab_ref[c, pl.ds(b, tt, stride=nb), :] = xp_ref[b, :, cs]
            o_ref[:, cs] = slab_ref[c]

    @pl.when(i == n_tiles)
    def _():
        o_ref[0:xs_ref.shape[0], :] = xs_ref[...]


def _pack_call(x_prompt, xs, tt):
    nb, seq, d = x_prompt.shape
    ns = xs.shape[0]
    n_tiles = seq // tt
    return pl.pallas_call(
        functools.partial(_pack_kernel, n_tiles=n_tiles),
        out_shape=jax.ShapeDtypeStruct((seq * nb + ns, d), F32),
        grid=(n_tiles + 1,),
        in_specs=[pl.BlockSpec((nb, tt, d), lambda i: (0, jnp.minimum(i, n_tiles - 1), 0)),
                  pl.BlockSpec((ns, d), lambda i: (0, 0))],
        out_specs=pl.BlockSpec((tt * nb, d), lambda i: (i, 0)),
        scratch_shapes=[pltpu.VMEM((d // LANES, tt * nb, LANES), F32)],
        compiler_params=_cp("arbitrary"),
        name="pack",
    )(x_prompt, xs)


def _final_norm_kernel(x_ref, w_ref, yp_ref, ys_ref, slab_ref, *, n_tiles):
    i = pl.program_id(0)
    nb, tt, d = yp_ref.shape
    w = w_ref[...]

    def inv_rms(x):
        return lax.rsqrt(jnp.mean(x * x, axis=-1, keepdims=True) + EPS)

    @pl.when(i < n_tiles)
    def _():
        inv = inv_rms(x_ref[...])
        for c in range(d // LANES):
            cs = slice(c * LANES, (c + 1) * LANES)
            slab_ref[c] = x_ref[:, cs] * inv * w[:, cs]
            for b in range(nb):
                yp_ref[b, :, cs] = slab_ref[c, pl.ds(b, tt, stride=nb), :]

    @pl.when(i == n_tiles)
    def _():
        x = x_ref[0:ys_ref.shape[0], :]
        ys_ref[...] = x * inv_rms(x) * w


def _final_norm_call(x, w, nb, seq, ns, tt):
    d = x.shape[1]
    n_tiles = seq // tt
    return pl.pallas_call(
        functools.partial(_final_norm_kernel, n_tiles=n_tiles),
        out_shape=(jax.ShapeDtypeStruct((nb, seq, d), F32), jax.ShapeDtypeStruct((ns, d), F32)),
        grid=(n_tiles + 1,),
        in_specs=[pl.BlockSpec((tt * nb, d), lambda i: (i, 0)),
                  pl.BlockSpec((1, d), lambda i: (0, 0))],
        out_specs=(pl.BlockSpec((nb, tt, d), lambda i: (0, jnp.minimum(i, n_tiles - 1), 0)),
                   pl.BlockSpec((ns, d), lambda i: (0, 0))),
        scratch_shapes=[pltpu.VMEM((d // LANES, tt * nb, LANES), F32)],
        compiler_params=_cp("arbitrary"),
        name="final_norm",
    )(x, w)


def _s5_kernel(*refs, nb, tc, aliased):
    (u_ref, wb_ref, ar_ref, ai_ref, wcr_ref, wci_ref, d_ref, wglu_ref, bglu_ref,
     h0r_ref, h0i_ref) = refs[:11]
    o_ref, hr_ref, hi_ref, xr_ref, xi_ref = refs[12:] if aliased else refs[11:]

    @pl.when(pl.program_id(0) == 0)
    def _():
        hr_ref[...] = h0r_ref[...]
        hi_ref[...] = h0i_ref[...]

    u = u_ref[...]
    ub = u.astype(BF16)
    n_state = xr_ref.shape[1]
    cw = n_state // S5_COLS
    for c in range(S5_COLS):
        xc = _dot(ub[:, c * LANES:(c + 1) * LANES], wb_ref[c])
        xr_ref[:, c * cw:(c + 1) * cw] = xc[:, :cw]
        xi_ref[:, c * cw:(c + 1) * cw] = xc[:, cw:]

    lane_group = S5_SCAN_LANES
    for lo_ in range(0, n_state, lane_group):
        ls = pl.ds(lo_, lane_group)
        ar = ar_ref[:, ls]
        ai = ai_ref[:, ls]
        if tc == 1:
            hr, hi = hr_ref[:, ls], hi_ref[:, ls]
            nr = ar * hr - ai * hi + xr_ref[:, ls]
            ni = ar * hi + ai * hr + xi_ref[:, ls]
            xr_ref[:, ls] = nr
            xi_ref[:, ls] = ni
        else:
            lower = lax.broadcasted_iota(jnp.int32, (SUBLANES, lane_group), 0) < nb

            def pair(k, carry, ls=ls, ar=ar, ai=ai, lower=lower):
                hr, hi = carry
                rows = pl.ds(pl.multiple_of(k * SUBLANES, SUBLANES), SUBLANES)
                x_r, x_i = xr_ref[rows, ls], xi_ref[rows, ls]
                ar_ = ar * hr - ai * hi + x_r
                ai_ = ar * hi + ai * hr + x_i
                sr, si = pltpu.roll(ar_, nb, 0), pltpu.roll(ai_, nb, 0)
                br_ = ar * sr - ai * si + x_r
                bi_ = ar * si + ai * sr + x_i
                xr_ref[rows, ls] = jnp.where(lower, ar_, br_)
                xi_ref[rows, ls] = jnp.where(lower, ai_, bi_)
                return pltpu.roll(br_, nb, 0), pltpu.roll(bi_, nb, 0)

            nr, ni = lax.fori_loop(0, tc // 2, pair, (hr_ref[:, ls], hi_ref[:, ls]), unroll=4)
        hr_ref[:, ls] = nr
        hi_ref[:, ls] = ni

    ys = []
    for c in range(S5_COLS):
        cs = pl.ds(c * cw, cw)
        ys.append(_dot(xr_ref[:, cs].astype(BF16), wcr_ref[c])
                  + _dot(xi_ref[:, cs].astype(BF16), wci_ref[c]))
    y = jnp.concatenate(ys, axis=1) + d_ref[...] * u
    y = _gelu(y)
    z = _dot(y.astype(BF16), wglu_ref[...]) + bglu_ref[...]
    o_ref[...] = (y * jax.nn.sigmoid(z)).astype(BF16)


def _s5_call(z, prm, h0r, h0i, nb, tc, row_block0, n_chunks, y_prev=None):
    m = z.shape[0]
    d_a = prm["d"].shape[1]
    n_state = prm["ar"].shape[1]
    r = nb * tc
    sr = h0r.shape[0]
    fix2 = lambda i: (0, 0)
    fix3 = lambda i: (0, 0, 0)
    blk = lambda i: (row_block0 + i, 0)
    ins = [z, prm["wb"], prm["ar"], prm["ai"], prm["wcr"], prm["wci"], prm["d"], prm["wglu"],
           prm["bglu"], h0r, h0i]
    specs = [pl.BlockSpec((r, d_a), blk),
             pl.BlockSpec(prm["wb"].shape, fix3),
             pl.BlockSpec((1, n_state), fix2), pl.BlockSpec((1, n_state), fix2),
             pl.BlockSpec(prm["wcr"].shape, fix3), pl.BlockSpec(prm["wci"].shape, fix3),
             pl.BlockSpec((1, d_a), fix2), pl.BlockSpec((d_a, d_a), fix2),
             pl.BlockSpec((1, d_a), fix2),
             pl.BlockSpec((sr, n_state), fix2), pl.BlockSpec((sr, n_state), fix2)]
    aliases = {}
    if y_prev is not None:
        ins.append(y_prev)
        specs.append(pl.BlockSpec(memory_space=pl.ANY))
        aliases = {len(ins) - 1: 0}
    return pl.pallas_call(
        functools.partial(_s5_kernel, nb=nb, tc=tc, aliased=y_prev is not None),
        out_shape=(jax.ShapeDtypeStruct((m, d_a), BF16),
                   jax.ShapeDtypeStruct((sr, n_state), F32),
                   jax.ShapeDtypeStruct((sr, n_state), F32)),
        grid=(n_chunks,),
        in_specs=specs,
        out_specs=(pl.BlockSpec((r, d_a), blk),
                   pl.BlockSpec((sr, n_state), fix2), pl.BlockSpec((sr, n_state), fix2)),
        scratch_shapes=[pltpu.VMEM((r, n_state), F32), pltpu.VMEM((r, n_state), F32)],
        input_output_aliases=aliases,
        compiler_params=_cp("arbitrary"),
        name="s5",
    )(*ins)


def _s5_params(lam_re, lam_im, b_re, b_im, c_re, c_im, d_skip, log_step, w_glu, b_glu):
    g, n = lam_re.shape
    gs = b_re.shape[2]
    step = jnp.exp(log_step)[:, None]
    mag = jnp.exp(lam_re * step)
    abar_r, abar_i = mag * jnp.cos(lam_im * step), mag * jnp.sin(lam_im * step)
    den = lam_re * lam_re + lam_im * lam_im
    pr, pim = abar_r - 1.0, abar_i
    zr = (pr * lam_re + pim * lam_im) / den
    zi = (pim * lam_re - pr * lam_im) / den
    bbr = zr[..., None] * b_re - zi[..., None] * b_im
    bbi = zr[..., None] * b_im + zi[..., None] * b_re
    gl = LANES // gs
    nc = g // gl

    def in_mat(bb):
        a = jnp.transpose(bb.reshape(nc, gl, n, gs), (0, 1, 3, 2)).reshape(nc, gl * gs, n)
        return jnp.tile(a, (1, 1, gl)) * _block_diag_mask(gl, gs, n)

    def out_mat(cc):
        a = jnp.transpose(cc.reshape(nc, gl, gs, n), (0, 1, 3, 2)).reshape(nc, gl * n, gs)
        return jnp.tile(a, (1, 1, gl)) * _block_diag_mask(gl, n, gs)

    return dict(
        wb=jnp.concatenate([in_mat(bbr), in_mat(bbi)], axis=-1).astype(BF16),
        wcr=out_mat(c_re).astype(BF16),
        wci=out_mat(-c_im).astype(BF16),
        ar=abar_r.reshape(1, g * n), ai=abar_i.reshape(1, g * n),
        d=d_skip.reshape(1, -1), wglu=w_glu.astype(BF16), bglu=b_glu.reshape(1, -1))


def _log_f(fz, log_lb, log1m_lb):
    b = log1m_lb + _log_sigmoid(fz)
    a = jnp.broadcast_to(log_lb, b.shape)
    return jnp.maximum(a, b) + jnp.log1p(jnp.exp(-jnp.abs(a - b)))


def _hgrn_consts(tc, nb):
    r = tc * nb
    levels = int(np.log2(tc))
    t = np.arange(r) // nb
    b = np.arange(r) % nb
    same_b = b[:, None] == b[None, :]
    tt, ts = t[:, None], t[None, :]
    mask = np.zeros((levels + 1, r, r), np.float32)
    for l in range(levels):
        blk, half = 2 << l, 1 << l
        upper = (t % blk) >= half
        mask[l] = same_b & ((tt // blk) == (ts // blk)) & upper[:, None] & ~upper[None, :]
    expo = (same_b & (ts <= tt)).astype(np.float32)
    mask[levels] = np.eye(r, dtype=np.float32)
    bmask = np.zeros((nb, r, LANES), np.float32)
    for k in range(nb):
        bmask[k, b == k, :] = 1.0
    return expo, mask, bmask


def _hgrn_kernel(q_ref, fz_ref, v_ref, g_ref, llb_ref, l1m_ref, om_ref, gw_ref, expo_ref, mask_ref,
                 bm_ref, s0_ref, o_ref, st_ref, sc_ref, gs_ref, lf_ref, *, nb):
    levels = mask_ref.shape[0] - 1
    hd = HG_DIM
    r = q_ref.shape[0]

    @pl.when(pl.program_id(0) == 0)
    def _():
        st_ref[...] = s0_ref[...]
        sc_ref[...] = jnp.zeros_like(sc_ref)

    lf = _log_f(fz_ref[...], llb_ref[...], l1m_ref[...])
    lf_ref[...] = lf
    lf_hi = lf.astype(BF16)
    lf_lo = (lf - lf_hi.astype(F32)).astype(BF16)
    gs_ref[...] = _dot(expo_ref[...], lf_hi) + _dot(expo_ref[...], lf_lo)

    tiles = r // SUBLANES
    lower = lax.broadcasted_iota(jnp.int32, (1, SUBLANES, 1), 1) < nb

    def both_halves(t):
        return jnp.where(lower, pltpu.roll(t, nb, 1), t)

    gw = gw_ref[...]
    for h in range(HG_HEADS):
        hs = slice(h * hd, (h + 1) * hd)
        q = q_ref[:, hs]
        kk = om_ref[:, hs] * jax.nn.sigmoid(-fz_ref[:, hs])
        gcum = gs_ref[:, hs]
        g3 = gcum.reshape(tiles, SUBLANES, hd)

        def level_expo(l, g3=g3, hs=hs):
            if l == 0:
                return jnp.where(lower, 0.0, lf_ref[:, hs].reshape(tiles, SUBLANES, hd)).reshape(r, hd)
            span = 1 << l
            g4 = g3.reshape(tiles // span, span, SUBLANES, hd)
            split = both_halves(g4[:, span // 2 - 1])
            return (-jnp.abs(g4 - split[:, None])).reshape(r, hd)

        halves = (slice(0, r // 2), slice(r // 2, r))
        qb, kb = q.astype(BF16), kk.astype(BF16)
        acc = [mask_ref[levels, hv, hv] * _dot_nt(qb[hv], kb[hv]) for hv in halves]
        for l in range(levels - 1):
            e = jnp.exp(level_expo(l))
            qt, kt = (q * e).astype(BF16), (kk * e).astype(BF16)
            acc = [a + mask_ref[l, hv, hv] * _dot_nt(qt[hv], kt[hv]) for a, hv in zip(acc, halves)]
        e = jnp.exp(level_expo(levels - 1))
        lo_, hi_ = halves
        sc_ref[lo_, lo_] = acc[0]
        sc_ref[hi_, hi_] = acc[1]
        sc_ref[hi_, lo_] = mask_ref[levels - 1, hi_, lo_] * _dot_nt((q[hi_] * e[hi_]).astype(BF16),
                                                                   (kk[lo_] * e[lo_]).astype(BF16))

        qg = (q * jnp.exp(gcum)).astype(BF16)
        to_end = (both_halves(g3[tiles - 1:tiles]) - g3).reshape(r, hd)
        kend = (kk * jnp.exp(to_end)).astype(BF16)
        v = v_ref[:, hs]
        st = st_ref[h]
        o = _dot(sc_ref[...].astype(BF16), v.astype(BF16))
        oi = _dot_nt(qg, st.astype(BF16))
        for k in range(nb):
            o = o + bm_ref[k] * oi[:, k * hd:(k + 1) * hd]
        vcat = jnp.concatenate([(v * bm_ref[k]).astype(BF16) for k in range(nb)], axis=1)
        upd = _dot_tn(vcat, kend)
        dec = jnp.concatenate(
            [jnp.broadcast_to(jnp.exp(gs_ref[r - nb + k:r - nb + k + 1, hs]), (hd, hd)) for k in range(nb)],
            axis=0)
        st_ref[h] = dec * st + upd
        o = o * lax.rsqrt(jnp.mean(o * o, axis=-1, keepdims=True) + EPS) * gw
        o_ref[:, hs] = (o * _silu(g_ref[:, hs])).astype(BF16)


def _hgrn_call(z, lbp, gnorm_w, s0t, nb, n_chunks):
    m = z.shape[0]
    d_b = HG_HEADS * HG_DIM
    r = HG_CHUNK_T * nb
    expo, mask, bmask = _hgrn_consts(HG_CHUNK_T, nb)
    fix2 = lambda i: (0, 0)
    fix3 = lambda i: (0, 0, 0)
    col = lambda c: pl.BlockSpec((r, d_b), functools.partial(lambda i, c: (i, c), c=c))
    vec = pl.BlockSpec((1, d_b), fix2)
    return pl.pallas_call(
        functools.partial(_hgrn_kernel, nb=nb),
        out_shape=(jax.ShapeDtypeStruct((m, d_b), BF16),
                   jax.ShapeDtypeStruct(s0t.shape, F32)),
        grid=(n_chunks,),
        in_specs=[col(1), col(2), col(3), col(4), vec, vec, vec,
                  pl.BlockSpec((1, HG_DIM), fix2),
                  pl.BlockSpec(expo.shape, fix2), pl.BlockSpec(mask.shape, fix3),
                  pl.BlockSpec(bmask.shape, fix3), pl.BlockSpec(s0t.shape, fix3)],
        out_specs=(pl.BlockSpec((r, d_b), lambda i: (i, 0)), pl.BlockSpec(s0t.shape, fix3)),
        scratch_shapes=[pltpu.VMEM((r, r), F32), pltpu.VMEM((r, d_b), F32), pltpu.VMEM((r, d_b), F32)],
        compiler_params=_cp("arbitrary"),
        name="hgrn",
    )(z, z, z, z, lbp["log_lb"], lbp["log1m_lb"], lbp["one_m_lb"], gnorm_w.reshape(1, -1),
      jnp.asarray(expo, BF16), jnp.asarray(mask, F32), jnp.asarray(bmask, F32), s0t)


def _hgrn_dec_kernel(*refs, aliased):
    q_ref, fz_ref, v_ref, g_ref, llb_ref, l1m_ref, om_ref, gw_ref, fsel_ref, qsel_ref, s_ref = refs[:11]
    o_ref, so_ref = refs[13:] if aliased else refs[11:]
    hd = HG_DIM
    sb = q_ref.shape[0]
    fz = fz_ref[...]
    f = jnp.exp(_log_f(fz, llb_ref[...], l1m_ref[...]))
    kk = om_ref[...] * jax.nn.sigmoid(-fz)
    v = v_ref[...]
    gw = gw_ref[...]
    parts = [p.astype(F32) for p in _split3(f)]
    parts += [kk.astype(BF16).astype(F32), q_ref[...].astype(BF16).astype(F32)]
    n_f = 3 * sb
    zpad = jnp.zeros((hd - len(parts) * sb, hd), F32)
    wide = sb * hd
    own = (lax.broadcasted_iota(jnp.int32, (sb, wide), 0)
           == lax.shift_right_logical(lax.broadcasted_iota(jnp.int32, (sb, wide), 1), HG_DIM_LOG2))
    outs = []
    for h in range(HG_HEADS):
        hs = slice(h * hd, (h + 1) * hd)
        pt = jnp.concatenate([p[:, hs] for p in parts] + [zpad], axis=0).T.astype(BF16)
        vt = jnp.where(own, jnp.concatenate([v[:, hs]] * sb, axis=1), 0.0)
        vall = jnp.concatenate([jnp.zeros((n_f, wide), F32), vt,
                                jnp.zeros((hd - n_f - sb, wide), F32)], axis=0).astype(BF16)
        prod = _dot(pt, jnp.concatenate([fsel_ref[...], vall, qsel_ref[...]], axis=1))
        o_rows = []
        for j in range(sb):
            js = slice(j * hd, (j + 1) * hd)
            s_new = prod[:, js] * s_ref[0, j, h] + prod[:, wide:2 * wide][:, js]
            so_ref[0, j, h] = s_new
            o_rows.append(jnp.sum(prod[:, 2 * wide:][:, js] * s_new, axis=0, keepdims=True))
        o = jnp.concatenate(o_rows, axis=0)
        outs.append(o * lax.rsqrt(jnp.mean(o * o, axis=-1, keepdims=True) + EPS) * gw)
    o_all = jnp.concatenate(outs, axis=1)
    o_ref[...] = (o_all * _silu(g_ref[...])).astype(BF16)


def _hgrn_dec_call(z, row0, n_rows, lbp, gnorm_w, s_all, j, y_prev, s_prev, sb=SUBLANES):
    d_b = HG_HEADS * HG_DIM
    rb0 = row0 // sb
    fix2 = lambda i: (0, 0)
    col = lambda c: pl.BlockSpec((sb, d_b), functools.partial(lambda i, c: (rb0 + i, c), c=c))
    vec = pl.BlockSpec((1, d_b), fix2)
    sblk = pl.BlockSpec((1, sb, HG_HEADS, HG_DIM, HG_DIM), lambda i: (j, i, 0, 0, 0))
    fsel = np.zeros((HG_DIM, sb * HG_DIM), np.float32)
    qsel = np.zeros((HG_DIM, sb * HG_DIM), np.float32)
    for jj in range(sb):
        fsel[jj:3 * sb:sb, jj * HG_DIM:(jj + 1) * HG_DIM] = 1.0
        qsel[4 * sb + jj, jj * HG_DIM:(jj + 1) * HG_DIM] = 1.0
    sel = pl.BlockSpec(fsel.shape, fix2)
    ins = [z, z, z, z, lbp["log_lb"], lbp["log1m_lb"], lbp["one_m_lb"], gnorm_w.reshape(1, -1),
           jnp.asarray(fsel, BF16), jnp.asarray(qsel, BF16), s_all, y_prev]
    specs = [col(1), col(2), col(3), col(4), vec, vec, vec, pl.BlockSpec((1, HG_DIM), fix2),
             sel, sel, sblk, pl.BlockSpec(memory_space=pl.ANY)]
    aliases = {11: 0}
    if s_prev is not None:
        ins.append(s_prev)
        specs.append(pl.BlockSpec(memory_space=pl.ANY))
        aliases[12] = 1
    else:
        ins.append(jnp.zeros((SUBLANES, LANES), F32))
        specs.append(pl.BlockSpec((SUBLANES, LANES), fix2))
    return pl.pallas_call(
        functools.partial(_hgrn_dec_kernel, aliased=True),
        out_shape=(jax.ShapeDtypeStruct(y_prev.shape, BF16), jax.ShapeDtypeStruct(s_all.shape, F32)),
        grid=(n_rows // sb,),
        in_specs=specs,
        out_specs=(pl.BlockSpec((sb, d_b), lambda i: (rb0 + i, 0)), sblk),
        input_output_aliases=aliases,
        compiler_params=_cp("parallel"),
        name="hgrn_dec",
    )(*ins)


def _lru_coeffs(conv_block, wg_ref, bga_ref, bgx_ref, lam_ref, a_ref, b_ref):
    sw = LRU_SUPER
    for j in range(a_ref.shape[1] // sw):
        cs = slice(j * sw, (j + 1) * sw)
        xj = conv_block(cs)
        gj = _dot(xj.astype(BF16), wg_ref[j])
        rg = jax.nn.sigmoid(gj[:, :sw] + bga_ref[:, cs])
        ig = jax.nn.sigmoid(gj[:, sw:] + bgx_ref[:, cs])
        log_a = LRU_C * rg * _log_sigmoid(lam_ref[:, cs])
        a = jnp.exp(log_a)
        a_ref[:, cs] = a
        b_ref[:, cs] = jnp.sqrt(-jnp.tanh(log_a) * (a * a + 1.0)) * ig * xj


def _lru_kernel(gb_ref, xr_ref, cw_ref, cb_ref, wg_ref, bga_ref, bgx_ref, lam_ref, c0_ref, h0_ref,
                x_ref, gt_ref, wo_ref, o_ref, hl_ref, xs_ref, a_ref, b_ref, *, nb):
    r = xr_ref.shape[0]
    hdr = xs_ref.shape[0] - r
    kw = cw_ref.shape[0]

    @pl.when(pl.program_id(0) == 0)
    def _():
        xs_ref[0:hdr, :] = c0_ref[...]
        hl_ref[...] = h0_ref[...]

    xs_ref[hdr:hdr + r, :] = xr_ref[...]
    n_win = hdr + r

    def conv_block(cs):
        win = xs_ref[:, cs]
        sh = pltpu.roll(win, n_win - nb, 0)
        taps = [sh[hdr - kw * nb:hdr - kw * nb + r], win[hdr - 2 * nb:hdr - 2 * nb + r],
                sh[hdr - 2 * nb:hdr - 2 * nb + r], win[hdr:hdr + r]]
        xc = cb_ref[:, cs] + taps[0] * cw_ref[0:1, cs]
        for j in range(1, kw):
            xc = xc + taps[j] * cw_ref[j:j + 1, cs]
        return xc

    _lru_coeffs(conv_block, wg_ref, bga_ref, bgx_ref, lam_ref, a_ref, b_ref)
    xs_ref[0:hdr, :] = xs_ref[r:r + hdr, :]

    lower = lax.broadcasted_iota(jnp.int32, (SUBLANES, a_ref.shape[1]), 0) < nb

    def pair(k, h):
        rows = pl.ds(pl.multiple_of(k * SUBLANES, SUBLANES), SUBLANES)
        a8 = a_ref[rows, :]
        b8 = b_ref[rows, :]
        h_a = a8 * h + b8
        h_b = a8 * pltpu.roll(h_a, nb, 0) + b8
        b_ref[rows, :] = jnp.where(lower, h_a, h_b)
        return pltpu.roll(h_b, nb, 0)

    hl_ref[...] = lax.fori_loop(0, r // SUBLANES, pair, hl_ref[...], unroll=2)
    y = (_gelu(gb_ref[...]) * b_ref[...]).astype(BF16)
    o_ref[...] = x_ref[...] + _mul_rows(_dot(y, wo_ref[0]), gt_ref[0])


def _lru_call(z, prm, conv0, h0, nb, n_chunks, x, mods, ns, l, w_bf, j):
    m, d = x.shape
    d_rnn = prm["lam"].shape[1]
    r = nb * LRU_CHUNK_T
    hdr = conv0.shape[0]
    fix2 = lambda i: (0, 0)
    fix3 = lambda i: (0, 0, 0)
    vec = pl.BlockSpec((1, d_rnn), fix2)
    return pl.pallas_call(
        functools.partial(_lru_kernel, nb=nb),
        out_shape=(jax.ShapeDtypeStruct((m, d), F32),
                   jax.ShapeDtypeStruct((SUBLANES, d_rnn), F32)),
        grid=(n_chunks,),
        in_specs=[pl.BlockSpec((r, d_rnn), lambda i: (i, 0)),
                  pl.BlockSpec((r, d_rnn), lambda i: (i, 1)),
                  pl.BlockSpec(prm["cw"].shape, fix2), vec,
                  pl.BlockSpec(prm["wg"].shape, fix3), vec, vec, vec,
                  pl.BlockSpec((hdr, d_rnn), fix2), pl.BlockSpec((SUBLANES, d_rnn), fix2),
                  pl.BlockSpec((r, d), lambda i: (i, 0)),
                  pl.BlockSpec((1, SUBLANES, d), lambda i: (l, ns // SUBLANES, 5)),
                  pl.BlockSpec((1, d_rnn, d), lambda i: (j, 0, 0), pipeline_mode=pl.Buffered(1))],
        out_specs=(pl.BlockSpec((r, d), lambda i: (i, 0)),
                   pl.BlockSpec((SUBLANES, d_rnn), fix2)),
        scratch_shapes=[pltpu.VMEM((hdr + r, d_rnn), F32), pltpu.VMEM((r, d_rnn), F32),
                        pltpu.VMEM((r, d_rnn), F32)],
        compiler_params=_cp("arbitrary"),
        name="lru",
    )(z, z, prm["cw"], prm["cb"], prm["wg"], prm["bga"], prm["bgx"], prm["lam"], conv0, h0, x, mods, w_bf)


def _lru_dec_kernel(gb_ref, xr_ref, buf_ref, cw_ref, cb_ref, wg_ref, bga_ref, bgx_ref, lam_ref,
                    h0_ref, x_ref, gt_ref, wo_ref, xp_ref, o_ref, h_ref, a_ref, b_ref):
    kw = cw_ref.shape[0]

    def conv_block(cs):
        xc = cb_ref[:, cs] + xr_ref[:, cs] * cw_ref[kw - 1:kw, cs]
        for j in range(kw - 1):
            xc = xc + buf_ref[j, :, cs] * cw_ref[j:j + 1, cs]
        return xc

    _lru_coeffs(conv_block, wg_ref, bga_ref, bgx_ref, lam_ref, a_ref, b_ref)
    h = a_ref[...] * h0_ref[...] + b_ref[...]
    h_ref[...] = h
    y = (_gelu(gb_ref[...]) * h).astype(BF16)
    o_ref[...] = x_ref[...] + _dot(y, wo_ref[0]) * gt_ref[0]


def _lru_dec_call(z, row0, prm, buf, h0, x, mods, l, w_bf, j, x_prev):
    n = h0.shape[0]
    d = x.shape[1]
    d_rnn = prm["lam"].shape[1]
    rb0 = row0 // n
    fix2 = lambda i: (0, 0)
    fix3 = lambda i: (0, 0, 0)
    vec = pl.BlockSpec((1, d_rnn), fix2)
    full = pl.BlockSpec((n, d_rnn), fix2)
    return pl.pallas_call(
        _lru_dec_kernel,
        out_shape=(jax.ShapeDtypeStruct(x_prev.shape, F32), jax.ShapeDtypeStruct((n, d_rnn), F32)),
        grid=(1,),
        in_specs=[pl.BlockSpec((n, d_rnn), lambda i: (rb0, 0)),
                  pl.BlockSpec((n, d_rnn), lambda i: (rb0, 1)),
                  pl.BlockSpec(buf.shape, fix3),
                  pl.BlockSpec(prm["cw"].shape, fix2), vec,
                  pl.BlockSpec(prm["wg"].shape, fix3), vec, vec, vec, full,
                  pl.BlockSpec((n, d), lambda i: (rb0, 0)),
                  pl.BlockSpec((1, n, d), lambda i: (l, 0, 5)),
                  pl.BlockSpec((1, d_rnn, d), lambda i: (j, 0, 0)),
                  pl.BlockSpec(memory_space=pl.ANY)],
        out_specs=(pl.BlockSpec((n, d), lambda i: (rb0, 0)), full),
        scratch_shapes=[pltpu.VMEM((n, d_rnn), F32), pltpu.VMEM((n, d_rnn), F32)],
        input_output_aliases={13: 0},
        compiler_params=_cp("arbitrary"),
        name="lru_dec",
    )(z, z, buf, prm["cw"], prm["cb"], prm["wg"], prm["bga"], prm["bgx"], prm["lam"], h0, x, mods, w_bf, x_prev)


def _lru_params(conv_w, conv_b, w_ga, b_ga, w_gx, b_gx, lam):
    nblk, bs = w_ga.shape[0], w_ga.shape[1]
    per = LRU_SUPER // bs
    ns = nblk // per

    def sup(w):
        return jnp.tile(w.reshape(ns, per * bs, bs), (1, 1, per)) * _block_diag_mask(per, bs, bs)

    return dict(cw=conv_w, cb=conv_b.reshape(1, -1),
                wg=jnp.concatenate([sup(w_ga), sup(w_gx)], axis=-1).astype(BF16),
                bga=b_ga.reshape(1, -1), bgx=b_gx.reshape(1, -1), lam=lam.reshape(1, -1))


def kernel(x_prompt, x_sample, state_s5_re, state_s5_im, state_hgrn, state_lru, state_conv,
           c_prompt, c_sample, norm_w, final_norm_w, w_ada, b_ada, w_ffn_gu, w_ffn_d,
           w_in_ab, s5_lam_re, s5_lam_im, s5_b_re, s5_b_im, s5_c_re, s5_c_im, s5_d, s5_log_step,
           s5_w_glu, s5_b_glu, hg_lb_logits, hg_norm_w, w_out_ab, w_in_c, conv_w, conv_b,
           w_gate_a, b_gate_a, w_gate_x, b_gate_x, lru_lambda, w_out_c):
    bsz, seq, d = x_prompt.shape
    nsm = x_sample.shape[0]
    depth = w_ada.shape[0]
    n_ab, n_c = w_in_ab.shape[0], w_in_c.shape[0]
    g_a, n_a = s5_lam_re.shape[1], s5_lam_re.shape[2]
    n_state = g_a * n_a
    d_rnn = lru_lambda.shape[1]
    kw = conv_w.shape[1]
    mp_rows = seq * bsz
    assert 2 * bsz == SUBLANES and mp_rows % nsm == 0
    assert all((mp_rows + nsm) % t == 0 and t > nsm for t in (ROW_TILE, IN_ROW_TILE, OUT_ROW_TILE))

    s5p = [_s5_params(s5_lam_re[j], s5_lam_im[j], s5_b_re[j], s5_b_im[j], s5_c_re[j], s5_c_im[j],
                      s5_d[j], s5_log_step[j], s5_w_glu[j], s5_b_glu[j]) for j in range(n_ab)]
    lb_all = jnp.cumsum(jax.nn.softmax(hg_lb_logits.astype(F32), axis=0), axis=0)
    lb_all = lb_all - lb_all[0:1]
    lbp = [dict(log_lb=jnp.log(lb_all[j]).reshape(1, -1),
                log1m_lb=jnp.log1p(-lb_all[j]).reshape(1, -1),
                one_m_lb=(1.0 - lb_all[j]).reshape(1, -1)) for j in range(n_ab)]
    lrup = [_lru_params(conv_w[j], conv_b[j], w_gate_a[j], b_gate_a[j], w_gate_x[j], b_gate_x[j],
                        lru_lambda[j]) for j in range(n_c)]
    w_out_ab_bf = w_out_ab.astype(BF16)
    w_out_c_bf = w_out_c.astype(BF16)
    norm_w = norm_w.reshape(depth * 3, 1, d)

    c_all = jnp.concatenate([c_sample] + [c_prompt] * (SUBLANES // bsz), axis=0)
    mods = _ada_call(c_all, w_ada, b_ada)

    x = _pack_call(x_prompt, x_sample.reshape(nsm, d), FINAL_TILE_T)
    zeros_state = jnp.zeros((SUBLANES, n_state), F32)
    s5r_p, s5i_p, hg_p, lru_p, conv_p = [], [], [], [], []
    s5r_s, s5i_s, lru_s, conv_s = [], [], [], []
    hg_s = None
    for l in range(depth):
        j = l // 2
        x = _ffn_call(x, norm_w, mods, nsm, w_ffn_gu, w_ffn_d, l, 0, 0, ROW_TILE, FFN_COL_TILE)
        if l % 2 == 0:
            z = _inproj_call(x, norm_w, mods, nsm, w_in_ab, l, j, IN_ROW_TILE, IN_COL_TILE)
            ya, hr, hi = _s5_call(z, s5p[j], zeros_state, zeros_state, bsz, S5_CHUNK_T,
                                  0, seq // S5_CHUNK_T)
            ya, hrs, his = _s5_call(z, s5p[j], state_s5_re[j].reshape(nsm, n_state),
                                    state_s5_im[j].reshape(nsm, n_state), nsm, 1,
                                    mp_rows // nsm, 1, y_prev=ya)
            yb, hg = _hgrn_call(z, lbp[j], hg_norm_w[j],
                                jnp.zeros((HG_HEADS, bsz * HG_DIM, HG_DIM), F32), bsz, seq // HG_CHUNK_T)
            yb, hg_s = _hgrn_dec_call(z, mp_rows, nsm, lbp[j], hg_norm_w[j], state_hgrn, j, yb, hg_s)
            s5r_p.append(hr[:bsz].reshape(bsz, g_a, n_a))
            s5i_p.append(hi[:bsz].reshape(bsz, g_a, n_a))
            s5r_s.append(hrs.reshape(nsm, g_a, n_a))
            s5i_s.append(his.reshape(nsm, g_a, n_a))
            hg_p.append(jnp.transpose(hg.reshape(HG_HEADS, bsz, HG_DIM, HG_DIM), (1, 0, 3, 2)))
            x = _outproj_call(x, mods, nsm, l, [ya, yb], w_out_ab_bf, j, OUT_ROW_TILE)
        else:
            z = _inproj_call(x, norm_w, mods, nsm, w_in_c, l, j, IN_ROW_TILE, IN_COL_TILE)
            xn, hl = _lru_call(z, lrup[j], jnp.zeros((4 * bsz, d_rnn), F32),
                               jnp.zeros((SUBLANES, d_rnn), F32), bsz, seq // LRU_CHUNK_T,
                               x, mods, nsm, l, w_out_c_bf, j)
            x, hls = _lru_dec_call(z, mp_rows, lrup[j], jnp.transpose(state_conv[j], (1, 0, 2)),
                                   state_lru[j], x, mods, l, w_out_c_bf, j, xn)
            lru_p.append(hl[:bsz])
            lru_s.append(hls)
            tail = z[mp_rows - (kw - 1) * bsz:mp_rows, d_rnn:]
            conv_p.append(jnp.transpose(tail.reshape(kw - 1, bsz, d_rnn), (1, 0, 2)))
            conv_s.append(jnp.concatenate([state_conv[j][:, 1:], z[mp_rows:, None, d_rnn:]], axis=1))
        x = _ffn_call(x, norm_w, mods, nsm, w_ffn_gu, w_ffn_d, l, 1, 2, ROW_TILE, FFN_COL_TILE)
    y_prompt, y_sample = _final_norm_call(x, final_norm_w.reshape(1, -1), bsz, seq, nsm, FINAL_TILE_T)
    return (y_prompt, y_sample.reshape(nsm, 1, d), jnp.stack(s5r_p), jnp.stack(s5i_p), jnp.stack(hg_p),
            jnp.stack(lru_p), jnp.stack(conv_p),
            jnp.stack(s5r_s), jnp.stack(s5i_s), hg_s, jnp.stack(lru_s), jnp.stack(conv_s))
```
